```python
import jax, jax.numpy as jnp
from jax import lax
import numpy as np


D_MODEL = 1024
BATCH = 16
SEQ = 2048
DEPTH = 1

GRID_W = 64
CTX_LEN = 256
HG_DK = 128
HG_HEADS = (D_MODEL // 2) // HG_DK
HG_DV = (D_MODEL // 2) // HG_HEADS
HG_WIDTH = HG_HEADS * HG_DV
RET_HEADS = 4
RET_DK = (D_MODEL // 2) // RET_HEADS
RET_DV = RET_DK
RET_WIDTH = RET_HEADS * RET_DV
MIX_WIDTH = HG_WIDTH + RET_WIDTH
CHUNK = 64
D_FF = ((8 * D_MODEL // 3 + 127) // 128) * 128
CONV_W = 3
ROPE_THETA = 10000.0
EPS = 1e-6
IN_SIZES = (HG_HEADS * HG_DK, HG_WIDTH, HG_HEADS * HG_DK, HG_HEADS * HG_DK, HG_WIDTH,
            RET_HEADS * RET_DK, RET_HEADS * RET_DK, RET_WIDTH, RET_WIDTH)
IN_WIDTH = sum(IN_SIZES)
IN_OFFSETS = tuple(sum(IN_SIZES[:i + 1]) for i in range(len(IN_SIZES) - 1))

kernel_name = 'hymba_hgrn2_retention_convffn_dit_block'


def _rmsnorm(t, g):
    tf = t.astype(jnp.float32)
    y = tf * lax.rsqrt(jnp.mean(tf * tf, axis=-1, keepdims=True) + EPS)
    return (y * g.astype(jnp.float32)).astype(t.dtype)


def _heads(t, n_heads):
    b, L, w = t.shape
    return t.reshape(b, L, n_heads, w // n_heads).transpose(0, 2, 1, 3)


def _merge(t):
    b, h, L, d = t.shape
    return t.transpose(0, 2, 1, 3).reshape(b, L, h * d)


def _flip(t):
    return jnp.flip(t, axis=2)


def _rope_2d(t, rows, cols):
    half = t.shape[-1] // 2
    quarter = half // 2
    freqs = ROPE_THETA ** (-jnp.arange(quarter, dtype=jnp.float32) / quarter)

    def rot(u, pos):
        ang = pos[:, None] * freqs[None, :]
        cos, sin = jnp.cos(ang), jnp.sin(ang)
        u1, u2 = u[..., :quarter], u[..., quarter:]
        return jnp.concatenate([u1 * cos - u2 * sin, u1 * sin + u2 * cos], axis=-1)

    return jnp.concatenate([rot(t[..., :half], rows), rot(t[..., half:], cols)], axis=-1)


def _to_chunks(t):
    b, h, L, d = t.shape
    return jnp.moveaxis(t.reshape(b, h, L // CHUNK, CHUNK, d), 2, 0)


def _from_chunks(t):
    n, b, h, c, d = t.shape
    return jnp.moveaxis(t, 0, 2).reshape(b, h, n * c, d)


def _hgrn2_chunk_scan(q, k, v, logf, s0):
    mask = jnp.tril(jnp.ones((CHUNK, CHUNK), dtype=bool))

    def step(s, inp):
        qc, kc, vc, lfc = inp
        b = jnp.cumsum(lfc, axis=2)
        diff = b[:, :, :, None, :] - b[:, :, None, :, :]
        decay = jnp.where(mask[:, :, None], jnp.exp(jnp.minimum(diff, 0.0)), 0.0)
        attn = jnp.einsum('bhtsk,bhsk->bhts', qc[:, :, :, None, :] * decay, kc)
        o = (jnp.einsum('bhtk,bhkv->bhtv', qc * jnp.exp(b), s)
             + jnp.einsum('bhts,bhsv->bhtv', attn, vc))
        b_last = b[:, :, -1:, :]
        s_new = (jnp.exp(b_last[:, :, 0, :])[..., None] * s
                 + jnp.einsum('bhsk,bhsv->bhkv', kc * jnp.exp(b_last - b), vc))
        return s_new, o

    s_fin, o = lax.scan(step, s0, (_to_chunks(q), _to_chunks(k), _to_chunks(v), _to_chunks(logf)))
    return _from_chunks(o), s_fin


def _hgrn2_final_state(k, v, logf):
    w = jnp.exp(lax.cumsum(logf, axis=2, reverse=True) - logf)
    return jnp.einsum('bhsk,bhsv->bhkv', k * w, v)


def _retention_chunk_scan(q, k, v, log_gamma, r0):
    pos = jnp.arange(CHUNK, dtype=jnp.float32)
    rel = pos[:, None] - pos[None, :]
    lg = log_gamma[:, None, None]
    intra = jnp.where(rel >= 0, jnp.exp(jnp.maximum(rel, 0.0) * lg), 0.0)
    q_dec = jnp.exp((pos + 1.0)[None, :] * log_gamma[:, None])
    k_dec = jnp.exp((CHUNK - 1.0 - pos)[None, :] * log_gamma[:, None])
    chunk_dec = jnp.exp(CHUNK * log_gamma)

    def step(r, inp):
        qc, kc, vc = inp
        scores = jnp.einsum('bhtk,bhsk->bhts', qc, kc) * intra
        o = (jnp.einsum('bhts,bhsv->bhtv', scores, vc)
             + jnp.einsum('bhtk,bhkv->bhtv', qc * q_dec[:, :, None], r))
        r_new = (chunk_dec[:, None, None] * r
                 + jnp.einsum('bhsk,bhsv->bhkv', kc * k_dec[:, :, None], vc))
        return r_new, o

    r_fin, o = lax.scan(step, r0, (_to_chunks(q), _to_chunks(k), _to_chunks(v)))
    return _from_chunks(o), r_fin


def _retention_final_state(k, v, log_gamma):
    L = k.shape[2]
    w = jnp.exp((L - 1.0 - jnp.arange(L, dtype=jnp.float32))[None, :] * log_gamma[:, None])
    return jnp.einsum('bhsk,bhsv->bhkv', k * w[:, :, None], v)


def _mixer_features(h, w_in_l, lb, pos):
    p = (h @ w_in_l).astype(jnp.float32)
    hq, hi, hff, hfb, hg, rq, rk, rv, rg = jnp.split(p, IN_OFFSETS, axis=-1)
    hq = _heads(hq, HG_HEADS)
    hv = _heads(hi, HG_HEADS)
    dirs = []
    for z, lbd in ((hff, lb[0]), (hfb, lb[1])):
        f = lbd + (1.0 - lbd) * jax.nn.sigmoid(_heads(z, HG_HEADS))
        dirs.append((1.0 - f, jnp.log(f)))
    rq = _heads(rq, RET_HEADS)
    rk = _heads(rk, RET_HEADS) * (RET_DK ** -0.5)
    rv = _heads(rv, RET_HEADS)
    if pos is not None:
        rq = _rope_2d(rq, pos[0], pos[1])
        rk = _rope_2d(rk, pos[0], pos[1])
    return (hq, hv, dirs[0], dirs[1], hg, rq, rk, rv, rg)


def _context_states(feats, log_gamma):
    hq, hv, (kf, lff), (kb, lfb), hg, rq, rk, rv, rg = feats
    s_f = _hgrn2_final_state(kf, hv, lff)
    s_b = _hgrn2_final_state(_flip(kb), _flip(hv), _flip(lfb))
    r_f = _retention_final_state(rk, rv, log_gamma[0])
    r_b = _retention_final_state(_flip(rk), _flip(rv), log_gamma[1])
    return (s_f, s_b, r_f, r_b)


def _zero_states(bsz):
    zh = jnp.zeros((bsz, HG_HEADS, HG_DK, HG_DV), jnp.float32)
    zr = jnp.zeros((bsz, RET_HEADS, RET_DK, RET_DV), jnp.float32)
    return (zh, zh, zr, zr)


def _bidir_mix(feats, states, log_gamma, hg_norm_g, ret_norm_g):
    hq, hv, (kf, lff), (kb, lfb), hg, rq, rk, rv, rg = feats
    s_f, s_b, r_f, r_b = states
    o_f, sf_out = _hgrn2_chunk_scan(hq, kf, hv, lff, s_f)
    o_b, sb_out = _hgrn2_chunk_scan(_flip(hq), _flip(kb), _flip(hv), _flip(lfb), s_b)
    hg_o = (o_f + _flip(o_b)) * jax.nn.sigmoid(_heads(hg, HG_HEADS))
    hg_o = _rmsnorm(hg_o, hg_norm_g.reshape(HG_HEADS, 1, HG_DV))
    y_f, rf_out = _retention_chunk_scan(rq, rk, rv, log_gamma[0], r_f)
    y_b, rb_out = _retention_chunk_scan(_flip(rq), _flip(rk), _flip(rv), log_gamma[1], r_b)
    ret_o = _rmsnorm(y_f + _flip(y_b), ret_norm_g.reshape(RET_HEADS, 1, RET_DV))
    ret_o = ret_o * jax.nn.silu(_heads(rg, RET_HEADS))
    mixed = jnp.concatenate([_merge(hg_o), _merge(ret_o)], axis=-1)
    return mixed, (sf_out, sb_out, rf_out, rb_out)


def _conv_ffn(h, w_up_l, conv_w_l, conv_b_l, w_down_l):
    gate, up = jnp.split(h @ w_up_l, 2, axis=-1)
    L = gate.shape[1]
    pad = CONV_W // 2
    gp = jnp.pad(gate, ((0, 0), (pad, pad), (0, 0)))
    gate = sum(gp[:, j:j + L] * conv_w_l[j] for j in range(CONV_W)) + conv_b_l
    return (jax.nn.silu(gate) * up) @ w_down_l


def setup_inputs(seed: int = 0) -> dict:
    key = jax.random.key(seed)
    ks = jax.random.split(key, 19)
    f32 = jnp.float32

    def nrm(k, shape, s):
        return jax.random.normal(k, shape, f32) * s

    x = nrm(ks[0], (BATCH, SEQ, D_MODEL), 1.0)
    c = nrm(ks[1], (BATCH, D_MODEL), 1.0)
    ctx = nrm(ks[2], (BATCH, CTX_LEN, D_MODEL), 1.0)
    c_ctx = nrm(ks[3], (D_MODEL,), 1.0)
    w_mod = nrm(ks[4], (DEPTH, D_MODEL, 6 * D_MODEL), 0.5 * D_MODEL ** -0.5)
    b_mod = nrm(ks[5], (DEPTH, 6 * D_MODEL), 0.02)
    norm1_g = 1.0 + nrm(ks[6], (DEPTH, D_MODEL), 0.02)
    w_in = nrm(ks[7], (DEPTH, D_MODEL, IN_WIDTH), D_MODEL ** -0.5)
    hgrn_lb = nrm(ks[8], (2, DEPTH + 1, HG_HEADS * HG_DK), 0.1)
    hgrn_norm_g = 1.0 + nrm(ks[9], (DEPTH, HG_WIDTH), 0.02)
    base = jnp.log(2.0 ** (5.0 + jnp.arange(RET_HEADS, dtype=f32)) - 1.0)
    ret_decay = base + nrm(ks[10], (DEPTH, 2, RET_HEADS), 0.01)
    ret_norm_g = 1.0 + nrm(ks[11], (DEPTH, RET_WIDTH), 0.02)
    w_out = nrm(ks[12], (DEPTH, MIX_WIDTH, D_MODEL), MIX_WIDTH ** -0.5)
    norm2_g = 1.0 + nrm(ks[13], (DEPTH, D_MODEL), 0.02)
    w_up = nrm(ks[14], (DEPTH, D_MODEL, 2 * D_FF), D_MODEL ** -0.5)
    conv_w = nrm(ks[15], (DEPTH, CONV_W, D_FF), CONV_W ** -0.5)
    conv_b = nrm(ks[16], (DEPTH, D_FF), 0.01)
    w_down = nrm(ks[17], (DEPTH, D_FF, D_MODEL), D_FF ** -0.5)
    final_g = 1.0 + nrm(ks[18], (D_MODEL,), 0.02)
    return {'x': x, 'c': c, 'ctx': ctx, 'c_ctx': c_ctx, 'w_mod': w_mod, 'b_mod': b_mod,
            'norm1_g': norm1_g, 'w_in': w_in, 'hgrn_lb': hgrn_lb, 'hgrn_norm_g': hgrn_norm_g,
            'ret_decay': ret_decay, 'ret_norm_g': ret_norm_g, 'w_out': w_out, 'norm2_g': norm2_g,
            'w_up': w_up, 'conv_w': conv_w, 'conv_b': conv_b, 'w_down': w_down, 'final_g': final_g}


def reference(x, c, ctx, c_ctx, w_mod, b_mod, norm1_g, w_in, hgrn_lb, hgrn_norm_g,
              ret_decay, ret_norm_g, w_out, norm2_g, w_up, conv_w, conv_b, w_down, final_g):
    f32 = jnp.float32
    seq_len = x.shape[1]
    ROWS = seq_len // GRID_W
    rows = jnp.repeat(jnp.arange(ROWS, dtype=f32), GRID_W)
    cols = jnp.tile(jnp.arange(GRID_W, dtype=f32), ROWS)
    lb_all = jnp.cumsum(jax.nn.softmax(hgrn_lb.astype(f32), axis=1), axis=1)
    for layer in range(DEPTH):
        last = layer == DEPTH - 1
        mod_x = (jax.nn.silu(c) @ w_mod[layer] + b_mod[layer])[:, None, :]
        mod_c = jax.nn.silu(c_ctx) @ w_mod[layer] + b_mod[layer]
        sh1, sc1, g1, sh2, sc2, g2 = jnp.split(mod_x, 6, axis=-1)
        csh1, csc1, cg1, csh2, csc2, cg2 = jnp.split(mod_c, 6, axis=-1)
        lb = lb_all[:, layer].reshape(2, HG_HEADS, 1, HG_DK)
        log_gamma = jax.nn.log_sigmoid(ret_decay[layer].astype(f32))

        hx = _rmsnorm(x, norm1_g[layer]) * (1.0 + sc1) + sh1
        hc = _rmsnorm(ctx, norm1_g[layer]) * (1.0 + csc1) + csh1
        feats_x = _mixer_features(hx, w_in[layer], lb, (rows, cols))
        feats_c = _mixer_features(hc, w_in[layer], lb, None)
        if last:
            ctx_states = _context_states(feats_c, log_gamma)
        else:
            mix_c, ctx_states = _bidir_mix(feats_c, _zero_states(ctx.shape[0]), log_gamma,
                                           hgrn_norm_g[layer], ret_norm_g[layer])
        mix_x, _ = _bidir_mix(feats_x, ctx_states, log_gamma, hgrn_norm_g[layer], ret_norm_g[layer])
        x = x + g1 * (mix_x.astype(x.dtype) @ w_out[layer])

        hx2 = _rmsnorm(x, norm2_g[layer]) * (1.0 + sc2) + sh2
        x = x + g2 * _conv_ffn(hx2, w_up[layer], conv_w[layer], conv_b[layer], w_down[layer])

        if not last:
            ctx = ctx + cg1 * (mix_c.astype(ctx.dtype) @ w_out[layer])
            hc2 = _rmsnorm(ctx, norm2_g[layer]) * (1.0 + csc2) + csh2
            ctx = ctx + cg2 * _conv_ffn(hc2, w_up[layer], conv_w[layer], conv_b[layer], w_down[layer])
    return _rmsnorm(x, final_g)
```

```python
import functools

import jax
import jax.numpy as jnp
from jax import lax
from jax.experimental import pallas as pl
from jax.experimental.pallas import tpu as pltpu

F32 = jnp.float32
BF16 = jnp.bfloat16

D_MODEL = 1024
HEADS = 4
DH = 128
GROUP_W = HEADS * DH
N_GROUPS = 9
IN_WIDTH = N_GROUPS * GROUP_W
D_FF = 2816
GRID_W = 64
ROPE_THETA = 10000.0
EPS = 1e-6

CHUNK = 128
FF_TILE = 256
ROW_TILE = 512
VMEM_LIMIT = 56 * 1024 * 1024


def _dot(a, b):
    return jnp.dot(a, b, preferred_element_type=F32)


def _dot_nt(a, b):
    return lax.dot_general(a, b, (((1,), (1,)), ((), ())), preferred_element_type=F32)


def _dot_tn(a, b):
    return lax.dot_general(a, b, (((0,), (0,)), ((), ())), preferred_element_type=F32)


def _rms(x, gain):
    return x * lax.rsqrt(jnp.mean(x * x, axis=-1, keepdims=True) + EPS) * gain


def _cumsum_rows(x):
    rows = x.shape[0]
    row = lax.broadcasted_iota(jnp.int32, x.shape, 0)
    shift = 1
    while shift < rows:
        x = x + jnp.where(row >= shift, pltpu.roll(x, shift, axis=0), 0.0)
        shift *= 2
    return x


def _mod_kernel(c_ref, w_ref, b_ref, o_ref):
    c = c_ref[...]
    a = (c * jax.nn.sigmoid(c)).astype(BF16)
    o_ref[...] = _dot(a, w_ref[...].astype(BF16)) + b_ref[...]


def _modulation(c_rows, w_mod, b_mod):
    n_rows = c_rows.shape[0]
    width = w_mod.shape[1]
    tile = D_MODEL
    return pl.pallas_call(
        _mod_kernel,
        grid=(width // tile,),
        in_specs=[
            pl.BlockSpec((n_rows, D_MODEL), lambda j: (0, 0)),
            pl.BlockSpec((D_MODEL, tile), lambda j: (0, j)),
            pl.BlockSpec((1, tile), lambda j: (0, j)),
        ],
        out_specs=pl.BlockSpec((n_rows, tile), lambda j: (0, j)),
        out_shape=jax.ShapeDtypeStruct((n_rows, width), F32),
        compiler_params=pltpu.CompilerParams(dimension_semantics=("parallel",)),
        name="modulation",
    )(c_rows, w_mod, b_mod)


def _inproj_kernel(x_ref, sh_ref, sc_ref, g_ref, w_ref, o_ref):
    x = x_ref[...]
    h = _rms(x, g_ref[...]) * (1.0 + sc_ref[0]) + sh_ref[0]
    hb = h.astype(BF16)
    for g in range(N_GROUPS):
        cols = slice(g * GROUP_W, (g + 1) * GROUP_W)
        o_ref[:, cols] = _dot(hb, w_ref[:, cols]).astype(BF16)


def _in_projection(x2d, mod3, tiles_per_mod_row, norm_g, w_in_bf16, row_tile):
    n_rows = x2d.shape[0]

    def mod_spec(chunk):
        return pl.BlockSpec((1, 1, D_MODEL), lambda i: (i // tiles_per_mod_row, 0, chunk))

    return pl.pallas_call(
        _inproj_kernel,
        grid=(n_rows // row_tile,),
        in_specs=[
            pl.BlockSpec((row_tile, D_MODEL), lambda i: (i, 0)),
            mod_spec(0),
            mod_spec(1),
            pl.BlockSpec((1, D_MODEL), lambda i: (0, 0)),
            pl.BlockSpec((D_MODEL, IN_WIDTH), lambda i: (0, 0)),
        ],
        out_specs=pl.BlockSpec((row_tile, IN_WIDTH), lambda i: (i, 0)),
        out_shape=jax.ShapeDtypeStruct((n_rows, IN_WIDTH), BF16),
        compiler_params=pltpu.CompilerParams(
            dimension_semantics=("parallel",), vmem_limit_bytes=VMEM_LIMIT),
        name="in_projection",
    )(x2d, mod3, mod3, norm_g, w_in_bf16)


def _forget(z, lb):
    f = lb + (1.0 - lb) * jax.nn.sigmoid(z)
    return f, jnp.log(f)


def _ctx_kernel(lg_ref, v_ref, zf_ref, zb_ref, rk_ref, rv_ref, lb_ref,
                sf_ref, sb_ref, rf_ref, rb_ref):
    h = pl.program_id(1)
    n = v_ref.shape[1]
    v = v_ref[0]
    ff, lff = _forget(zf_ref[0].astype(F32), lb_ref[0:1, :])
    fb, lfb = _forget(zb_ref[0].astype(F32), lb_ref[1:2, :])
    bf = _cumsum_rows(lff)
    bb = _cumsum_rows(lfb)
    kf = (1.0 - ff) * jnp.exp(bf[n - 1:n, :] - bf)
    kb = (1.0 - fb) * jnp.exp(bb - lfb)
    sf_ref[0, 0] = _dot_tn(v, kf.astype(BF16))
    sb_ref[0, 0] = _dot_tn(v, kb.astype(BF16))

    pos = lax.broadcasted_iota(jnp.int32, (n, DH), 0).astype(F32)
    rk = rk_ref[0].astype(F32) * (DH ** -0.5)
    rv = rv_ref[0]
    wf = jnp.exp((n - 1.0 - pos) * lg_ref[0, h])
    wb = jnp.exp(pos * lg_ref[1, h])
    rf_ref[0, 0] = _dot_tn(rv, (rk * wf).astype(BF16))
    rb_ref[0, 0] = _dot_tn(rv, (rk * wb).astype(BF16))


def _context_states(p_ctx, lb, log_gamma):
    batch, n_ctx, _ = p_ctx.shape

    def group(g):
        return pl.BlockSpec((1, n_ctx, DH), lambda b, h: (b, 0, g * HEADS + h))

    state = pl.BlockSpec((1, 1, DH, DH), lambda b, h: (b, h, 0, 0))
    state_shape = jax.ShapeDtypeStruct((batch, HEADS, DH, DH), F32)
    return pl.pallas_call(
        _ctx_kernel,
        grid=(batch, HEADS),
        in_specs=[
            pl.BlockSpec(memory_space=pltpu.SMEM),
            group(1), group(2), group(3), group(6), group(7),
            pl.BlockSpec((2, DH), lambda b, h: (0, h)),
        ],
        out_specs=[state, state, state, state],
        out_shape=[state_shape] * 4,
        compiler_params=pltpu.CompilerParams(dimension_semantics=("parallel", "parallel")),
        name="context_states",
    )(log_gamma, p_ctx, p_ctx, p_ctx, p_ctx, p_ctx, lb)


def _anchor(phi, half, forward):
    rows = phi.shape[0]
    pick = half - 1 if forward else half
    block = 2 * half
    if block >= 8:
        p3 = phi.reshape(rows // block, block, DH)
        return jnp.broadcast_to(p3[:, pick:pick + 1, :], p3.shape).reshape(rows, DH)
    p3 = phi.reshape(rows // 8, 8, DH)
    sub = lax.broadcasted_iota(jnp.int32, p3.shape, 1)
    out = None
    for start in range(0, 8, block):
        a = jnp.broadcast_to(p3[:, start + pick:start + pick + 1, :], p3.shape)
        out = a if out is None else jnp.where(sub >= start, a, out)
    return out.reshape(rows, DH)


def _rope(t, cos, sin_signed, first_quarter):
    swapped = jnp.where(first_quarter, pltpu.roll(t, DH - DH // 4, axis=1),
                        pltpu.roll(t, DH // 4, axis=1))
    return t * cos + swapped * sin_signed


def _mixer_kernel(lg_ref, q_ref, v_ref, zf_ref, zb_ref, og_ref, rq_ref, rk_ref, rv_ref, rg_ref,
                  lb_ref, hgn_ref, rn_ref, cos_ref, sin_ref,
                  sf0_ref, sb0_ref, rf0_ref, rb0_ref,
                  hgo_ref, reto_ref,
                  oh_s, or_s, qf_s, kf_s, qb_s, kb_s, rq_s, rk_s, decf_s, decb_s,
                  sf_s, sb_s, rf_s, rb_s):
    head = pl.program_id(1)
    seq = q_ref.shape[1]
    n_chunks = seq // CHUNK
    lgf = lg_ref[0, head]
    lgb = lg_ref[1, head]
    lbf = lb_ref[0:1, :]
    lbb = lb_ref[1:2, :]

    row = lax.broadcasted_iota(jnp.int32, (CHUNK, DH), 0)
    lane = lax.broadcasted_iota(jnp.int32, (CHUNK, DH), 1)
    first_quarter = (lane % (DH // 2)) < (DH // 4)
    t_idx = lax.broadcasted_iota(jnp.int32, (CHUNK, CHUNK), 0)
    s_idx = lax.broadcasted_iota(jnp.int32, (CHUNK, CHUNK), 1)
    split = t_idx ^ s_idx
    rel = (t_idx - s_idx).astype(F32)
    ret_decay = jnp.where(rel >= 0, jnp.exp(rel * lgf), 0.0) + jnp.where(rel <= 0, jnp.exp(-rel * lgb), 0.0)

    def pass_a(i, carry):
        rows = pl.ds(pl.multiple_of(i * CHUNK, CHUNK), CHUNK)
        q = q_ref[0, rows, :].astype(F32)
        v = v_ref[0, rows, :]
        ff, lff = _forget(zf_ref[0, rows, :].astype(F32), lbf)
        fb, lfb = _forget(zb_ref[0, rows, :].astype(F32), lbb)
        kf = 1.0 - ff
        kb = 1.0 - fb
        cf = _cumsum_rows(lff)
        cb = _cumsum_rows(lfb)
        phif = cf
        phib = lfb - cb

        attn = None
        half = CHUNK // 2
        while half >= 1:
            af = _anchor(phif, half, True)
            ab = _anchor(phib, half, False)
            second = (row & half) != 0
            eq = jnp.where(second, phif - af, phib - ab)
            ek = jnp.where(second, ab - phib, af - phif)
            zz = (q * jnp.exp(eq)).astype(BF16)
            ww = (jnp.where(second, kb, kf) * jnp.exp(ek)).astype(BF16)
            a = _dot_nt(zz, ww)
            attn = a if attn is None else jnp.where(split < 2 * half, a, attn)
            half //= 2
        diag = jnp.sum(q * (kf + kb), axis=-1, keepdims=True)
        attn = jnp.where(split == 0, 0.0, attn)
        vf = v.astype(F32)
        oh_s[rows, :] = _dot(attn.astype(BF16), v) + diag * vf

        last_f = cf[CHUNK - 1:CHUNK, :]
        last_b = cb[CHUNK - 1:CHUNK, :]
        qf_s[rows, :] = (q * jnp.exp(cf)).astype(BF16)
        kf_s[rows, :] = (kf * jnp.exp(last_f - cf)).astype(BF16)
        qb_s[rows, :] = (q * jnp.exp(last_b + phib)).astype(BF16)
        kb_s[rows, :] = (kb * jnp.exp(cb - lfb)).astype(BF16)
        decf_s[pl.ds(i, 1), :] = jnp.exp(last_f)
        decb_s[pl.ds(i, 1), :] = jnp.exp(last_b)

        cos = cos_ref[rows, :]
        sin = sin_ref[rows, :]
        rq = _rope(rq_ref[0, rows, :].astype(F32), cos, sin, first_quarter)
        rk = _rope(rk_ref[0, rows, :].astype(F32) * (DH ** -0.5), cos, sin, first_quarter)
        rq_s[rows, :] = rq
        rk_s[rows, :] = rk
        scores = _dot_nt(rq.astype(BF16), rk.astype(BF16)) * ret_decay
        or_s[rows, :] = _dot(scores.astype(BF16), rv_ref[0, rows, :])
        return carry

    lax.fori_loop(0, n_chunks, pass_a, 0)

    sf_s[...] = sf0_ref[0, 0]
    sb_s[...] = sb0_ref[0, 0]
    rf_s[...] = rf0_ref[0, 0]
    rb_s[...] = rb0_ref[0, 0]
    pos = row.astype(F32)
    qdec_f = jnp.exp((pos + 1.0) * lgf)
    kdec_f = jnp.exp((CHUNK - 1.0 - pos) * lgf)
    qdec_b = jnp.exp((CHUNK - pos) * lgb)
    kdec_b = jnp.exp(pos * lgb)
    cdec_f = jnp.exp(CHUNK * lgf)
    cdec_b = jnp.exp(CHUNK * lgb)

    def pass_b(i, carry):
        j = n_chunks - 1 - i
        rf = pl.ds(pl.multiple_of(i * CHUNK, CHUNK), CHUNK)
        rb = pl.ds(pl.multiple_of(j * CHUNK, CHUNK), CHUNK)
        s = sf_s[...]
        oh_s[rf, :] += _dot_nt(qf_s[rf, :], s.astype(BF16))
        sf_s[...] = s * decf_s[pl.ds(i, 1), :] + _dot_tn(v_ref[0, rf, :], kf_s[rf, :])
        s = sb_s[...]
        oh_s[rb, :] += _dot_nt(qb_s[rb, :], s.astype(BF16))
        sb_s[...] = s * decb_s[pl.ds(j, 1), :] + _dot_tn(v_ref[0, rb, :], kb_s[rb, :])
        r = rf_s[...]
        or_s[rf, :] += _dot_nt((rq_s[rf, :] * qdec_f).astype(BF16), r.astype(BF16))
        rf_s[...] = r * cdec_f + _dot_tn(rv_ref[0, rf, :], (rk_s[rf, :] * kdec_f).astype(BF16))
        r = rb_s[...]
        or_s[rb, :] += _dot_nt((rq_s[rb, :] * qdec_b).astype(BF16), r.astype(BF16))
        rb_s[...] = r * cdec_b + _dot_tn(rv_ref[0, rb, :], (rk_s[rb, :] * kdec_b).astype(BF16))
        return carry

    lax.fori_loop(0, n_chunks, pass_b, 0)

    def pass_c(i, carry):
        rows = pl.ds(pl.multiple_of(i * CHUNK, CHUNK), CHUNK)
        hg = oh_s[rows, :] * jax.nn.sigmoid(og_ref[0, rows, :].astype(F32))
        hgo_ref[0, rows, :] = _rms(hg, hgn_ref[...]).astype(BF16)
        rg = rg_ref[0, rows, :].astype(F32)
        ret = _rms(or_s[rows, :], rn_ref[...]) * (rg * jax.nn.sigmoid(rg))
        reto_ref[0, rows, :] = ret.astype(BF16)
        return carry

    lax.fori_loop(0, n_chunks, pass_c, 0)


def _mixer(p, lb, log_gamma, hg_norm, ret_norm, cos_t, sin_t, states):
    batch, seq, _ = p.shape
    n_chunks = seq // CHUNK

    def group(g):
        return pl.BlockSpec((1, seq, DH), lambda b, h: (b, 0, g * HEADS + h))

    per_head_row = pl.BlockSpec((1, DH), lambda b, h: (0, h))
    table = pl.BlockSpec((seq, DH), lambda b, h: (0, 0))
    state = pl.BlockSpec((1, 1, DH, DH), lambda b, h: (b, h, 0, 0))
    out_spec = pl.BlockSpec((1, seq, DH), lambda b, h: (b, 0, h))
    out_shape = jax.ShapeDtypeStruct((batch, seq, GROUP_W), BF16)
    return pl.pallas_call(
        _mixer_kernel,
        grid=(batch, HEADS),
        in_specs=[pl.BlockSpec(memory_space=pltpu.SMEM)]
        + [group(g) for g in (0, 1, 2, 3, 4, 5, 6, 7, 8)]
        + [pl.BlockSpec((2, DH), lambda b, h: (0, h)), per_head_row, per_head_row, table, table,
           state, state, state, state],
        out_specs=[out_spec, out_spec],
        out_shape=[out_shape, out_shape],
        scratch_shapes=[
            pltpu.VMEM((seq, DH), F32), pltpu.VMEM((seq, DH), F32),
            pltpu.VMEM((seq, DH), BF16), pltpu.VMEM((seq, DH), BF16),
            pltpu.VMEM((seq, DH), BF16), pltpu.VMEM((seq, DH), BF16),
            pltpu.VMEM((seq, DH), F32), pltpu.VMEM((seq, DH), F32),
            pltpu.VMEM((n_chunks, DH), F32), pltpu.VMEM((n_chunks, DH), F32),
            pltpu.VMEM((DH, DH), F32), pltpu.VMEM((DH, DH), F32),
            pltpu.VMEM((DH, DH), F32), pltpu.VMEM((DH, DH), F32),
        ],
        compiler_params=pltpu.CompilerParams(
            dimension_semantics=("parallel", "parallel"), vmem_limit_bytes=VMEM_LIMIT),
        name="token_mixer",
    )(log_gamma, p, p, p, p, p, p, p, p, p, lb, hg_norm, ret_norm, cos_t, sin_t, *states)


HALO = 8


def _ffn_kernel(x_ref, hgo_ref, reto_ref, g1_ref, sh2_ref, sc2_ref, g2_ref,
                woa_ref, wob_ref, n2_ref, fin_ref, wg_ref, wu_ref, cw_ref, cb_ref, wd_ref,
                o_ref, h2_s, acc_s, gate_s):
    j = pl.program_id(1)
    seq = x_ref.shape[1]
    tiles = [pl.ds(r, ROW_TILE) for r in range(0, seq, ROW_TILE)]

    @pl.when(j == 0)
    def _():
        for rows in tiles:
            proj = _dot(hgo_ref[0, rows, :], woa_ref[...]) + _dot(reto_ref[0, rows, :], wob_ref[...])
            x1 = x_ref[0, rows, :] + g1_ref[0] * proj
            o_ref[0, rows, :] = x1
            h2 = _rms(x1, n2_ref[...]) * (1.0 + sc2_ref[0]) + sh2_ref[0]
            h2_s[rows, :] = h2.astype(BF16)
            acc_s[rows, :] = jnp.zeros((ROW_TILE, D_MODEL), F32)
        gate_s[0:HALO, :] = jnp.zeros((HALO, FF_TILE), F32)
        gate_s[HALO + seq:HALO + seq + HALO, :] = jnp.zeros((HALO, FF_TILE), F32)

    for r in range(0, seq, ROW_TILE):
        gate_s[HALO + r:HALO + r + ROW_TILE, :] = _dot(h2_s[r:r + ROW_TILE, :], wg_ref[...])
    for r in range(0, seq, ROW_TILE):
        up = _dot(h2_s[r:r + ROW_TILE, :], wu_ref[...])
        conv = (gate_s[HALO + r - 1:HALO + r - 1 + ROW_TILE, :] * cw_ref[0:1, :]
                + gate_s[HALO + r:HALO + r + ROW_TILE, :] * cw_ref[1:2, :]
                + gate_s[HALO + r + 1:HALO + r + 1 + ROW_TILE, :] * cw_ref[2:3, :]
                + cb_ref[...])
        act = (conv * jax.nn.sigmoid(conv) * up).astype(BF16)
        acc_s[r:r + ROW_TILE, :] += _dot(act, wd_ref[...])

    @pl.when(j == pl.num_programs(1) - 1)
    def _():
        for rows in tiles:
            x2 = o_ref[0, rows, :] + g2_ref[0] * acc_s[rows, :]
            o_ref[0, rows, :] = _rms(x2, fin_ref[...])


def _out_ffn(x, hgo, reto, mod3, w_out_bf16, norm2_g, final_g, w_up_bf16, conv_w, conv_b, w_down_bf16):
    batch, seq, _ = x.shape
    n_ff = D_FF // FF_TILE

    def mod_spec(chunk):
        return pl.BlockSpec((1, 1, D_MODEL), lambda b, j: (b, 0, chunk))

    once = pl.Buffered(1)
    row_vec = pl.BlockSpec((1, D_MODEL), lambda b, j: (0, 0))
    return pl.pallas_call(
        _ffn_kernel,
        grid=(batch, n_ff),
        in_specs=[
            pl.BlockSpec((1, seq, D_MODEL), lambda b, j: (b, 0, 0), pipeline_mode=once),
            pl.BlockSpec((1, seq, GROUP_W), lambda b, j: (b, 0, 0), pipeline_mode=once),
            pl.BlockSpec((1, seq, GROUP_W), lambda b, j: (b, 0, 0), pipeline_mode=once),
            mod_spec(2), mod_spec(3), mod_spec(4), mod_spec(5),
            pl.BlockSpec((GROUP_W, D_MODEL), lambda b, j: (0, 0)),
            pl.BlockSpec((GROUP_W, D_MODEL), lambda b, j: (1, 0)),
            row_vec, row_vec,
            pl.BlockSpec((D_MODEL, FF_TILE), lambda b, j: (0, j)),
            pl.BlockSpec((D_MODEL, FF_TILE), lambda b, j: (0, n_ff + j)),
            pl.BlockSpec((3, FF_TILE), lambda b, j: (0, j)),
            pl.BlockSpec((1, FF_TILE), lambda b, j: (0, j)),
            pl.BlockSpec((FF_TILE, D_MODEL), lambda b, j: (j, 0)),
        ],
        out_specs=pl.BlockSpec((1, seq, D_MODEL), lambda b, j: (b, 0, 0)),
        out_shape=jax.ShapeDtypeStruct((batch, seq, D_MODEL), F32),
        scratch_shapes=[
            pltpu.VMEM((seq, D_MODEL), BF16),
            pltpu.VMEM((seq, D_MODEL), F32),
            pltpu.VMEM((seq + 2 * HALO, FF_TILE), F32),
        ],
        compiler_params=pltpu.CompilerParams(
            dimension_semantics=("parallel", "arbitrary"), vmem_limit_bytes=VMEM_LIMIT),
        name="out_proj_ffn",
    )(x, hgo, reto, mod3, mod3, mod3, mod3, w_out_bf16, w_out_bf16, norm2_g, final_g,
      w_up_bf16, w_up_bf16, conv_w, conv_b, w_down_bf16)


def _rope_tables(seq):
    quarter = DH // 4
    freqs = ROPE_THETA ** (-jnp.arange(quarter, dtype=F32) / quarter)
    t = jnp.arange(seq)
    rows = (t // GRID_W).astype(F32)
    cols = (t % GRID_W).astype(F32)
    ang_r = rows[:, None] * freqs[None, :]
    ang_c = cols[:, None] * freqs[None, :]
    cos = jnp.concatenate([jnp.cos(ang_r)] * 2 + [jnp.cos(ang_c)] * 2, axis=-1)
    sin = jnp.concatenate([-jnp.sin(ang_r), jnp.sin(ang_r), -jnp.sin(ang_c), jnp.sin(ang_c)], axis=-1)
    return cos, sin


def kernel(x, c, ctx, c_ctx, w_mod, b_mod, norm1_g, w_in, hgrn_lb, hgrn_norm_g, ret_decay,
           ret_norm_g, w_out, norm2_g, w_up, conv_w, conv_b, w_down, final_g):
    batch, seq, _ = x.shape
    n_ctx = ctx.shape[1]
    assert w_mod.shape[0] == 1, "single-layer block"
    assert seq % CHUNK == 0 and seq % ROW_TILE == 0 and (n_ctx & (n_ctx - 1)) == 0

    lb = jnp.cumsum(jax.nn.softmax(hgrn_lb.astype(F32), axis=1), axis=1)[:, 0]
    log_gamma = jax.nn.log_sigmoid(ret_decay[0].astype(F32))
    cos_t, sin_t = _rope_tables(seq)

    n_mod_rows = -(-(batch + 1) // 8) * 8
    c_rows = jnp.zeros((n_mod_rows, D_MODEL), F32).at[:batch].set(c).at[batch].set(c_ctx)
    mod = _modulation(c_rows, w_mod[0], b_mod[0][None, :])
    mod3 = mod.reshape(n_mod_rows, 1, 6 * D_MODEL)

    w_in_b = w_in[0].astype(BF16)
    norm1 = norm1_g[0][None, :]
    p = _in_projection(x.reshape(batch * seq, D_MODEL), mod3[:batch], seq // ROW_TILE,
                       norm1, w_in_b, ROW_TILE)
    ctx_tile = min(ROW_TILE, batch * n_ctx)
    p_ctx = _in_projection(ctx.reshape(batch * n_ctx, D_MODEL), mod3[batch:batch + 1],
                           batch * n_ctx // ctx_tile + 1, norm1, w_in_b, ctx_tile)

    states = _context_states(p_ctx.reshape(batch, n_ctx, IN_WIDTH), lb, log_gamma)
    hgo, reto = _mixer(p.reshape(batch, seq, IN_WIDTH), lb, log_gamma,
                       hgrn_norm_g[0][None, :], ret_norm_g[0][None, :], cos_t, sin_t, states)

    return _out_ffn(x, hgo, reto, mod3[:batch], w_out[0].astype(BF16), norm2_g[0][None, :],
                    final_g[None, :], w_up[0].astype(BF16), conv_w[0], conv_b[0][None, :],
                    w_down[0].astype(BF16))
```

```python
import functools

import jax
import jax.numpy as jnp
from jax import lax
from jax.experimental import pallas as pl
from jax.experimental.pallas import tpu as pltpu

F32 = jnp.float32
BF16 = jnp.bfloat16

D_MODEL = 1024
HEADS = 4
DH = 128
GROUP_W = HEADS * DH
N_GROUPS = 9
IN_WIDTH = N_GROUPS * GROUP_W
D_FF = 2816
GRID_W = 64
ROPE_THETA = 10000.0
EPS = 1e-6
LOG2E = 1.4426950408889634

CHUNK = 128
FF_TILE = 256
ROW_TILE = 512
VMEM_LIMIT = 56 * 1024 * 1024


def _dot(a, b):
    return jnp.dot(a, b, preferred_element_type=F32)


def _dot_nt(a, b):
    return lax.dot_general(a, b, (((1,), (1,)), ((), ())), preferred_element_type=F32)


def _dot_tn(a, b):
    return lax.dot_general(a, b, (((0,), (0,)), ((), ())), preferred_element_type=F32)


def _rms(x, gain):
    return x * lax.rsqrt(jnp.mean(x * x, axis=-1, keepdims=True) + EPS) * gain


def _cumsum_rows(x):
    rows = x.shape[0]
    row = lax.broadcasted_iota(jnp.int32, x.shape, 0)
    shift = 1
    while shift < rows:
        x = x + jnp.where(row >= shift, pltpu.roll(x, shift, axis=0), 0.0)
        shift *= 2
    return x


def _mod_kernel(c_ref, w_ref, b_ref, o_ref):
    c = c_ref[...]
    a = (c * jax.nn.sigmoid(c)).astype(BF16)
    o_ref[...] = _dot(a, w_ref[...].astype(BF16)) + b_ref[...]


def _modulation(c_rows, w_mod, b_mod):
    n_rows = c_rows.shape[0]
    width = w_mod.shape[1]
    tile = D_MODEL
    return pl.pallas_call(
        _mod_kernel,
        grid=(width // tile,),
        in_specs=[
            pl.BlockSpec((n_rows, D_MODEL), lambda j: (0, 0)),
            pl.BlockSpec((D_MODEL, tile), lambda j: (0, j)),
            pl.BlockSpec((1, tile), lambda j: (0, j)),
        ],
        out_specs=pl.BlockSpec((n_rows, tile), lambda j: (0, j)),
        out_shape=jax.ShapeDtypeStruct((n_rows, width), F32),
        compiler_params=pltpu.CompilerParams(dimension_semantics=("parallel",)),
        name="modulation",
    )(c_rows, w_mod, b_mod)


def _inproj_kernel(x_ref, sh_ref, sc_ref, g_ref, w_ref, o_ref):
    x = x_ref[...]
    h = _rms(x, g_ref[...]) * (1.0 + sc_ref[0]) + sh_ref[0]
    hb = h.astype(BF16)
    for g in range(N_GROUPS):
        cols = slice(g * GROUP_W, (g + 1) * GROUP_W)
        o_ref[:, cols] = _dot(hb, w_ref[:, cols]).astype(BF16)


def _in_projection(x2d, mod3, tiles_per_mod_row, norm_g, w_in_bf16, row_tile):
    n_rows = x2d.shape[0]

    def mod_spec(chunk):
        return pl.BlockSpec((1, 1, D_MODEL), lambda i: (i // tiles_per_mod_row, 0, chunk))

    return pl.pallas_call(
        _inproj_kernel,
        grid=(n_rows // row_tile,),
        in_specs=[
            pl.BlockSpec((row_tile, D_MODEL), lambda i: (i, 0)),
            mod_spec(0),
            mod_spec(1),
            pl.BlockSpec((1, D_MODEL), lambda i: (0, 0)),
            pl.BlockSpec((D_MODEL, IN_WIDTH), lambda i: (0, 0)),
        ],
        out_specs=pl.BlockSpec((row_tile, IN_WIDTH), lambda i: (i, 0)),
        out_shape=jax.ShapeDtypeStruct((n_rows, IN_WIDTH), BF16),
        compiler_params=pltpu.CompilerParams(
            dimension_semantics=("parallel",), vmem_limit_bytes=VMEM_LIMIT),
        name="in_projection",
    )(x2d, mod3, mod3, norm_g, w_in_bf16)


def _forget(z, lb):
    f = lb + (1.0 - lb) * jax.nn.sigmoid(z)
    return f, jnp.log(f)


def _ctx_kernel(lg_ref, v_ref, zf_ref, zb_ref, rk_ref, rv_ref, lb_ref,
                sf_ref, sb_ref, rf_ref, rb_ref):
    h = pl.program_id(1)
    n = v_ref.shape[1]
    v = v_ref[0]
    ff, lff = _forget(zf_ref[0].astype(F32), lb_ref[0:1, :])
    fb, lfb = _forget(zb_ref[0].astype(F32), lb_ref[1:2, :])
    bf = _cumsum_rows(lff)
    bb = _cumsum_rows(lfb)
    kf = (1.0 - ff) * jnp.exp(bf[n - 1:n, :] - bf)
    kb = (1.0 - fb) * jnp.exp(bb - lfb)
    sf_ref[0, 0] = _dot_tn(v, kf.astype(BF16))
    sb_ref[0, 0] = _dot_tn(v, kb.astype(BF16))

    pos = lax.broadcasted_iota(jnp.int32, (n, DH), 0).astype(F32)
    rk = rk_ref[0].astype(F32) * (DH ** -0.5)
    rv = rv_ref[0]
    wf = jnp.exp((n - 1.0 - pos) * lg_ref[0, h])
    wb = jnp.exp(pos * lg_ref[1, h])
    rf_ref[0, 0] = _dot_tn(rv, (rk * wf).astype(BF16))
    rb_ref[0, 0] = _dot_tn(rv, (rk * wb).astype(BF16))


def _context_states(p_ctx, lb, log_gamma):
    batch, n_ctx, _ = p_ctx.shape

    def group(g):
        return pl.BlockSpec((1, n_ctx, DH), lambda b, h: (b, 0, g * HEADS + h))

    state = pl.BlockSpec((1, 1, DH, DH), lambda b, h: (b, h, 0, 0))
    state_shape = jax.ShapeDtypeStruct((batch, HEADS, DH, DH), F32)
    return pl.pallas_call(
        _ctx_kernel,
        grid=(batch, HEADS),
        in_specs=[
            pl.BlockSpec(memory_space=pltpu.SMEM),
            group(1), group(2), group(3), group(6), group(7),
            pl.BlockSpec((2, DH), lambda b, h: (0, h)),
        ],
        out_specs=[state, state, state, state],
        out_shape=[state_shape] * 4,
        compiler_params=pltpu.CompilerParams(dimension_semantics=("parallel", "parallel")),
        name="context_states",
    )(log_gamma, p_ctx, p_ctx, p_ctx, p_ctx, p_ctx, lb)


def _anchor(phi, half, forward):
    rows = phi.shape[0]
    pick = half - 1 if forward else half
    block = 2 * half
    p3 = phi.reshape(rows // 8, 8, DH)
    sub = lax.broadcasted_iota(jnp.int32, p3.shape, 1)
    out = None
    for start in range(0, 8, block):
        a = jnp.broadcast_to(p3[:, start + pick:start + pick + 1, :], p3.shape)
        out = a if out is None else jnp.where(sub >= start, a, out)
    return out.reshape(rows, DH)


def _level_operands(q, kf, kb, ff, fb, phif, phib, half, row):
    n = q.shape[0]
    if half == 1:
        odd = (row & 1) != 0
        zz = q * jnp.where(odd, ff, fb)
        ww = jnp.where(odd, kb, kf)
    elif half >= 8:
        zs, ws = [], []
        for b0 in range(0, n, 2 * half):
            first = slice(b0, b0 + half)
            second = slice(b0 + half, b0 + 2 * half)
            af = phif[b0 + half - 1:b0 + half, :]
            ab = phib[b0 + half:b0 + half + 1, :]
            zs += [q[first] * jnp.exp2(phib[first] - ab), q[second] * jnp.exp2(phif[second] - af)]
            ws += [kf[first] * jnp.exp2(af - phif[first]), kb[second] * jnp.exp2(ab - phib[second])]
        zz = jnp.concatenate(zs, axis=0)
        ww = jnp.concatenate(ws, axis=0)
    else:
        af = _anchor(phif, half, True)
        ab = _anchor(phib, half, False)
        second = (row & half) != 0
        zz = q * jnp.exp2(jnp.where(second, phif - af, phib - ab))
        ww = jnp.where(second, kb, kf) * jnp.exp2(jnp.where(second, ab - phib, af - phif))
    return zz.astype(BF16), ww.astype(BF16)


def _rope(t, cos, sin_signed, first_quarter):
    swapped = jnp.where(first_quarter, pltpu.roll(t, DH - DH // 4, axis=1),
                        pltpu.roll(t, DH // 4, axis=1))
    return t * cos + swapped * sin_signed


def _mixer_kernel(lg_ref, q_ref, v_ref, zf_ref, zb_ref, og_ref, rq_ref, rk_ref, rv_ref, rg_ref,
                  lb_ref, hgn_ref, rn_ref, cos_ref, sin_ref,
                  sf0_ref, sb0_ref, rf0_ref, rb0_ref,
                  hgo_ref, reto_ref,
                  oh_s, or_s, qcat_s, rqcat_s, sh_s, sr_s, dec_s):
    head = pl.program_id(1)
    seq = q_ref.shape[1]
    n_chunks = seq // CHUNK
    lgf = lg_ref[0, head]
    lgb = lg_ref[1, head]
    lbf = lb_ref[0:1, :]
    lbb = lb_ref[1:2, :]
    fwd = slice(0, DH)
    bwd = slice(DH, 2 * DH)

    row = lax.broadcasted_iota(jnp.int32, (CHUNK, DH), 0)
    lane = lax.broadcasted_iota(jnp.int32, (CHUNK, DH), 1)
    first_quarter = (lane % (DH // 2)) < (DH // 4)
    t_idx = lax.broadcasted_iota(jnp.int32, (CHUNK, CHUNK), 0)
    s_idx = lax.broadcasted_iota(jnp.int32, (CHUNK, CHUNK), 1)
    split = t_idx ^ s_idx
    rel = (t_idx - s_idx).astype(F32)
    ret_decay = jnp.where(rel >= 0, jnp.exp(rel * lgf), 0.0) + jnp.where(rel <= 0, jnp.exp(-rel * lgb), 0.0)
    pos = row.astype(F32)
    qdec_f = jnp.exp((pos + 1.0) * lgf)
    kdec_f = jnp.exp((CHUNK - 1.0 - pos) * lgf)
    qdec_b = jnp.exp((CHUNK - pos) * lgb)
    kdec_b = jnp.exp(pos * lgb)

    def pass_a(i, carry):
        rows = pl.ds(pl.multiple_of(i * CHUNK, CHUNK), CHUNK)
        q = q_ref[0, rows, :].astype(F32)
        v = v_ref[0, rows, :]
        ff, lff = _forget(zf_ref[0, rows, :].astype(F32), lbf)
        fb, lfb = _forget(zb_ref[0, rows, :].astype(F32), lbb)
        lff = lff * LOG2E
        lfb = lfb * LOG2E
        kf = 1.0 - ff
        kb = 1.0 - fb
        cf = _cumsum_rows(lff)
        cb = _cumsum_rows(lfb)
        phif = cf
        phib = lfb - cb

        attn = None
        half = CHUNK // 2
        while half >= 1:
            zz, ww = _level_operands(q, kf, kb, ff, fb, phif, phib, half, row)
            a = _dot_nt(zz, ww)
            attn = a if attn is None else jnp.where(split < 2 * half, a, attn)
            half //= 2
        attn = jnp.where(split == 0, 0.0, attn)
        diag = jnp.sum(q * (kf + kb), axis=-1, keepdims=True)
        oh_s[rows, :] = _dot(attn.astype(BF16), v) + diag * v.astype(F32)

        last_f = cf[CHUNK - 1:CHUNK, :]
        last_b = cb[CHUNK - 1:CHUNK, :]
        qcat_s[rows, fwd] = (q * jnp.exp2(cf)).astype(BF16)
        qcat_s[rows, bwd] = (q * jnp.exp2(last_b + phib)).astype(BF16)
        kcat = jnp.concatenate([(kf * jnp.exp2(last_f - cf)).astype(BF16),
                                (kb * jnp.exp2(cb - lfb)).astype(BF16)], axis=1)
        sh_s[i] = _dot_tn(v, kcat)
        dec_s[i, :, fwd] = jnp.exp2(last_f)
        dec_s[i, :, bwd] = jnp.exp2(last_b)

        cos = cos_ref[rows, :]
        sin = sin_ref[rows, :]
        rq = _rope(rq_ref[0, rows, :].astype(F32), cos, sin, first_quarter)
        rk = _rope(rk_ref[0, rows, :].astype(F32) * (DH ** -0.5), cos, sin, first_quarter)
        rv = rv_ref[0, rows, :]
        scores = _dot_nt(rq.astype(BF16), rk.astype(BF16)) * ret_decay
        or_s[rows, :] = _dot(scores.astype(BF16), rv)
        rqcat_s[rows, fwd] = (rq * qdec_f).astype(BF16)
        rqcat_s[rows, bwd] = (rq * qdec_b).astype(BF16)
        rkcat = jnp.concatenate([(rk * kdec_f).astype(BF16), (rk * kdec_b).astype(BF16)], axis=1)
        sr_s[i] = _dot_tn(rv, rkcat)
        return carry

    lax.fori_loop(0, n_chunks, pass_a, 0, unroll=2)

    cdec_f = jnp.exp(CHUNK * lgf)
    cdec_b = jnp.exp(CHUNK * lgb)
    sf, rf = sf0_ref[0, 0], rf0_ref[0, 0]
    for i in range(n_chunks):
        inc_s, inc_r = sh_s[i, :, fwd], sr_s[i, :, fwd]
        sh_s[i, :, fwd] = sf
        sr_s[i, :, fwd] = rf
        sf = sf * dec_s[i, :, fwd] + inc_s
        rf = rf * cdec_f + inc_r
    sb, rb = sb0_ref[0, 0], rb0_ref[0, 0]
    for i in reversed(range(n_chunks)):
        inc_s, inc_r = sh_s[i, :, bwd], sr_s[i, :, bwd]
        sh_s[i, :, bwd] = sb
        sr_s[i, :, bwd] = rb
        sb = sb * dec_s[i, :, bwd] + inc_s
        rb = rb * cdec_b + inc_r

    def pass_c(i, carry):
        rows = pl.ds(pl.multiple_of(i * CHUNK, CHUNK), CHUNK)
        o = oh_s[rows, :] + _dot_nt(qcat_s[rows, :], sh_s[i].astype(BF16))
        hg = o * jax.nn.sigmoid(og_ref[0, rows, :].astype(F32))
        hgo_ref[0, rows, :] = _rms(hg, hgn_ref[...]).astype(BF16)
        y = or_s[rows, :] + _dot_nt(rqcat_s[rows, :], sr_s[i].astype(BF16))
        rg = rg_ref[0, rows, :].astype(F32)
        ret = _rms(y, rn_ref[...]) * (rg * jax.nn.sigmoid(rg))
        reto_ref[0, rows, :] = ret.astype(BF16)
        return carry

    lax.fori_loop(0, n_chunks, pass_c, 0, unroll=2)


def _mixer(p, lb, log_gamma, hg_norm, ret_norm, cos_t, sin_t, states):
    batch, seq, _ = p.shape
    n_chunks = seq // CHUNK

    def group(g):
        return pl.BlockSpec((1, seq, DH), lambda b, h: (b, 0, g * HEADS + h))

    per_head_row = pl.BlockSpec((1, DH), lambda b, h: (0, h))
    table = pl.BlockSpec((seq, DH), lambda b, h: (0, 0))
    state = pl.BlockSpec((1, 1, DH, DH), lambda b, h: (b, h, 0, 0))
    out_spec = pl.BlockSpec((1, seq, DH), lambda b, h: (b, 0, h))
    out_shape = jax.ShapeDtypeStruct((batch, seq, GROUP_W), BF16)
    return pl.pallas_call(
        _mixer_kernel,
        grid=(batch, HEADS),
        in_specs=[pl.BlockSpec(memory_space=pltpu.SMEM)]
        + [group(g) for g in (0, 1, 2, 3, 4, 5, 6, 7, 8)]
        + [pl.BlockSpec((2, DH), lambda b, h: (0, h)), per_head_row, per_head_row, table, table,
           state, state, state, state],
        out_specs=[out_spec, out_spec],
        out_shape=[out_shape, out_shape],
        scratch_shapes=[
            pltpu.VMEM((seq, DH), F32), pltpu.VMEM((seq, DH), F32),
            pltpu.VMEM((seq, 2 * DH), BF16), pltpu.VMEM((seq, 2 * DH), BF16),
            pltpu.VMEM((n_chunks, DH, 2 * DH), F32),
            pltpu.VMEM((n_chunks, DH, 2 * DH), F32),
            pltpu.VMEM((n_chunks, 1, 2 * DH), F32),
        ],
        compiler_params=pltpu.CompilerParams(
            dimension_semantics=("parallel", "parallel"), vmem_limit_bytes=VMEM_LIMIT),
        name="token_mixer",
    )(log_gamma, p, p, p, p, p, p, p, p, p, lb, hg_norm, ret_norm, cos_t, sin_t, *states)


HALO = 8


def _ffn_kernel(x_ref, hgo_ref, reto_ref, g1_ref, sh2_ref, sc2_ref, g2_ref,
                woa_ref, wob_ref, n2_ref, fin_ref, wg_ref, wu_ref, cw_ref, cb_ref, wd_ref,
                o_ref, h2_s, acc_s, gate_s):
    j = pl.program_id(1)
    seq = x_ref.shape[1]
    tiles = [pl.ds(r, ROW_TILE) for r in range(0, seq, ROW_TILE)]

    @pl.when(j == 0)
    def _():
        for rows in tiles:
            proj = _dot(hgo_ref[0, rows, :], woa_ref[...]) + _dot(reto_ref[0, rows, :], wob_ref[...])
            x1 = x_ref[0, rows, :] + g1_ref[0] * proj
            o_ref[0, rows, :] = x1
            h2 = _rms(x1, n2_ref[...]) * (1.0 + sc2_ref[0]) + sh2_ref[0]
            h2_s[rows, :] = h2.astype(BF16)
            acc_s[rows, :] = jnp.zeros((ROW_TILE, D_MODEL), F32)
        gate_s[0:HALO, :] = jnp.zeros((HALO, FF_TILE), F32)
        gate_s[HALO + seq:HALO + seq + HALO, :] = jnp.zeros((HALO, FF_TILE), F32)

    for r in range(0, seq, ROW_TILE):
        gate_s[HALO + r:HALO + r + ROW_TILE, :] = _dot(h2_s[r:r + ROW_TILE, :], wg_ref[...])
    for r in range(0, seq, ROW_TILE):
        up = _dot(h2_s[r:r + ROW_TILE, :], wu_ref[...])
        conv = (gate_s[HALO + r - 1:HALO + r - 1 + ROW_TILE, :] * cw_ref[0:1, :]
                + gate_s[HALO + r:HALO + r + ROW_TILE, :] * cw_ref[1:2, :]
                + gate_s[HALO + r + 1:HALO + r + 1 + ROW_TILE, :] * cw_ref[2:3, :]
                + cb_ref[...])
        act = (conv * jax.nn.sigmoid(conv) * up).astype(BF16)
        acc_s[r:r + ROW_TILE, :] += _dot(act, wd_ref[...])

    @pl.when(j == pl.num_programs(1) - 1)
    def _():
        for rows in tiles:
            x2 = o_ref[0, rows, :] + g2_ref[0] * acc_s[rows, :]
            o_ref[0, rows, :] = _rms(x2, fin_ref[...])


def _out_ffn(x, hgo, reto, mod3, w_out_bf16, norm2_g, final_g, w_up_bf16, conv_w, conv_b, w_down_bf16):
    batch, seq, _ = x.shape
    n_ff = D_FF // FF_TILE

    def mod_spec(chunk):
        return pl.BlockSpec((1, 1, D_MODEL), lambda b, j: (b, 0, chunk))

    once = pl.Buffered(1)
    row_vec = pl.BlockSpec((1, D_MODEL), lambda b, j: (0, 0))
    return pl.pallas_call(
        _ffn_kernel,
        grid=(batch, n_ff),
        in_specs=[
            pl.BlockSpec((1, seq, D_MODEL), lambda b, j: (b, 0, 0), pipeline_mode=once),
            pl.BlockSpec((1, seq, GROUP_W), lambda b, j: (b, 0, 0), pipeline_mode=once),
            pl.BlockSpec((1, seq, GROUP_W), lambda b, j: (b, 0, 0), pipeline_mode=once),
            mod_spec(2), mod_spec(3), mod_spec(4), mod_spec(5),
            pl.BlockSpec((GROUP_W, D_MODEL), lambda b, j: (0, 0)),
            pl.BlockSpec((GROUP_W, D_MODEL), lambda b, j: (1, 0)),
            row_vec, row_vec,
            pl.BlockSpec((D_MODEL, FF_TILE), lambda b, j: (0, j)),
            pl.BlockSpec((D_MODEL, FF_TILE), lambda b, j: (0, n_ff + j)),
            pl.BlockSpec((3, FF_TILE), lambda b, j: (0, j)),
            pl.BlockSpec((1, FF_TILE), lambda b, j: (0, j)),
            pl.BlockSpec((FF_TILE, D_MODEL), lambda b, j: (j, 0)),
        ],
        out_specs=pl.BlockSpec((1, seq, D_MODEL), lambda b, j: (b, 0, 0)),
        out_shape=jax.ShapeDtypeStruct((batch, seq, D_MODEL), F32),
        scratch_shapes=[
            pltpu.VMEM((seq, D_MODEL), BF16),
            pltpu.VMEM((seq, D_MODEL), F32),
            pltpu.VMEM((seq + 2 * HALO, FF_TILE), F32),
        ],
        compiler_params=pltpu.CompilerParams(
            dimension_semantics=("parallel", "arbitrary"), vmem_limit_bytes=VMEM_LIMIT),
        name="out_proj_ffn",
    )(x, hgo, reto, mod3, mod3, mod3, mod3, w_out_bf16, w_out_bf16, norm2_g, final_g,
      w_up_bf16, w_up_bf16, conv_w, conv_b, w_down_bf16)


def _rope_tables(seq):
    quarter = DH // 4
    freqs = ROPE_THETA ** (-jnp.arange(quarter, dtype=F32) / quarter)
    t = jnp.arange(seq)
    rows = (t // GRID_W).astype(F32)
    cols = (t % GRID_W).astype(F32)
    ang_r = rows[:, None] * freqs[None, :]
    ang_c = cols[:, None] * freqs[None, :]
    cos = jnp.concatenate([jnp.cos(ang_r)] * 2 + [jnp.cos(ang_c)] * 2, axis=-1)
    sin = jnp.concatenate([-jnp.sin(ang_r), jnp.sin(ang_r), -jnp.sin(ang_c), jnp.sin(ang_c)], axis=-1)
    return cos, sin


def kernel(x, c, ctx, c_ctx, w_mod, b_mod, norm1_g, w_in, hgrn_lb, hgrn_norm_g, ret_decay,
           ret_norm_g, w_out, norm2_g, w_up, conv_w, conv_b, w_down, final_g):
    batch, seq, _ = x.shape
    n_ctx = ctx.shape[1]
    assert w_mod.shape[0] == 1, "single-layer block"
    assert seq % CHUNK == 0 and seq % ROW_TILE == 0 and (n_ctx & (n_ctx - 1)) == 0

    lb = jnp.cumsum(jax.nn.softmax(hgrn_lb.astype(F32), axis=1), axis=1)[:, 0]
    log_gamma = jax.nn.log_sigmoid(ret_decay[0].astype(F32))
    cos_t, sin_t = _rope_tables(seq)

    n_mod_rows = -(-(batch + 1) // 8) * 8
    c_rows = jnp.zeros((n_mod_rows, D_MODEL), F32).at[:batch].set(c).at[batch].set(c_ctx)
    mod = _modulation(c_rows, w_mod[0], b_mod[0][None, :])
    mod3 = mod.reshape(n_mod_rows, 1, 6 * D_MODEL)

    w_in_b = w_in[0].astype(BF16)
    norm1 = norm1_g[0][None, :]
    p = _in_projection(x.reshape(batch * seq, D_MODEL), mod3[:batch], seq // ROW_TILE,
                       norm1, w_in_b, ROW_TILE)
    ctx_tile = min(ROW_TILE, batch * n_ctx)
    p_ctx = _in_projection(ctx.reshape(batch * n_ctx, D_MODEL), mod3[batch:batch + 1],
                           batch * n_ctx // ctx_tile + 1, norm1, w_in_b, ctx_tile)

    states = _context_states(p_ctx.reshape(batch, n_ctx, IN_WIDTH), lb, log_gamma)
    hgo, reto = _mixer(p.reshape(batch, seq, IN_WIDTH), lb, log_gamma,
                       hgrn_norm_g[0][None, :], ret_norm_g[0][None, :], cos_t, sin_t, states)

    return _out_ffn(x, hgo, reto, mod3[:batch], w_out[0].astype(BF16), norm2_g[0][None, :],
                    final_g[None, :], w_up[0].astype(BF16), conv_w[0], conv_b[0][None, :],
                    w_down[0].astype(BF16))
```

```python
import functools

import jax
import jax.numpy as jnp
from jax import lax
from jax.experimental import pallas as pl
from jax.experimental.pallas import tpu as pltpu

F32 = jnp.float32
BF16 = jnp.bfloat16

D_MODEL = 1024
HEADS = 4
DH = 128
GROUP_W = HEADS * DH
N_GROUPS = 9
IN_WIDTH = N_GROUPS * GROUP_W
D_FF = 2816
GRID_W = 64
ROPE_THETA = 10000.0
EPS = 1e-6
LOG2E = 1.4426950408889634

CHUNK = 128
FF_TILE = 256
ROW_TILE = 512
VMEM_LIMIT = 56 * 1024 * 1024


def _dot(a, b):
    return jnp.dot(a, b, preferred_element_type=F32)


def _dot_nt(a, b):
    return lax.dot_general(a, b, (((1,), (1,)), ((), ())), preferred_element_type=F32)


def _dot_tn(a, b):
    return lax.dot_general(a, b, (((0,), (0,)), ((), ())), preferred_element_type=F32)


def _rms(x, gain):
    return x * lax.rsqrt(jnp.mean(x * x, axis=-1, keepdims=True) + EPS) * gain


def _cumsum_rows(x):
    rows = x.shape[0]
    row = lax.broadcasted_iota(jnp.int32, x.shape, 0)
    shift = 1
    while shift < rows:
        x = x + jnp.where(row >= shift, pltpu.roll(x, shift, axis=0), 0.0)
        shift *= 2
    return x


def _mod_kernel(c_ref, w_ref, b_ref, o_ref):
    c = c_ref[...]
    a = (c * jax.nn.sigmoid(c)).astype(BF16)
    o_ref[...] = _dot(a, w_ref[...].astype(BF16)) + b_ref[...]


def _modulation(c_rows, w_mod, b_mod):
    n_rows = c_rows.shape[0]
    width = w_mod.shape[1]
    tile = D_MODEL
    return pl.pallas_call(
        _mod_kernel,
        grid=(width // tile,),
        in_specs=[
            pl.BlockSpec((n_rows, D_MODEL), lambda j: (0, 0)),
            pl.BlockSpec((D_MODEL, tile), lambda j: (0, j)),
            pl.BlockSpec((1, tile), lambda j: (0, j)),
        ],
        out_specs=pl.BlockSpec((n_rows, tile), lambda j: (0, j)),
        out_shape=jax.ShapeDtypeStruct((n_rows, width), F32),
        compiler_params=pltpu.CompilerParams(dimension_semantics=("parallel",)),
        name="modulation",
    )(c_rows, w_mod, b_mod)


def _inproj_kernel(x_ref, sh_ref, sc_ref, g_ref, w_ref, o_ref):
    x = x_ref[...]
    h = _rms(x, g_ref[...]) * (1.0 + sc_ref[0]) + sh_ref[0]
    hb = h.astype(BF16)
    for g in range(N_GROUPS):
        cols = slice(g * GROUP_W, (g + 1) * GROUP_W)
        o_ref[:, cols] = _dot(hb, w_ref[:, cols]).astype(BF16)


def _in_projection(x2d, mod3, tiles_per_mod_row, norm_g, w_in_bf16, row_tile):
    n_rows = x2d.shape[0]

    def mod_spec(chunk):
        return pl.BlockSpec((1, 1, D_MODEL), lambda i: (i // tiles_per_mod_row, 0, chunk))

    return pl.pallas_call(
        _inproj_kernel,
        grid=(n_rows // row_tile,),
        in_specs=[
            pl.BlockSpec((row_tile, D_MODEL), lambda i: (i, 0)),
            mod_spec(0),
            mod_spec(1),
            pl.BlockSpec((1, D_MODEL), lambda i: (0, 0)),
            pl.BlockSpec((D_MODEL, IN_WIDTH), lambda i: (0, 0)),
        ],
        out_specs=pl.BlockSpec((row_tile, IN_WIDTH), lambda i: (i, 0)),
        out_shape=jax.ShapeDtypeStruct((n_rows, IN_WIDTH), BF16),
        compiler_params=pltpu.CompilerParams(
            dimension_semantics=("parallel",), vmem_limit_bytes=VMEM_LIMIT),
        name="in_projection",
    )(x2d, mod3, mod3, norm_g, w_in_bf16)


def _forget(z, lb):
    f = lb + (1.0 - lb) * jax.nn.sigmoid(z)
    return f, jnp.log(f)


def _ctx_kernel(lg_ref, v_ref, zf_ref, zb_ref, rk_ref, rv_ref, lb_ref,
                sf_ref, sb_ref, rf_ref, rb_ref):
    h = pl.program_id(1)
    n = v_ref.shape[1]
    v = v_ref[0]
    ff, lff = _forget(zf_ref[0].astype(F32), lb_ref[0:1, :])
    fb, lfb = _forget(zb_ref[0].astype(F32), lb_ref[1:2, :])
    bf = _cumsum_rows(lff)
    bb = _cumsum_rows(lfb)
    kf = (1.0 - ff) * jnp.exp(bf[n - 1:n, :] - bf)
    kb = (1.0 - fb) * jnp.exp(bb - lfb)
    sf_ref[0, 0] = _dot_tn(v, kf.astype(BF16))
    sb_ref[0, 0] = _dot_tn(v, kb.astype(BF16))

    pos = lax.broadcasted_iota(jnp.int32, (n, DH), 0).astype(F32)
    rk = rk_ref[0].astype(F32) * (DH ** -0.5)
    rv = rv_ref[0]
    wf = jnp.exp((n - 1.0 - pos) * lg_ref[0, h])
    wb = jnp.exp(pos * lg_ref[1, h])
    rf_ref[0, 0] = _dot_tn(rv, (rk * wf).astype(BF16))
    rb_ref[0, 0] = _dot_tn(rv, (rk * wb).astype(BF16))


def _context_states(p_ctx, lb, log_gamma):
    batch, n_ctx, _ = p_ctx.shape

    def group(g):
        return pl.BlockSpec((1, n_ctx, DH), lambda b, h: (b, 0, g * HEADS + h))

    state = pl.BlockSpec((1, 1, DH, DH), lambda b, h: (b, h, 0, 0))
    state_shape = jax.ShapeDtypeStruct((batch, HEADS, DH, DH), F32)
    return pl.pallas_call(
        _ctx_kernel,
        grid=(batch, HEADS),
        in_specs=[
            pl.BlockSpec(memory_space=pltpu.SMEM),
            group(1), group(2), group(3), group(6), group(7),
            pl.BlockSpec((2, DH), lambda b, h: (0, h)),
        ],
        out_specs=[state, state, state, state],
        out_shape=[state_shape] * 4,
        compiler_params=pltpu.CompilerParams(dimension_semantics=("parallel", "parallel")),
        name="context_states",
    )(log_gamma, p_ctx, p_ctx, p_ctx, p_ctx, p_ctx, lb)


def _anchor(phi, half, forward):
    rows = phi.shape[0]
    pick = half - 1 if forward else half
    block = 2 * half
    p3 = phi.reshape(rows // 8, 8, DH)
    sub = lax.broadcasted_iota(jnp.int32, p3.shape, 1)
    out = None
    for start in range(0, 8, block):
        a = jnp.broadcast_to(p3[:, start + pick:start + pick + 1, :], p3.shape)
        out = a if out is None else jnp.where(sub >= start, a, out)
    return out.reshape(rows, DH)


def _level_operands(q, kf, kb, ff, fb, phif, phib, half, row):
    n = q.shape[0]
    if half == 1:
        odd = (row & 1) != 0
        zz = q * jnp.where(odd, ff, fb)
        ww = jnp.where(odd, kb, kf)
    elif half >= 8:
        zs, ws = [], []
        for b0 in range(0, n, 2 * half):
            first = slice(b0, b0 + half)
            second = slice(b0 + half, b0 + 2 * half)
            af = phif[b0 + half - 1:b0 + half, :]
            ab = phib[b0 + half:b0 + half + 1, :]
            zs += [q[first] * jnp.exp2(phib[first] - ab), q[second] * jnp.exp2(phif[second] - af)]
            ws += [kf[first] * jnp.exp2(af - phif[first]), kb[second] * jnp.exp2(ab - phib[second])]
        zz = jnp.concatenate(zs, axis=0)
        ww = jnp.concatenate(ws, axis=0)
    else:
        af = _anchor(phif, half, True)
        ab = _anchor(phib, half, False)
        second = (row & half) != 0
        zz = q * jnp.exp2(jnp.where(second, phif - af, phib - ab))
        ww = jnp.where(second, kb, kf) * jnp.exp2(jnp.where(second, ab - phib, af - phif))
    return zz.astype(BF16), ww.astype(BF16)


def _rope(t, cos, sin_signed, first_quarter):
    swapped = jnp.where(first_quarter, pltpu.roll(t, DH - DH // 4, axis=1),
                        pltpu.roll(t, DH // 4, axis=1))
    return t * cos + swapped * sin_signed


def _mixer_kernel(lg_ref, q_ref, v_ref, zf_ref, zb_ref, og_ref, rq_ref, rk_ref, rv_ref, rg_ref,
                  lb_ref, hgn_ref, rn_ref, cos_ref, sin_ref,
                  sf0_ref, sb0_ref, rf0_ref, rb0_ref,
                  hgo_ref, reto_ref,
                  oh_s, or_s, qcat_s, rqcat_s, sh_s, sr_s, dec_s):
    head = pl.program_id(1)
    seq = q_ref.shape[1]
    n_chunks = seq // CHUNK
    lgf = lg_ref[0, head]
    lgb = lg_ref[1, head]
    lbf = lb_ref[0:1, :]
    lbb = lb_ref[1:2, :]
    fwd = slice(0, DH)
    bwd = slice(DH, 2 * DH)

    row = lax.broadcasted_iota(jnp.int32, (CHUNK, DH), 0)
    lane = lax.broadcasted_iota(jnp.int32, (CHUNK, DH), 1)
    first_quarter = (lane % (DH // 2)) < (DH // 4)
    t_idx = lax.broadcasted_iota(jnp.int32, (CHUNK, CHUNK), 0)
    s_idx = lax.broadcasted_iota(jnp.int32, (CHUNK, CHUNK), 1)
    split = t_idx ^ s_idx
    rel = (t_idx - s_idx).astype(F32)
    ret_decay = jnp.where(rel >= 0, jnp.exp(rel * lgf), 0.0) + jnp.where(rel <= 0, jnp.exp(-rel * lgb), 0.0)
    pos = row.astype(F32)
    qdec_f = jnp.exp((pos + 1.0) * lgf)
    kdec_f = jnp.exp((CHUNK - 1.0 - pos) * lgf)
    qdec_b = jnp.exp((CHUNK - pos) * lgb)
    kdec_b = jnp.exp(pos * lgb)

    def pass_a(i, carry):
        rows = pl.ds(pl.multiple_of(i * CHUNK, CHUNK), CHUNK)
        q = q_ref[0, rows, :].astype(F32)
        v = v_ref[0, rows, :]
        ff, lff = _forget(zf_ref[0, rows, :].astype(F32), lbf)
        fb, lfb = _forget(zb_ref[0, rows, :].astype(F32), lbb)
        lff = lff * LOG2E
        lfb = lfb * LOG2E
        kf = 1.0 - ff
        kb = 1.0 - fb
        cf = _cumsum_rows(lff)
        cb = _cumsum_rows(lfb)
        phif = cf
        phib = lfb - cb

        attn = None
        half = CHUNK // 2
        while half >= 1:
            zz, ww = _level_operands(q, kf, kb, ff, fb, phif, phib, half, row)
            a = _dot_nt(zz, ww)
            attn = a if attn is None else jnp.where(split < 2 * half, a, attn)
            half //= 2
        attn = jnp.where(split == 0, 0.0, attn)
        diag = jnp.sum(q * (kf + kb), axis=-1, keepdims=True)
        oh_s[rows, :] = _dot(attn.astype(BF16), v) + diag * v.astype(F32)

        last_f = cf[CHUNK - 1:CHUNK, :]
        last_b = cb[CHUNK - 1:CHUNK, :]
        qcat_s[rows, fwd] = (q * jnp.exp2(cf)).astype(BF16)
        qcat_s[rows, bwd] = (q * jnp.exp2(last_b + phib)).astype(BF16)
        kcat = jnp.concatenate([(kf * jnp.exp2(last_f - cf)).astype(BF16),
                                (kb * jnp.exp2(cb - lfb)).astype(BF16)], axis=1)
        sh_s[i] = _dot_tn(v, kcat)
        dec_s[i, :, fwd] = jnp.exp2(last_f)
        dec_s[i, :, bwd] = jnp.exp2(last_b)

        cos = cos_ref[rows, :]
        sin = sin_ref[rows, :]
        rq = _rope(rq_ref[0, rows, :].astype(F32), cos, sin, first_quarter)
        rk = _rope(rk_ref[0, rows, :].astype(F32) * (DH ** -0.5), cos, sin, first_quarter)
        rv = rv_ref[0, rows, :]
        scores = _dot_nt(rq.astype(BF16), rk.astype(BF16)) * ret_decay
        or_s[rows, :] = _dot(scores.astype(BF16), rv)
        rqcat_s[rows, fwd] = (rq * qdec_f).astype(BF16)
        rqcat_s[rows, bwd] = (rq * qdec_b).astype(BF16)
        rkcat = jnp.concatenate([(rk * kdec_f).astype(BF16), (rk * kdec_b).astype(BF16)], axis=1)
        sr_s[i] = _dot_tn(rv, rkcat)
        return carry

    lax.fori_loop(0, n_chunks, pass_a, 0, unroll=4)

    cdec_f = jnp.exp(CHUNK * lgf)
    cdec_b = jnp.exp(CHUNK * lgb)
    sf, rf = sf0_ref[0, 0], rf0_ref[0, 0]
    for i in range(n_chunks):
        inc_s, inc_r = sh_s[i, :, fwd], sr_s[i, :, fwd]
        sh_s[i, :, fwd] = sf
        sr_s[i, :, fwd] = rf
        sf = sf * dec_s[i, :, fwd] + inc_s
        rf = rf * cdec_f + inc_r
    sb, rb = sb0_ref[0, 0], rb0_ref[0, 0]
    for i in reversed(range(n_chunks)):
        inc_s, inc_r = sh_s[i, :, bwd], sr_s[i, :, bwd]
        sh_s[i, :, bwd] = sb
        sr_s[i, :, bwd] = rb
        sb = sb * dec_s[i, :, bwd] + inc_s
        rb = rb * cdec_b + inc_r

    def pass_c(i, carry):
        rows = pl.ds(pl.multiple_of(i * CHUNK, CHUNK), CHUNK)
        o = oh_s[rows, :] + _dot_nt(qcat_s[rows, :], sh_s[i].astype(BF16))
        hg = o * jax.nn.sigmoid(og_ref[0, rows, :].astype(F32))
        hgo_ref[0, rows, :] = _rms(hg, hgn_ref[...]).astype(BF16)
        y = or_s[rows, :] + _dot_nt(rqcat_s[rows, :], sr_s[i].astype(BF16))
        rg = rg_ref[0, rows, :].astype(F32)
        ret = _rms(y, rn_ref[...]) * (rg * jax.nn.sigmoid(rg))
        reto_ref[0, rows, :] = ret.astype(BF16)
        return carry

    lax.fori_loop(0, n_chunks, pass_c, 0, unroll=4)


def _mixer(p, lb, log_gamma, hg_norm, ret_norm, cos_t, sin_t, states):
    batch, seq, _ = p.shape
    n_chunks = seq // CHUNK

    def group(g):
        return pl.BlockSpec((1, seq, DH), lambda b, h: (b, 0, g * HEADS + h))

    per_head_row = pl.BlockSpec((1, DH), lambda b, h: (0, h))
    table = pl.BlockSpec((seq, DH), lambda b, h: (0, 0))
    state = pl.BlockSpec((1, 1, DH, DH), lambda b, h: (b, h, 0, 0))
    out_spec = pl.BlockSpec((1, seq, DH), lambda b, h: (b, 0, h))
    out_shape = jax.ShapeDtypeStruct((batch, seq, GROUP_W), BF16)
    return pl.pallas_call(
        _mixer_kernel,
        grid=(batch, HEADS),
        in_specs=[pl.BlockSpec(memory_space=pltpu.SMEM)]
        + [group(g) for g in (0, 1, 2, 3, 4, 5, 6, 7, 8)]
        + [pl.BlockSpec((2, DH), lambda b, h: (0, h)), per_head_row, per_head_row, table, table,
           state, state, state, state],
        out_specs=[out_spec, out_spec],
        out_shape=[out_shape, out_shape],
        scratch_shapes=[
            pltpu.VMEM((seq, DH), F32), pltpu.VMEM((seq, DH), F32),
            pltpu.VMEM((seq, 2 * DH), BF16), pltpu.VMEM((seq, 2 * DH), BF16),
            pltpu.VMEM((n_chunks, DH, 2 * DH), F32),
            pltpu.VMEM((n_chunks, DH, 2 * DH), F32),
            pltpu.VMEM((n_chunks, 1, 2 * DH), F32),
        ],
        compiler_params=pltpu.CompilerParams(
            dimension_semantics=("parallel", "parallel"), vmem_limit_bytes=VMEM_LIMIT),
        name="token_mixer",
    )(log_gamma, p, p, p, p, p, p, p, p, p, lb, hg_norm, ret_norm, cos_t, sin_t, *states)


HALO = 8


def _ffn_kernel(x_ref, hgo_ref, reto_ref, g1_ref, sh2_ref, sc2_ref, g2_ref,
                woa_ref, wob_ref, n2_ref, fin_ref, wgu_ref, cw_ref, cb_ref, wd_ref,
                o_ref, h2_s, acc_s, gate_s, up_s):
    j = pl.program_id(1)
    last = pl.num_programs(1) - 1
    seq = x_ref.shape[1]
    starts = range(0, seq, ROW_TILE)

    def out_projection():
        for r in starts:
            rows = slice(r, r + ROW_TILE)
            proj = _dot(hgo_ref[0, rows, :], woa_ref[...]) + _dot(reto_ref[0, rows, :], wob_ref[...])
            x1 = x_ref[0, rows, :] + g1_ref[0] * proj
            o_ref[0, rows, :] = x1
            h2 = _rms(x1, n2_ref[...]) * (1.0 + sc2_ref[0]) + sh2_ref[0]
            h2_s[rows, :] = h2.astype(BF16)
        gate_s[0:HALO, :] = jnp.zeros((HALO, FF_TILE), F32)
        gate_s[HALO + seq:HALO + seq + HALO, :] = jnp.zeros((HALO, FF_TILE), F32)

    def ffn_tile(first, finalize):
        if first:
            out_projection()
        for r in starts:
            gu = _dot(h2_s[r:r + ROW_TILE, :], wgu_ref[...])
            gate_s[HALO + r:HALO + r + ROW_TILE, :] = gu[:, :FF_TILE]
            up_s[r:r + ROW_TILE, :] = gu[:, FF_TILE:]
        for r in starts:
            rows = slice(r, r + ROW_TILE)
            conv = (gate_s[HALO + r - 1:HALO + r - 1 + ROW_TILE, :] * cw_ref[0:1, :]
                    + gate_s[HALO + r:HALO + r + ROW_TILE, :] * cw_ref[1:2, :]
                    + gate_s[HALO + r + 1:HALO + r + 1 + ROW_TILE, :] * cw_ref[2:3, :]
                    + cb_ref[...])
            act = (conv * jax.nn.sigmoid(conv) * up_s[rows, :]).astype(BF16)
            ffn = _dot(act, wd_ref[...])
            if not first:
                ffn += acc_s[rows, :]
            if finalize:
                x2 = o_ref[0, rows, :] + g2_ref[0] * ffn
                o_ref[0, rows, :] = _rms(x2, fin_ref[...])
            else:
                acc_s[rows, :] = ffn

    @pl.when(j == 0)
    def _():
        ffn_tile(True, False)

    @pl.when(jnp.logical_and(j > 0, j < last))
    def _():
        ffn_tile(False, False)

    @pl.when(j == last)
    def _():
        ffn_tile(False, True)


def _out_ffn(x, hgo, reto, mod3, w_out_bf16, norm2_g, final_g, w_up, conv_w, conv_b, w_down_bf16):
    batch, seq, _ = x.shape
    n_ff = D_FF // FF_TILE
    w_gu_bf16 = (w_up.reshape(D_MODEL, 2, n_ff, FF_TILE).transpose(0, 2, 1, 3)
                 .reshape(D_MODEL, 2 * D_FF).astype(BF16))

    def mod_spec(chunk):
        return pl.BlockSpec((1, 1, D_MODEL), lambda b, j: (b, 0, chunk))

    once = pl.Buffered(1)
    row_vec = pl.BlockSpec((1, D_MODEL), lambda b, j: (0, 0))
    return pl.pallas_call(
        _ffn_kernel,
        grid=(batch, n_ff),
        in_specs=[
            pl.BlockSpec((1, seq, D_MODEL), lambda b, j: (b, 0, 0), pipeline_mode=once),
            pl.BlockSpec((1, seq, GROUP_W), lambda b, j: (b, 0, 0), pipeline_mode=once),
            pl.BlockSpec((1, seq, GROUP_W), lambda b, j: (b, 0, 0), pipeline_mode=once),
            mod_spec(2), mod_spec(3), mod_spec(4), mod_spec(5),
            pl.BlockSpec((GROUP_W, D_MODEL), lambda b, j: (0, 0)),
            pl.BlockSpec((GROUP_W, D_MODEL), lambda b, j: (1, 0)),
            row_vec, row_vec,
            pl.BlockSpec((D_MODEL, 2 * FF_TILE), lambda b, j: (0, j)),
            pl.BlockSpec((3, FF_TILE), lambda b, j: (0, j)),
            pl.BlockSpec((1, FF_TILE), lambda b, j: (0, j)),
            pl.BlockSpec((FF_TILE, D_MODEL), lambda b, j: (j, 0)),
        ],
        out_specs=pl.BlockSpec((1, seq, D_MODEL), lambda b, j: (b, 0, 0)),
        out_shape=jax.ShapeDtypeStruct((batch, seq, D_MODEL), F32),
        scratch_shapes=[
            pltpu.VMEM((seq, D_MODEL), BF16),
            pltpu.VMEM((seq, D_MODEL), F32),
            pltpu.VMEM((seq + 2 * HALO, FF_TILE), F32),
            pltpu.VMEM((seq, FF_TILE), F32),
        ],
        compiler_params=pltpu.CompilerParams(
            dimension_semantics=("parallel", "arbitrary"), vmem_limit_bytes=VMEM_LIMIT),
        name="out_proj_ffn",
    )(x, hgo, reto, mod3, mod3, mod3, mod3, w_out_bf16, w_out_bf16, norm2_g, final_g,
      w_gu_bf16, conv_w, conv_b, w_down_bf16)


def _rope_tables(seq):
    quarter = DH // 4
    freqs = ROPE_THETA ** (-jnp.arange(quarter, dtype=F32) / quarter)
    t = jnp.arange(seq)
    rows = (t // GRID_W).astype(F32)
    cols = (t % GRID_W).astype(F32)
    ang_r = rows[:, None] * freqs[None, :]
    ang_c = cols[:, None] * freqs[None, :]
    cos = jnp.concatenate([jnp.cos(ang_r)] * 2 + [jnp.cos(ang_c)] * 2, axis=-1)
    sin = jnp.concatenate([-jnp.sin(ang_r), jnp.sin(ang_r), -jnp.sin(ang_c), jnp.sin(ang_c)], axis=-1)
    return cos, sin


def kernel(x, c, ctx, c_ctx, w_mod, b_mod, norm1_g, w_in, hgrn_lb, hgrn_norm_g, ret_decay,
           ret_norm_g, w_out, norm2_g, w_up, conv_w, conv_b, w_down, final_g):
    batch, seq, _ = x.shape
    n_ctx = ctx.shape[1]
    assert w_mod.shape[0] == 1, "single-layer block"
    assert seq % CHUNK == 0 and seq % ROW_TILE == 0 and (n_ctx & (n_ctx - 1)) == 0

    lb = jnp.cumsum(jax.nn.softmax(hgrn_lb.astype(F32), axis=1), axis=1)[:, 0]
    log_gamma = jax.nn.log_sigmoid(ret_decay[0].astype(F32))
    cos_t, sin_t = _rope_tables(seq)

    n_mod_rows = -(-(batch + 1) // 8) * 8
    c_rows = jnp.zeros((n_mod_rows, D_MODEL), F32).at[:batch].set(c).at[batch].set(c_ctx)
    mod = _modulation(c_rows, w_mod[0], b_mod[0][None, :])
    mod3 = mod.reshape(n_mod_rows, 1, 6 * D_MODEL)

    w_in_b = w_in[0].astype(BF16)
    norm1 = norm1_g[0][None, :]
    p = _in_projection(x.reshape(batch * seq, D_MODEL), mod3[:batch], seq // ROW_TILE,
                       norm1, w_in_b, ROW_TILE)
    ctx_tile = min(ROW_TILE, batch * n_ctx)
    p_ctx = _in_projection(ctx.reshape(batch * n_ctx, D_MODEL), mod3[batch:batch + 1],
                           batch * n_ctx // ctx_tile + 1, norm1, w_in_b, ctx_tile)

    states = _context_states(p_ctx.reshape(batch, n_ctx, IN_WIDTH), lb, log_gamma)
    hgo, reto = _mixer(p.reshape(batch, seq, IN_WIDTH), lb, log_gamma,
                       hgrn_norm_g[0][None, :], ret_norm_g[0][None, :], cos_t, sin_t, states)

    return _out_ffn(x, hgo, reto, mod3[:batch], w_out[0].astype(BF16), norm2_g[0][None, :],
                    final_g[None, :], w_up[0], conv_w[0], conv_b[0][None, :],
                    w_down[0].astype(BF16))
```

```python
import functools

import jax
import jax.numpy as jnp
from jax import lax
from jax.experimental import pallas as pl
from jax.experimental.pallas import tpu as pltpu

F32 = jnp.float32
BF16 = jnp.bfloat16

D_MODEL = 1024
HEADS = 4
DH = 128
GROUP_W = HEADS * DH
N_GROUPS = 9
IN_WIDTH = N_GROUPS * GROUP_W
D_FF = 2816
GRID_W = 64
ROPE_THETA = 10000.0
EPS = 1e-6
LOG2E = 1.4426950408889634

CHUNK = 128
FF_TILE = 256
ROW_TILE = 512
VMEM_LIMIT = 56 * 1024 * 1024


def _dot(a, b):
    return jnp.dot(a, b, preferred_element_type=F32)


def _dot_nt(a, b):
    return lax.dot_general(a, b, (((1,), (1,)), ((), ())), preferred_element_type=F32)


def _dot_tn(a, b):
    return lax.dot_general(a, b, (((0,), (0,)), ((), ())), preferred_element_type=F32)


def _rms(x, gain):
    return x * lax.rsqrt(jnp.mean(x * x, axis=-1, keepdims=True) + EPS) * gain


def _cumsum_rows(x):
    rows = x.shape[0]
    row = lax.broadcasted_iota(jnp.int32, x.shape, 0)
    shift = 1
    while shift < rows:
        x = x + jnp.where(row >= shift, pltpu.roll(x, shift, axis=0), 0.0)
        shift *= 2
    return x


def _mod_kernel(c_ref, w_ref, b_ref, o_ref):
    c = c_ref[...]
    a = (c * jax.nn.sigmoid(c)).astype(BF16)
    o_ref[...] = _dot(a, w_ref[...].astype(BF16)) + b_ref[...]


def _modulation(c_rows, w_mod, b_mod):
    n_rows = c_rows.shape[0]
    width = w_mod.shape[1]
    tile = D_MODEL
    return pl.pallas_call(
        _mod_kernel,
        grid=(width // tile,),
        in_specs=[
            pl.BlockSpec((n_rows, D_MODEL), lambda j: (0, 0)),
            pl.BlockSpec((D_MODEL, tile), lambda j: (0, j)),
            pl.BlockSpec((1, tile), lambda j: (0, j)),
        ],
        out_specs=pl.BlockSpec((n_rows, tile), lambda j: (0, j)),
        out_shape=jax.ShapeDtypeStruct((n_rows, width), F32),
        compiler_params=pltpu.CompilerParams(dimension_semantics=("parallel",)),
        name="modulation",
    )(c_rows, w_mod, b_mod)


def _inproj_kernel(x_ref, sh_ref, sc_ref, g_ref, w_ref, o_ref):
    x = x_ref[...]
    h = _rms(x, g_ref[...]) * (1.0 + sc_ref[0]) + sh_ref[0]
    hb = h.astype(BF16)
    for g in range(N_GROUPS):
        cols = slice(g * GROUP_W, (g + 1) * GROUP_W)
        o_ref[:, cols] = _dot(hb, w_ref[:, cols]).astype(BF16)


def _in_projection(x2d, mod3, tiles_per_mod_row, norm_g, w_in_bf16, row_tile):
    n_rows = x2d.shape[0]

    def mod_spec(chunk):
        return pl.BlockSpec((1, 1, D_MODEL), lambda i: (i // tiles_per_mod_row, 0, chunk))

    return pl.pallas_call(
        _inproj_kernel,
        grid=(n_rows // row_tile,),
        in_specs=[
            pl.BlockSpec((row_tile, D_MODEL), lambda i: (i, 0)),
            mod_spec(0),
            mod_spec(1),
            pl.BlockSpec((1, D_MODEL), lambda i: (0, 0)),
            pl.BlockSpec((D_MODEL, IN_WIDTH), lambda i: (0, 0)),
        ],
        out_specs=pl.BlockSpec((row_tile, IN_WIDTH), lambda i: (i, 0)),
        out_shape=jax.ShapeDtypeStruct((n_rows, IN_WIDTH), BF16),
        compiler_params=pltpu.CompilerParams(
            dimension_semantics=("parallel",), vmem_limit_bytes=VMEM_LIMIT),
        name="in_projection",
    )(x2d, mod3, mod3, norm_g, w_in_bf16)


def _forget(z, lb):
    f = lb + (1.0 - lb) * jax.nn.sigmoid(z)
    return f, jnp.log(f)


def _ctx_kernel(lg_ref, v_ref, zf_ref, zb_ref, rk_ref, rv_ref, lb_ref,
                sf_ref, sb_ref, rf_ref, rb_ref):
    h = pl.program_id(1)
    n = v_ref.shape[1]
    v = v_ref[0]
    ff, lff = _forget(zf_ref[0].astype(F32), lb_ref[0:1, :])
    fb, lfb = _forget(zb_ref[0].astype(F32), lb_ref[1:2, :])
    bf = _cumsum_rows(lff)
    bb = _cumsum_rows(lfb)
    kf = (1.0 - ff) * jnp.exp(bf[n - 1:n, :] - bf)
    kb = (1.0 - fb) * jnp.exp(bb - lfb)
    sf_ref[0, 0] = _dot_tn(v, kf.astype(BF16))
    sb_ref[0, 0] = _dot_tn(v, kb.astype(BF16))

    pos = lax.broadcasted_iota(jnp.int32, (n, DH), 0).astype(F32)
    rk = rk_ref[0].astype(F32) * (DH ** -0.5)
    rv = rv_ref[0]
    wf = jnp.exp((n - 1.0 - pos) * lg_ref[0, h])
    wb = jnp.exp(pos * lg_ref[1, h])
    rf_ref[0, 0] = _dot_tn(rv, (rk * wf).astype(BF16))
    rb_ref[0, 0] = _dot_tn(rv, (rk * wb).astype(BF16))


def _context_states(p_ctx, lb, log_gamma):
    batch, n_ctx, _ = p_ctx.shape

    def group(g):
        return pl.BlockSpec((1, n_ctx, DH), lambda b, h: (b, 0, g * HEADS + h))

    state = pl.BlockSpec((1, 1, DH, DH), lambda b, h: (b, h, 0, 0))
    state_shape = jax.ShapeDtypeStruct((batch, HEADS, DH, DH), F32)
    return pl.pallas_call(
        _ctx_kernel,
        grid=(batch, HEADS),
        in_specs=[
            pl.BlockSpec(memory_space=pltpu.SMEM),
            group(1), group(2), group(3), group(6), group(7),
            pl.BlockSpec((2, DH), lambda b, h: (0, h)),
        ],
        out_specs=[state, state, state, state],
        out_shape=[state_shape] * 4,
        compiler_params=pltpu.CompilerParams(dimension_semantics=("parallel", "parallel")),
        name="context_states",
    )(log_gamma, p_ctx, p_ctx, p_ctx, p_ctx, p_ctx, lb)


def _anchor(phi, half, forward):
    rows = phi.shape[0]
    pick = half - 1 if forward else half
    block = 2 * half
    p3 = phi.reshape(rows // 8, 8, DH)
    sub = lax.broadcasted_iota(jnp.int32, p3.shape, 1)
    out = None
    for start in range(0, 8, block):
        a = jnp.broadcast_to(p3[:, start + pick:start + pick + 1, :], p3.shape)
        out = a if out is None else jnp.where(sub >= start, a, out)
    return out.reshape(rows, DH)


def _level_operands(q, kf, kb, ff, fb, phif, phib, half, row):
    n = q.shape[0]
    if half == 1:
        odd = (row & 1) != 0
        zz = q * jnp.where(odd, ff, fb)
        ww = jnp.where(odd, kb, kf)
    elif half >= 8:
        zs, ws = [], []
        for b0 in range(0, n, 2 * half):
            first = slice(b0, b0 + half)
            second = slice(b0 + half, b0 + 2 * half)
            af = phif[b0 + half - 1:b0 + half, :]
            ab = phib[b0 + half:b0 + half + 1, :]
            zs += [q[first] * jnp.exp2(phib[first] - ab), q[second] * jnp.exp2(phif[second] - af)]
            ws += [kf[first] * jnp.exp2(af - phif[first]), kb[second] * jnp.exp2(ab - phib[second])]
        zz = jnp.concatenate(zs, axis=0)
        ww = jnp.concatenate(ws, axis=0)
    else:
        af = _anchor(phif, half, True)
        ab = _anchor(phib, half, False)
        second = (row & half) != 0
        zz = q * jnp.exp2(jnp.where(second, phif - af, phib - ab))
        ww = jnp.where(second, kb, kf) * jnp.exp2(jnp.where(second, ab - phib, af - phif))
    return zz.astype(BF16), ww.astype(BF16)


def _rope(t, cos, sin_signed, first_quarter):
    swapped = jnp.where(first_quarter, pltpu.roll(t, DH - DH // 4, axis=1),
                        pltpu.roll(t, DH // 4, axis=1))
    return t * cos + swapped * sin_signed


def _mixer_kernel(lg_ref, q_ref, v_ref, zf_ref, zb_ref, og_ref, rq_ref, rk_ref, rv_ref, rg_ref,
                  lb_ref, hgn_ref, rn_ref, cos_ref, sin_ref,
                  sf0_ref, sb0_ref, rf0_ref, rb0_ref,
                  hgo_ref, reto_ref,
                  oh_s, or_s, qcat_s, rqcat_s, sh_s, sr_s, dec_s):
    head = pl.program_id(1)
    seq = q_ref.shape[1]
    n_chunks = seq // CHUNK
    lgf = lg_ref[0, head]
    lgb = lg_ref[1, head]
    lbf = lb_ref[0:1, :]
    lbb = lb_ref[1:2, :]
    fwd = slice(0, DH)
    bwd = slice(DH, 2 * DH)

    row = lax.broadcasted_iota(jnp.int32, (CHUNK, DH), 0)
    lane = lax.broadcasted_iota(jnp.int32, (CHUNK, DH), 1)
    first_quarter = (lane % (DH // 2)) < (DH // 4)
    t_idx = lax.broadcasted_iota(jnp.int32, (CHUNK, CHUNK), 0)
    s_idx = lax.broadcasted_iota(jnp.int32, (CHUNK, CHUNK), 1)
    split = t_idx ^ s_idx
    rel = (t_idx - s_idx).astype(F32)
    ret_decay = jnp.where(rel >= 0, jnp.exp(rel * lgf), 0.0) + jnp.where(rel <= 0, jnp.exp(-rel * lgb), 0.0)
    pos = row.astype(F32)
    qdec_f = jnp.exp((pos + 1.0) * lgf)
    kdec_f = jnp.exp((CHUNK - 1.0 - pos) * lgf)
    qdec_b = jnp.exp((CHUNK - pos) * lgb)
    kdec_b = jnp.exp(pos * lgb)

    def pass_a(i, carry):
        rows = pl.ds(pl.multiple_of(i * CHUNK, CHUNK), CHUNK)
        q = q_ref[0, rows, :].astype(F32)
        v = v_ref[0, rows, :]
        ff, lff = _forget(zf_ref[0, rows, :].astype(F32), lbf)
        fb, lfb = _forget(zb_ref[0, rows, :].astype(F32), lbb)
        lff = lff * LOG2E
        lfb = lfb * LOG2E
        kf = 1.0 - ff
        kb = 1.0 - fb
        cf = _cumsum_rows(lff)
        cb = _cumsum_rows(lfb)
        phif = cf
        phib = lfb - cb

        attn = None
        half = CHUNK // 2
        while half >= 1:
            zz, ww = _level_operands(q, kf, kb, ff, fb, phif, phib, half, row)
            a = _dot_nt(zz, ww)
            attn = a if attn is None else jnp.where(split < 2 * half, a, attn)
            half //= 2
        attn = jnp.where(split == 0, 0.0, attn)
        diag = jnp.sum(q * (kf + kb), axis=-1, keepdims=True)
        oh_s[rows, :] = _dot(attn.astype(BF16), v) + diag * v.astype(F32)

        last_f = cf[CHUNK - 1:CHUNK, :]
        last_b = cb[CHUNK - 1:CHUNK, :]
        qcat_s[rows, fwd] = (q * jnp.exp2(cf)).astype(BF16)
        qcat_s[rows, bwd] = (q * jnp.exp2(last_b + phib)).astype(BF16)
        kcat = jnp.concatenate([(kf * jnp.exp2(last_f - cf)).astype(BF16),
                                (kb * jnp.exp2(cb - lfb)).astype(BF16)], axis=1)
        sh_s[i] = _dot_tn(v, kcat)
        dec_s[i, :, fwd] = jnp.exp2(last_f)
        dec_s[i, :, bwd] = jnp.exp2(last_b)

        cos = cos_ref[rows, :]
        sin = sin_ref[rows, :]
        rq = _rope(rq_ref[0, rows, :].astype(F32), cos, sin, first_quarter)
        rk = _rope(rk_ref[0, rows, :].astype(F32) * (DH ** -0.5), cos, sin, first_quarter)
        rv = rv_ref[0, rows, :]
        scores = _dot_nt(rq.astype(BF16), rk.astype(BF16)) * ret_decay
        or_s[rows, :] = _dot(scores.astype(BF16), rv)
        rqcat_s[rows, fwd] = (rq * qdec_f).astype(BF16)
        rqcat_s[rows, bwd] = (rq * qdec_b).astype(BF16)
        rkcat = jnp.concatenate([(rk * kdec_f).astype(BF16), (rk * kdec_b).astype(BF16)], axis=1)
        sr_s[i] = _dot_tn(rv, rkcat)
        return carry

    lax.fori_loop(0, n_chunks, pass_a, 0, unroll=4)

    cdec_f = jnp.exp(CHUNK * lgf)
    cdec_b = jnp.exp(CHUNK * lgb)
    sf, rf = sf0_ref[0, 0], rf0_ref[0, 0]
    for i in range(n_chunks):
        inc_s, inc_r = sh_s[i, :, fwd], sr_s[i, :, fwd]
        sh_s[i, :, fwd] = sf
        sr_s[i, :, fwd] = rf
        sf = sf * dec_s[i, :, fwd] + inc_s
        rf = rf * cdec_f + inc_r
    sb, rb = sb0_ref[0, 0], rb0_ref[0, 0]
    for i in reversed(range(n_chunks)):
        inc_s, inc_r = sh_s[i, :, bwd], sr_s[i, :, bwd]
        sh_s[i, :, bwd] = sb
        sr_s[i, :, bwd] = rb
        sb = sb * dec_s[i, :, bwd] + inc_s
        rb = rb * cdec_b + inc_r

    def pass_c(i, carry):
        rows = pl.ds(pl.multiple_of(i * CHUNK, CHUNK), CHUNK)
        o = oh_s[rows, :] + _dot_nt(qcat_s[rows, :], sh_s[i].astype(BF16))
        hg = o * jax.nn.sigmoid(og_ref[0, rows, :].astype(F32))
        hgo_ref[0, rows, :] = _rms(hg, hgn_ref[...]).astype(BF16)
        y = or_s[rows, :] + _dot_nt(rqcat_s[rows, :], sr_s[i].astype(BF16))
        rg = rg_ref[0, rows, :].astype(F32)
        ret = _rms(y, rn_ref[...]) * (rg * jax.nn.sigmoid(rg))
        reto_ref[0, rows, :] = ret.astype(BF16)
        return carry

    lax.fori_loop(0, n_chunks, pass_c, 0, unroll=4)


def _mixer(p, lb, log_gamma, hg_norm, ret_norm, cos_t, sin_t, states):
    batch, seq, _ = p.shape
    n_chunks = seq // CHUNK

    def group(g):
        return pl.BlockSpec((1, seq, DH), lambda b, h: (b, 0, g * HEADS + h))

    per_head_row = pl.BlockSpec((1, DH), lambda b, h: (0, h))
    table = pl.BlockSpec((seq, DH), lambda b, h: (0, 0))
    state = pl.BlockSpec((1, 1, DH, DH), lambda b, h: (b, h, 0, 0))
    out_spec = pl.BlockSpec((1, seq, DH), lambda b, h: (b, 0, h))
    out_shape = jax.ShapeDtypeStruct((batch, seq, GROUP_W), BF16)
    return pl.pallas_call(
        _mixer_kernel,
        grid=(batch, HEADS),
        in_specs=[pl.BlockSpec(memory_space=pltpu.SMEM)]
        + [group(g) for g in (0, 1, 2, 3, 4, 5, 6, 7, 8)]
        + [pl.BlockSpec((2, DH), lambda b, h: (0, h)), per_head_row, per_head_row, table, table,
           state, state, state, state],
        out_specs=[out_spec, out_spec],
        out_shape=[out_shape, out_shape],
        scratch_shapes=[
            pltpu.VMEM((seq, DH), F32), pltpu.VMEM((seq, DH), F32),
            pltpu.VMEM((seq, 2 * DH), BF16), pltpu.VMEM((seq, 2 * DH), BF16),
            pltpu.VMEM((n_chunks, DH, 2 * DH), F32),
            pltpu.VMEM((n_chunks, DH, 2 * DH), F32),
            pltpu.VMEM((n_chunks, 1, 2 * DH), F32),
        ],
        compiler_params=pltpu.CompilerParams(
            dimension_semantics=("parallel", "parallel"), vmem_limit_bytes=VMEM_LIMIT),
        name="token_mixer",
    )(log_gamma, p, p, p, p, p, p, p, p, p, lb, hg_norm, ret_norm, cos_t, sin_t, *states)


FFN_ROWS = 1024
HALO = 16


def _ffn_kernel(x_ref, xp_ref, xn_ref, hg_ref, hgp_ref, hgn_ref, rt_ref, rtp_ref, rtn_ref,
                g1_ref, sh2_ref, sc2_ref, g2_ref, woa_ref, wob_ref, n2_ref, fin_ref,
                wup_ref, cw_ref, cb_ref, wd_ref,
                o_ref, h2_s, gate2_s, gate3_s, up2_s, up3_s, acc_s):
    m = pl.program_id(1)
    rows = x_ref.shape[1]
    n_sub = rows // ROW_TILE
    n_ff = D_FF // FF_TILE

    def ext(k):
        lo = 0 if k == 0 else HALO + k * ROW_TILE
        hi = HALO + (k + 1) * ROW_TILE + (HALO if k == n_sub - 1 else 0)
        return lo, hi

    def piece(main_ref, prev_ref, next_ref, k):
        parts = ([prev_ref[0]] if k == 0 else []) + [main_ref[0, k * ROW_TILE:(k + 1) * ROW_TILE, :]] \
            + ([next_ref[0]] if k == n_sub - 1 else [])
        return jnp.concatenate(parts, axis=0) if len(parts) > 1 else parts[0]

    for k in range(n_sub):
        lo, hi = ext(k)
        skip = HALO if k == 0 else 0
        proj = (_dot(piece(hg_ref, hgp_ref, hgn_ref, k), woa_ref[...])
                + _dot(piece(rt_ref, rtp_ref, rtn_ref, k), wob_ref[...]))
        x1 = piece(x_ref, xp_ref, xn_ref, k) + g1_ref[0] * proj
        o_ref[0, k * ROW_TILE:(k + 1) * ROW_TILE, :] = x1[skip:skip + ROW_TILE]
        h2 = _rms(x1, n2_ref[...]) * (1.0 + sc2_ref[0]) + sh2_ref[0]
        h2_s[lo:hi, :] = h2.astype(BF16)

    prev_valid = (m > 0).astype(F32)
    next_valid = (m < pl.num_programs(1) - 1).astype(F32)
    edge_lo = slice(HALO - 8, HALO)
    edge_hi = slice(HALO + rows, HALO + rows + 8)

    gate_bufs = (gate2_s, gate3_s)
    up_bufs = (up2_s, up3_s)

    def gate_up(j):
        gate_s, up_s = gate_bufs[j % 2], up_bufs[j % 2]
        w_gu = jnp.concatenate([wup_ref[:, j * FF_TILE:(j + 1) * FF_TILE],
                                wup_ref[:, D_FF + j * FF_TILE:D_FF + (j + 1) * FF_TILE]], axis=1)
        for k in range(n_sub):
            lo, hi = ext(k)
            skip = HALO if k == 0 else 0
            gu = _dot(h2_s[lo:hi, :], w_gu)
            gate_s[lo:hi, :] = gu[:, :FF_TILE]
            up_s[k * ROW_TILE:(k + 1) * ROW_TILE, :] = gu[skip:skip + ROW_TILE, FF_TILE:]
        gate_s[edge_lo, :] = gate_s[edge_lo, :] * prev_valid
        gate_s[edge_hi, :] = gate_s[edge_hi, :] * next_valid

    gate_up(0)
    for j in range(n_ff):
        if j + 1 < n_ff:
            gate_up(j + 1)
        cols = slice(j * FF_TILE, (j + 1) * FF_TILE)
        gate_s, up_s = gate_bufs[j % 2], up_bufs[j % 2]
        for k in range(n_sub):
            sub = slice(k * ROW_TILE, (k + 1) * ROW_TILE)
            base = HALO + k * ROW_TILE
            conv = (gate_s[base - 1:base - 1 + ROW_TILE, :] * cw_ref[0:1, cols]
                    + gate_s[base:base + ROW_TILE, :] * cw_ref[1:2, cols]
                    + gate_s[base + 1:base + 1 + ROW_TILE, :] * cw_ref[2:3, cols]
                    + cb_ref[:, cols])
            act = (conv * jax.nn.sigmoid(conv) * up_s[sub, :]).astype(BF16)
            ffn = _dot(act, wd_ref[cols, :])
            if j > 0:
                ffn += acc_s[sub, :]
            if j < n_ff - 1:
                acc_s[sub, :] = ffn
            else:
                x2 = o_ref[0, sub, :] + g2_ref[0] * ffn
                o_ref[0, sub, :] = _rms(x2, fin_ref[...])


def _out_ffn(x, hgo, reto, mod3, w_out_bf16, norm2_g, final_g, w_up_bf16, conv_w, conv_b, w_down_bf16):
    batch, seq, _ = x.shape
    rows = min(FFN_ROWS, seq)
    n_row_tiles = seq // rows
    halo_per_tile = rows // HALO
    n_halo_blocks = seq // HALO
    once = pl.Buffered(1)

    def main(width):
        return pl.BlockSpec((1, rows, width), lambda b, m: (b, m, 0))

    def prev(width):
        return pl.BlockSpec((1, HALO, width), lambda b, m: (b, jnp.maximum(m * halo_per_tile - 1, 0), 0))

    def nxt(width):
        return pl.BlockSpec((1, HALO, width),
                            lambda b, m: (b, jnp.minimum((m + 1) * halo_per_tile, n_halo_blocks - 1), 0))

    def mod_spec(chunk):
        return pl.BlockSpec((1, 1, D_MODEL), lambda b, m: (b, 0, chunk))

    def const(shape, index=(0, 0)):
        return pl.BlockSpec(shape, lambda b, m: index, pipeline_mode=once)

    return pl.pallas_call(
        _ffn_kernel,
        grid=(batch, n_row_tiles),
        in_specs=[
            main(D_MODEL), prev(D_MODEL), nxt(D_MODEL),
            main(GROUP_W), prev(GROUP_W), nxt(GROUP_W),
            main(GROUP_W), prev(GROUP_W), nxt(GROUP_W),
            mod_spec(2), mod_spec(3), mod_spec(4), mod_spec(5),
            const((GROUP_W, D_MODEL), (0, 0)),
            const((GROUP_W, D_MODEL), (1, 0)),
            const((1, D_MODEL)), const((1, D_MODEL)),
            const((D_MODEL, 2 * D_FF)),
            const((3, D_FF)), const((1, D_FF)),
            const((D_FF, D_MODEL)),
        ],
        out_specs=pl.BlockSpec((1, rows, D_MODEL), lambda b, m: (b, m, 0)),
        out_shape=jax.ShapeDtypeStruct((batch, seq, D_MODEL), F32),
        scratch_shapes=[
            pltpu.VMEM((rows + 2 * HALO, D_MODEL), BF16),
            pltpu.VMEM((rows + 2 * HALO, FF_TILE), F32),
            pltpu.VMEM((rows + 2 * HALO, FF_TILE), F32),
            pltpu.VMEM((rows, FF_TILE), F32),
            pltpu.VMEM((rows, FF_TILE), F32),
            pltpu.VMEM((rows, D_MODEL), F32),
        ],
        compiler_params=pltpu.CompilerParams(
            dimension_semantics=("parallel", "parallel"), vmem_limit_bytes=VMEM_LIMIT),
        name="out_proj_ffn",
    )(x, x, x, hgo, hgo, hgo, reto, reto, reto, mod3, mod3, mod3, mod3,
      w_out_bf16, w_out_bf16, norm2_g, final_g, w_up_bf16, conv_w, conv_b, w_down_bf16)


def _rope_tables(seq):
    quarter = DH // 4
    freqs = ROPE_THETA ** (-jnp.arange(quarter, dtype=F32) / quarter)
    t = jnp.arange(seq)
    rows = (t // GRID_W).astype(F32)
    cols = (t % GRID_W).astype(F32)
    ang_r = rows[:, None] * freqs[None, :]
    ang_c = cols[:, None] * freqs[None, :]
    cos = jnp.concatenate([jnp.cos(ang_r)] * 2 + [jnp.cos(ang_c)] * 2, axis=-1)
    sin = jnp.concatenate([-jnp.sin(ang_r), jnp.sin(ang_r), -jnp.sin(ang_c), jnp.sin(ang_c)], axis=-1)
    return cos, sin


def kernel(x, c, ctx, c_ctx, w_mod, b_mod, norm1_g, w_in, hgrn_lb, hgrn_norm_g, ret_decay,
           ret_norm_g, w_out, norm2_g, w_up, conv_w, conv_b, w_down, final_g):
    batch, seq, _ = x.shape
    n_ctx = ctx.shape[1]
    assert w_mod.shape[0] == 1, "single-layer block"
    assert seq % CHUNK == 0 and seq % ROW_TILE == 0 and (n_ctx & (n_ctx - 1)) == 0

    lb = jnp.cumsum(jax.nn.softmax(hgrn_lb.astype(F32), axis=1), axis=1)[:, 0]
    log_gamma = jax.nn.log_sigmoid(ret_decay[0].astype(F32))
    cos_t, sin_t = _rope_tables(seq)

    n_mod_rows = -(-(batch + 1) // 8) * 8
    c_rows = jnp.zeros((n_mod_rows, D_MODEL), F32).at[:batch].set(c).at[batch].set(c_ctx)
    mod = _modulation(c_rows, w_mod[0], b_mod[0][None, :])
    mod3 = mod.reshape(n_mod_rows, 1, 6 * D_MODEL)

    w_in_b = w_in[0].astype(BF16)
    norm1 = norm1_g[0][None, :]
    p = _in_projection(x.reshape(batch * seq, D_MODEL), mod3[:batch], seq // ROW_TILE,
                       norm1, w_in_b, ROW_TILE)
    ctx_tile = min(ROW_TILE, batch * n_ctx)
    p_ctx = _in_projection(ctx.reshape(batch * n_ctx, D_MODEL), mod3[batch:batch + 1],
                           batch * n_ctx // ctx_tile + 1, norm1, w_in_b, ctx_tile)

    states = _context_states(p_ctx.reshape(batch, n_ctx, IN_WIDTH), lb, log_gamma)
    hgo, reto = _mixer(p.reshape(batch, seq, IN_WIDTH), lb, log_gamma,
                       hgrn_norm_g[0][None, :], ret_norm_g[0][None, :], cos_t, sin_t, states)

    return _out_ffn(x, hgo, reto, mod3[:batch], w_out[0].astype(BF16), norm2_g[0][None, :],
                    final_g[None, :], w_up[0].astype(BF16), conv_w[0], conv_b[0][None, :],
                    w_down[0].astype(BF16))
```

```python
import jax
import jax.numpy as jnp
import numpy as np
from jax import lax
from jax.experimental import pallas as pl
from jax.experimental.pallas import tpu as pltpu

F32 = jnp.float32
BF16 = jnp.bfloat16

D_MODEL = 1024
HEADS = 4
DH = 128
GROUP_W = HEADS * DH
N_GROUPS = 9
IN_WIDTH = N_GROUPS * GROUP_W
D_FF = 2816
GRID_W = 64
ROPE_THETA = 10000.0
EPS = 1e-6
LOG2E = 1.4426950408889634

CHUNK = 128
FF_TILE = 256
ROW_TILE = 512
VMEM_LIMIT = 56 * 1024 * 1024


def _dot(a, b):
    return jnp.dot(a, b, preferred_element_type=F32)


def _dot_nt(a, b):
    return lax.dot_general(a, b, (((1,), (1,)), ((), ())), preferred_element_type=F32)


def _dot_tn(a, b):
    return lax.dot_general(a, b, (((0,), (0,)), ((), ())), preferred_element_type=F32)


def _rms(x, gain):
    return x * lax.rsqrt(jnp.mean(x * x, axis=-1, keepdims=True) + EPS) * gain


def _cumsum_rows(x):
    rows = x.shape[0]
    row = lax.broadcasted_iota(jnp.int32, x.shape, 0)
    shift = 1
    while shift < rows:
        x = x + jnp.where(row >= shift, pltpu.roll(x, shift, axis=0), 0.0)
        shift *= 2
    return x


def _mod_kernel(c_ref, w_ref, b_ref, o_ref):
    c = c_ref[...]
    a = (c * jax.nn.sigmoid(c)).astype(BF16)
    o_ref[...] = _dot(a, w_ref[...].astype(BF16)) + b_ref[...]


def _modulation(c_rows, w_mod, b_mod):
    n_rows = c_rows.shape[0]
    width = w_mod.shape[1]
    tile = D_MODEL
    return pl.pallas_call(
        _mod_kernel,
        grid=(width // tile,),
        in_specs=[
            pl.BlockSpec((n_rows, D_MODEL), lambda j: (0, 0)),
            pl.BlockSpec((D_MODEL, tile), lambda j: (0, j)),
            pl.BlockSpec((1, tile), lambda j: (0, j)),
        ],
        out_specs=pl.BlockSpec((n_rows, tile), lambda j: (0, j)),
        out_shape=jax.ShapeDtypeStruct((n_rows, width), F32),
        compiler_params=pltpu.CompilerParams(dimension_semantics=("parallel",)),
        name="modulation",
    )(c_rows, w_mod, b_mod)


G_HQ, G_HV, G_ZF, G_ZB, G_HG, G_RQ, G_RK, G_RV, G_RG = range(N_GROUPS)
GROUP_ORDER = (G_RQ, G_RK, G_HQ, G_HV, G_ZF, G_ZB, G_HG, G_RV, G_RG)


def _rope(t, cos, sin_signed, first_quarter):
    swapped = jnp.where(first_quarter, pltpu.roll(t, DH - DH // 4, axis=1),
                        pltpu.roll(t, DH // 4, axis=1))
    return t * cos + swapped * sin_signed


def _modulated_norm(x, gain, scale, shift):
    return (_rms(x, gain) * (1.0 + scale) + shift).astype(BF16)


def _inproj_kernel(x_ref, sh_ref, sc_ref, g_ref, w_ref, cos_ref, sin_ref, o_ref):
    hb = _modulated_norm(x_ref[...], g_ref[...], sc_ref[0], sh_ref[0])
    cos = cos_ref[...]
    sin = sin_ref[...]
    lane = lax.broadcasted_iota(jnp.int32, cos.shape, 1)
    first_quarter = (lane % (DH // 2)) < (DH // 4)
    for g in GROUP_ORDER:
        cols = slice(g * GROUP_W, (g + 1) * GROUP_W)
        r = _dot(hb, w_ref[:, cols])
        if g == G_RK:
            r = r * (DH ** -0.5)
        if g in (G_RQ, G_RK):
            r = jnp.concatenate([_rope(r[:, h * DH:(h + 1) * DH], cos, sin, first_quarter)
                                 for h in range(HEADS)], axis=1)
        o_ref[:, cols] = r.astype(BF16)


def _in_projection(x2d, mod3, seq, norm_g, w_in_bf16, cos_t, sin_t):
    n_rows = x2d.shape[0]
    tiles_per_seq = seq // ROW_TILE

    def mod_spec(chunk):
        return pl.BlockSpec((1, 1, D_MODEL), lambda i: (i // tiles_per_seq, 0, chunk))

    table = pl.BlockSpec((ROW_TILE, DH), lambda i: (i % tiles_per_seq, 0))
    return pl.pallas_call(
        _inproj_kernel,
        grid=(n_rows // ROW_TILE,),
        in_specs=[
            pl.BlockSpec((ROW_TILE, D_MODEL), lambda i: (i, 0)),
            mod_spec(0),
            mod_spec(1),
            pl.BlockSpec((1, D_MODEL), lambda i: (0, 0)),
            pl.BlockSpec((D_MODEL, IN_WIDTH), lambda i: (0, 0)),
            table, table,
        ],
        out_specs=pl.BlockSpec((ROW_TILE, IN_WIDTH), lambda i: (i, 0)),
        out_shape=jax.ShapeDtypeStruct((n_rows, IN_WIDTH), BF16),
        compiler_params=pltpu.CompilerParams(
            dimension_semantics=("parallel",), vmem_limit_bytes=VMEM_LIMIT),
        name="in_projection",
    )(x2d, mod3, mod3, norm_g, w_in_bf16, cos_t, sin_t)


def _forget(z, lb):
    f = lb + (1.0 - lb) * jax.nn.sigmoid(z)
    return f, jnp.log(f)


def _ctx_kernel(lg_ref, c_ref, sh_ref, sc_ref, g_ref, w_ref, lb_ref,
                sf_ref, sb_ref, rf_ref, rb_ref):
    n = c_ref.shape[1]
    hb = _modulated_norm(c_ref[0], g_ref[...], sc_ref[0], sh_ref[0])

    def group(g):
        return _dot(hb, w_ref[:, g * GROUP_W:(g + 1) * GROUP_W])

    v = group(G_HV).astype(BF16)
    ff, lff = _forget(group(G_ZF), lb_ref[0:1, :])
    fb, lfb = _forget(group(G_ZB), lb_ref[1:2, :])
    bf = _cumsum_rows(lff)
    bb = _cumsum_rows(lfb)
    kf = ((1.0 - ff) * jnp.exp(bf[n - 1:n, :] - bf)).astype(BF16)
    kb = ((1.0 - fb) * jnp.exp(bb - lfb)).astype(BF16)
    rk = group(G_RK) * (DH ** -0.5)
    rv = group(G_RV).astype(BF16)
    pos = lax.broadcasted_iota(jnp.int32, (n, DH), 0).astype(F32)
    for h in range(HEADS):
        cols = slice(h * DH, (h + 1) * DH)
        sf_ref[0, h] = _dot_tn(v[:, cols], kf[:, cols])
        sb_ref[0, h] = _dot_tn(v[:, cols], kb[:, cols])
        wf = jnp.exp((n - 1.0 - pos) * lg_ref[0, h])
        wb = jnp.exp(pos * lg_ref[1, h])
        rf_ref[0, h] = _dot_tn(rv[:, cols], (rk[:, cols] * wf).astype(BF16))
        rb_ref[0, h] = _dot_tn(rv[:, cols], (rk[:, cols] * wb).astype(BF16))


def _context_states(ctx, mod_ctx, norm_g, w_in_bf16, lb, log_gamma):
    batch, n_ctx, _ = ctx.shape

    def mod_spec(chunk):
        return pl.BlockSpec((1, 1, D_MODEL), lambda b: (0, 0, chunk))

    state = pl.BlockSpec((1, HEADS, DH, DH), lambda b: (b, 0, 0, 0))
    state_shape = jax.ShapeDtypeStruct((batch, HEADS, DH, DH), F32)
    return pl.pallas_call(
        _ctx_kernel,
        grid=(batch,),
        in_specs=[
            pl.BlockSpec(memory_space=pltpu.SMEM),
            pl.BlockSpec((1, n_ctx, D_MODEL), lambda b: (b, 0, 0)),
            mod_spec(0), mod_spec(1),
            pl.BlockSpec((1, D_MODEL), lambda b: (0, 0)),
            pl.BlockSpec((D_MODEL, IN_WIDTH), lambda b: (0, 0), pipeline_mode=pl.Buffered(1)),
            pl.BlockSpec((2, GROUP_W), lambda b: (0, 0)),
        ],
        out_specs=[state, state, state, state],
        out_shape=[state_shape] * 4,
        compiler_params=pltpu.CompilerParams(
            dimension_semantics=("parallel",), vmem_limit_bytes=VMEM_LIMIT),
        name="context_states",
    )(log_gamma, ctx, mod_ctx, mod_ctx, norm_g, w_in_bf16, lb)


def _anchor(phi, half, forward):
    rows = phi.shape[0]
    pick = half - 1 if forward else half
    block = 2 * half
    p3 = phi.reshape(rows // 8, 8, DH)
    sub = lax.broadcasted_iota(jnp.int32, p3.shape, 1)
    out = None
    for start in range(0, 8, block):
        a = jnp.broadcast_to(p3[:, start + pick:start + pick + 1, :], p3.shape)
        out = a if out is None else jnp.where(sub >= start, a, out)
    return out.reshape(rows, DH)


def _level_operands(q, kf, kb, ff, fb, phif, phib, half, row):
    n = q.shape[0]
    if half == 1:
        odd = (row & 1) != 0
        zz = q * jnp.where(odd, ff, fb)
        ww = jnp.where(odd, kb, kf)
    elif half >= 8:
        zs, ws = [], []
        for b0 in range(0, n, 2 * half):
            first = slice(b0, b0 + half)
            second = slice(b0 + half, b0 + 2 * half)
            af = phif[b0 + half - 1:b0 + half, :]
            ab = phib[b0 + half:b0 + half + 1, :]
            zs += [q[first] * jnp.exp2(phib[first] - ab), q[second] * jnp.exp2(phif[second] - af)]
            ws += [kf[first] * jnp.exp2(af - phif[first]), kb[second] * jnp.exp2(ab - phib[second])]
        zz = jnp.concatenate(zs, axis=0)
        ww = jnp.concatenate(ws, axis=0)
    else:
        af = _anchor(phif, half, True)
        ab = _anchor(phib, half, False)
        second = (row & half) != 0
        zz = q * jnp.exp2(jnp.where(second, phif - af, phib - ab))
        ww = jnp.where(second, kb, kf) * jnp.exp2(jnp.where(second, ab - phib, af - phif))
    return zz.astype(BF16), ww.astype(BF16)


def _mixer_kernel(lg_ref, q_ref, v_ref, zf_ref, zb_ref, og_ref, rq_ref, rk_ref, rv_ref, rg_ref,
                  lb_ref, hgn_ref, rn_ref,
                  sf0_ref, sb0_ref, rf0_ref, rb0_ref,
                  hgo_ref, reto_ref,
                  oh_s, or_s, qcat_s, rqcat_s, sh_s, sr_s, shb_s, srb_s, dec_s):
    head = pl.program_id(1)
    seq = q_ref.shape[1]
    n_chunks = seq // CHUNK
    lgf = lg_ref[0, head]
    lgb = lg_ref[1, head]
    lbf = lb_ref[0:1, :]
    lbb = lb_ref[1:2, :]
    fwd = slice(0, DH)
    bwd = slice(DH, 2 * DH)

    row = lax.broadcasted_iota(jnp.int32, (CHUNK, DH), 0)
    t_idx = lax.broadcasted_iota(jnp.int32, (CHUNK, CHUNK), 0)
    s_idx = lax.broadcasted_iota(jnp.int32, (CHUNK, CHUNK), 1)
    split = t_idx ^ s_idx
    rel = (t_idx - s_idx).astype(F32)
    ret_decay = jnp.where(rel >= 0, jnp.exp(rel * lgf), 0.0) + jnp.where(rel <= 0, jnp.exp(-rel * lgb), 0.0)
    pos = row.astype(F32)
    qdec_f = jnp.exp((pos + 1.0) * lgf)
    kdec_f = jnp.exp((CHUNK - 1.0 - pos) * lgf)
    qdec_b = jnp.exp((CHUNK - pos) * lgb)
    kdec_b = jnp.exp(pos * lgb)

    def pass_a(i, carry):
        rows = pl.ds(pl.multiple_of(i * CHUNK, CHUNK), CHUNK)
        q = q_ref[0, rows, :].astype(F32)
        v = v_ref[0, rows, :]
        ff, lff = _forget(zf_ref[0, rows, :].astype(F32), lbf)
        fb, lfb = _forget(zb_ref[0, rows, :].astype(F32), lbb)
        lff = lff * LOG2E
        lfb = lfb * LOG2E
        kf = 1.0 - ff
        kb = 1.0 - fb
        cf = _cumsum_rows(lff)
        cb = _cumsum_rows(lfb)
        phif = cf
        phib = lfb - cb

        attn = None
        half = CHUNK // 2
        while half >= 1:
            zz, ww = _level_operands(q, kf, kb, ff, fb, phif, phib, half, row)
            a = _dot_nt(zz, ww)
            attn = a if attn is None else jnp.where(split < 2 * half, a, attn)
            half //= 2
        attn = jnp.where(split == 0, 0.0, attn)
        diag = jnp.sum(q * (kf + kb), axis=-1, keepdims=True)
        oh_s[rows, :] = _dot(attn.astype(BF16), v) + diag * v.astype(F32)

        last_f = cf[CHUNK - 1:CHUNK, :]
        last_b = cb[CHUNK - 1:CHUNK, :]
        qcat_s[rows, fwd] = (q * jnp.exp2(cf)).astype(BF16)
        qcat_s[rows, bwd] = (q * jnp.exp2(last_b + phib)).astype(BF16)
        kcat = jnp.concatenate([(kf * jnp.exp2(last_f - cf)).astype(BF16),
                                (kb * jnp.exp2(cb - lfb)).astype(BF16)], axis=1)
        sh_s[i] = _dot_tn(v, kcat)
        dec_s[i, :, fwd] = jnp.exp2(last_f)
        dec_s[i, :, bwd] = jnp.exp2(last_b)

        rqb = rq_ref[0, rows, :]
        rkb = rk_ref[0, rows, :]
        rq = rqb.astype(F32)
        rk = rkb.astype(F32)
        rv = rv_ref[0, rows, :]
        scores = _dot_nt(rqb, rkb) * ret_decay
        or_s[rows, :] = _dot(scores.astype(BF16), rv)
        rqcat_s[rows, fwd] = (rq * qdec_f).astype(BF16)
        rqcat_s[rows, bwd] = (rq * qdec_b).astype(BF16)
        rkcat = jnp.concatenate([(rk * kdec_f).astype(BF16), (rk * kdec_b).astype(BF16)], axis=1)
        sr_s[i] = _dot_tn(rv, rkcat)
        return carry

    lax.fori_loop(0, n_chunks, pass_a, 0, unroll=4)

    cdec_f = jnp.exp(CHUNK * lgf)
    cdec_b = jnp.exp(CHUNK * lgb)
    sf, rf = sf0_ref[0, 0], rf0_ref[0, 0]
    for i in range(n_chunks):
        shb_s[i, :, fwd] = sf.astype(BF16)
        srb_s[i, :, fwd] = rf.astype(BF16)
        sf = sf * dec_s[i, :, fwd] + sh_s[i, :, fwd]
        rf = rf * cdec_f + sr_s[i, :, fwd]
    sb, rb = sb0_ref[0, 0], rb0_ref[0, 0]
    for i in reversed(range(n_chunks)):
        shb_s[i, :, bwd] = sb.astype(BF16)
        srb_s[i, :, bwd] = rb.astype(BF16)
        sb = sb * dec_s[i, :, bwd] + sh_s[i, :, bwd]
        rb = rb * cdec_b + sr_s[i, :, bwd]

    def pass_c(i, carry):
        rows = pl.ds(pl.multiple_of(i * CHUNK, CHUNK), CHUNK)
        o = oh_s[rows, :] + _dot_nt(qcat_s[rows, :], shb_s[i])
        hg = o * jax.nn.sigmoid(og_ref[0, rows, :].astype(F32))
        hgo_ref[0, rows, :] = _rms(hg, hgn_ref[...]).astype(BF16)
        y = or_s[rows, :] + _dot_nt(rqcat_s[rows, :], srb_s[i])
        rg = rg_ref[0, rows, :].astype(F32)
        ret = _rms(y, rn_ref[...]) * (rg * jax.nn.sigmoid(rg))
        reto_ref[0, rows, :] = ret.astype(BF16)
        return carry

    lax.fori_loop(0, n_chunks, pass_c, 0, unroll=8)


def _mixer(p, lb, log_gamma, hg_norm, ret_norm, states):
    batch, seq, _ = p.shape
    n_chunks = seq // CHUNK

    def group(g):
        return pl.BlockSpec((1, seq, DH), lambda b, h: (b, 0, g * HEADS + h))

    per_head_row = pl.BlockSpec((1, DH), lambda b, h: (0, h))
    state = pl.BlockSpec((1, 1, DH, DH), lambda b, h: (b, h, 0, 0))
    out_spec = pl.BlockSpec((1, seq, DH), lambda b, h: (b, 0, h))
    out_shape = jax.ShapeDtypeStruct((batch, seq, GROUP_W), BF16)
    return pl.pallas_call(
        _mixer_kernel,
        grid=(batch, HEADS),
        in_specs=[pl.BlockSpec(memory_space=pltpu.SMEM)]
        + [group(g) for g in range(N_GROUPS)]
        + [pl.BlockSpec((2, DH), lambda b, h: (0, h)), per_head_row, per_head_row,
           state, state, state, state],
        out_specs=[out_spec, out_spec],
        out_shape=[out_shape, out_shape],
        scratch_shapes=[
            pltpu.VMEM((seq, DH), F32), pltpu.VMEM((seq, DH), F32),
            pltpu.VMEM((seq, 2 * DH), BF16), pltpu.VMEM((seq, 2 * DH), BF16),
            pltpu.VMEM((n_chunks, DH, 2 * DH), F32),
            pltpu.VMEM((n_chunks, DH, 2 * DH), F32),
            pltpu.VMEM((n_chunks, DH, 2 * DH), BF16),
            pltpu.VMEM((n_chunks, DH, 2 * DH), BF16),
            pltpu.VMEM((n_chunks, 1, 2 * DH), F32),
        ],
        compiler_params=pltpu.CompilerParams(
            dimension_semantics=("parallel", "parallel"), vmem_limit_bytes=VMEM_LIMIT),
        name="token_mixer",
    )(log_gamma, p, p, p, p, p, p, p, p, p, lb, hg_norm, ret_norm, *states)


FFN_ROWS = 1024
HALO = 16


def _ffn_kernel(x_ref, xp_ref, xn_ref, hg_ref, hgp_ref, hgn_ref, rt_ref, rtp_ref, rtn_ref,
                g1_ref, sh2_ref, sc2_ref, g2_ref, woa_ref, wob_ref, n2_ref, fin_ref,
                wup_ref, cw_ref, cb_ref, wd_ref,
                o_ref, h2_s, gate2_s, gate3_s, up2_s, up3_s, acc_s):
    m = pl.program_id(1)
    rows = x_ref.shape[1]
    n_sub = rows // ROW_TILE
    n_ff = D_FF // FF_TILE

    def ext(k):
        lo = 0 if k == 0 else HALO + k * ROW_TILE
        hi = HALO + (k + 1) * ROW_TILE + (HALO if k == n_sub - 1 else 0)
        return lo, hi

    def piece(main_ref, prev_ref, next_ref, k):
        parts = ([prev_ref[0]] if k == 0 else []) + [main_ref[0, k * ROW_TILE:(k + 1) * ROW_TILE, :]] \
            + ([next_ref[0]] if k == n_sub - 1 else [])
        return jnp.concatenate(parts, axis=0) if len(parts) > 1 else parts[0]

    for k in range(n_sub):
        lo, hi = ext(k)
        skip = HALO if k == 0 else 0
        proj = (_dot(piece(hg_ref, hgp_ref, hgn_ref, k), woa_ref[...])
                + _dot(piece(rt_ref, rtp_ref, rtn_ref, k), wob_ref[...]))
        x1 = piece(x_ref, xp_ref, xn_ref, k) + g1_ref[0] * proj
        o_ref[0, k * ROW_TILE:(k + 1) * ROW_TILE, :] = x1[skip:skip + ROW_TILE]
        h2 = _rms(x1, n2_ref[...]) * (1.0 + sc2_ref[0]) + sh2_ref[0]
        h2_s[lo:hi, :] = h2.astype(BF16)

    prev_valid = (m > 0).astype(F32)
    next_valid = (m < pl.num_programs(1) - 1).astype(F32)
    edge_lo = slice(HALO - 8, HALO)
    edge_hi = slice(HALO + rows, HALO + rows + 8)

    gate_bufs = (gate2_s, gate3_s)
    up_bufs = (up2_s, up3_s)

    def gate_up(j):
        gate_s, up_s = gate_bufs[j % 2], up_bufs[j % 2]
        w_gu = jnp.concatenate([wup_ref[:, j * FF_TILE:(j + 1) * FF_TILE],
                                wup_ref[:, D_FF + j * FF_TILE:D_FF + (j + 1) * FF_TILE]], axis=1)
        for k in range(n_sub):
            lo, hi = ext(k)
            skip = HALO if k == 0 else 0
            gu = _dot(h2_s[lo:hi, :], w_gu)
            gate_s[lo:hi, :] = gu[:, :FF_TILE]
            up_s[k * ROW_TILE:(k + 1) * ROW_TILE, :] = gu[skip:skip + ROW_TILE, FF_TILE:]
        gate_s[edge_lo, :] = gate_s[edge_lo, :] * prev_valid
        gate_s[edge_hi, :] = gate_s[edge_hi, :] * next_valid

    gate_up(0)
    for j in range(n_ff):
        if j + 1 < n_ff:
            gate_up(j + 1)
        cols = slice(j * FF_TILE, (j + 1) * FF_TILE)
        gate_s, up_s = gate_bufs[j % 2], up_bufs[j % 2]
        for k in range(n_sub):
            sub = slice(k * ROW_TILE, (k + 1) * ROW_TILE)
            base = HALO + k * ROW_TILE
            conv = (gate_s[base - 1:base - 1 + ROW_TILE, :] * cw_ref[0:1, cols]
                    + gate_s[base:base + ROW_TILE, :] * cw_ref[1:2, cols]
                    + gate_s[base + 1:base + 1 + ROW_TILE, :] * cw_ref[2:3, cols]
                    + cb_ref[:, cols])
            act = (conv * jax.nn.sigmoid(conv) * up_s[sub, :]).astype(BF16)
            ffn = _dot(act, wd_ref[cols, :])
            if j > 0:
                ffn += acc_s[sub, :]
            if j < n_ff - 1:
                acc_s[sub, :] = ffn
            else:
                x2 = o_ref[0, sub, :] + g2_ref[0] * ffn
                o_ref[0, sub, :] = _rms(x2, fin_ref[...])


def _out_ffn(x, hgo, reto, mod3, w_out_bf16, norm2_g, final_g, w_up_bf16, conv_w, conv_b, w_down_bf16):
    batch, seq, _ = x.shape
    rows = min(FFN_ROWS, seq)
    n_row_tiles = seq // rows
    halo_per_tile = rows // HALO
    n_halo_blocks = seq // HALO
    once = pl.Buffered(1)

    def main(width):
        return pl.BlockSpec((1, rows, width), lambda b, m: (b, m, 0))

    def prev(width):
        return pl.BlockSpec((1, HALO, width), lambda b, m: (b, jnp.maximum(m * halo_per_tile - 1, 0), 0))

    def nxt(width):
        return pl.BlockSpec((1, HALO, width),
                            lambda b, m: (b, jnp.minimum((m + 1) * halo_per_tile, n_halo_blocks - 1), 0))

    def mod_spec(chunk):
        return pl.BlockSpec((1, 1, D_MODEL), lambda b, m: (b, 0, chunk))

    def const(shape, index=(0, 0)):
        return pl.BlockSpec(shape, lambda b, m: index, pipeline_mode=once)

    return pl.pallas_call(
        _ffn_kernel,
        grid=(batch, n_row_tiles),
        in_specs=[
            main(D_MODEL), prev(D_MODEL), nxt(D_MODEL),
            main(GROUP_W), prev(GROUP_W), nxt(GROUP_W),
            main(GROUP_W), prev(GROUP_W), nxt(GROUP_W),
            mod_spec(2), mod_spec(3), mod_spec(4), mod_spec(5),
            const((GROUP_W, D_MODEL), (0, 0)),
            const((GROUP_W, D_MODEL), (1, 0)),
            const((1, D_MODEL)), const((1, D_MODEL)),
            const((D_MODEL, 2 * D_FF)),
            const((3, D_FF)), const((1, D_FF)),
            const((D_FF, D_MODEL)),
        ],
        out_specs=pl.BlockSpec((1, rows, D_MODEL), lambda b, m: (b, m, 0)),
        out_shape=jax.ShapeDtypeStruct((batch, seq, D_MODEL), F32),
        scratch_shapes=[
            pltpu.VMEM((rows + 2 * HALO, D_MODEL), BF16),
            pltpu.VMEM((rows + 2 * HALO, FF_TILE), F32),
            pltpu.VMEM((rows + 2 * HALO, FF_TILE), F32),
            pltpu.VMEM((rows, FF_TILE), F32),
            pltpu.VMEM((rows, FF_TILE), F32),
            pltpu.VMEM((rows, D_MODEL), F32),
        ],
        compiler_params=pltpu.CompilerParams(
            dimension_semantics=("parallel", "parallel"), vmem_limit_bytes=VMEM_LIMIT),
        name="out_proj_ffn",
    )(x, x, x, hgo, hgo, hgo, reto, reto, reto, mod3, mod3, mod3, mod3,
      w_out_bf16, w_out_bf16, norm2_g, final_g, w_up_bf16, conv_w, conv_b, w_down_bf16)


def _rope_tables(seq):
    quarter = DH // 4
    freqs = np.float32(ROPE_THETA) ** (-np.arange(quarter, dtype=np.float32) / np.float32(quarter))
    t = np.arange(seq)
    ang_r = (t // GRID_W).astype(np.float32)[:, None] * freqs[None, :]
    ang_c = (t % GRID_W).astype(np.float32)[:, None] * freqs[None, :]
    cos = np.concatenate([np.cos(ang_r)] * 2 + [np.cos(ang_c)] * 2, axis=-1)
    sin = np.concatenate([-np.sin(ang_r), np.sin(ang_r), -np.sin(ang_c), np.sin(ang_c)], axis=-1)
    return jnp.asarray(cos, F32), jnp.asarray(sin, F32)


def kernel(x, c, ctx, c_ctx, w_mod, b_mod, norm1_g, w_in, hgrn_lb, hgrn_norm_g, ret_decay,
           ret_norm_g, w_out, norm2_g, w_up, conv_w, conv_b, w_down, final_g):
    batch, seq, _ = x.shape
    n_ctx = ctx.shape[1]
    assert w_mod.shape[0] == 1, "single-layer block"
    assert seq % CHUNK == 0 and seq % ROW_TILE == 0 and (n_ctx & (n_ctx - 1)) == 0

    lb = jnp.cumsum(jax.nn.softmax(hgrn_lb.astype(F32), axis=1), axis=1)[:, 0]
    log_gamma = jax.nn.log_sigmoid(ret_decay[0].astype(F32))
    cos_t, sin_t = _rope_tables(seq)

    n_mod_rows = -(-(batch + 1) // 8) * 8
    c_rows = jnp.zeros((n_mod_rows, D_MODEL), F32).at[:batch].set(c).at[batch].set(c_ctx)
    mod = _modulation(c_rows, w_mod[0], b_mod[0][None, :])
    mod3 = mod.reshape(n_mod_rows, 1, 6 * D_MODEL)

    w_in_b = w_in[0].astype(BF16)
    norm1 = norm1_g[0][None, :]
    states = _context_states(ctx, mod3[batch:batch + 1], norm1, w_in_b, lb, log_gamma)
    p = _in_projection(x.reshape(batch * seq, D_MODEL), mod3[:batch], seq, norm1, w_in_b, cos_t, sin_t)
    hgo, reto = _mixer(p.reshape(batch, seq, IN_WIDTH), lb, log_gamma,
                       hgrn_norm_g[0][None, :], ret_norm_g[0][None, :], states)

    return _out_ffn(x, hgo, reto, mod3[:batch], w_out[0].astype(BF16), norm2_g[0][None, :],
                    final_g[None, :], w_up[0].astype(BF16), conv_w[0], conv_b[0][None, :],
                    w_down[0].astype(BF16))
```

```python
import jax
import jax.numpy as jnp
import numpy as np
from jax import lax
from jax.experimental import pallas as pl
from jax.experimental.pallas import tpu as pltpu

F32 = jnp.float32
BF16 = jnp.bfloat16

D_MODEL = 1024
HEADS = 4
DH = 128
GROUP_W = HEADS * DH
N_GROUPS = 9
IN_WIDTH = N_GROUPS * GROUP_W
D_FF = 2816
GRID_W = 64
ROPE_THETA = 10000.0
EPS = 1e-6
LOG2E = 1.4426950408889634

CHUNK = 128
FF_TILE = 256
ROW_TILE = 512
VMEM_LIMIT = 56 * 1024 * 1024


def _dot(a, b):
    return jnp.dot(a, b, preferred_element_type=F32)


def _dot_nt(a, b):
    return lax.dot_general(a, b, (((1,), (1,)), ((), ())), preferred_element_type=F32)


def _dot_tn(a, b):
    return lax.dot_general(a, b, (((0,), (0,)), ((), ())), preferred_element_type=F32)


def _rms(x, gain):
    return x * lax.rsqrt(jnp.mean(x * x, axis=-1, keepdims=True) + EPS) * gain


def _cumsum_rows(x):
    rows = x.shape[0]
    row = lax.broadcasted_iota(jnp.int32, x.shape, 0)
    shift = 1
    while shift < rows:
        x = x + jnp.where(row >= shift, pltpu.roll(x, shift, axis=0), 0.0)
        shift *= 2
    return x


def _mod_kernel(c_ref, w_ref, b_ref, o_ref):
    c = c_ref[...]
    a = (c * jax.nn.sigmoid(c)).astype(BF16)
    o_ref[...] = _dot(a, w_ref[...].astype(BF16)) + b_ref[...]


def _modulation(c_rows, w_mod, b_mod):
    n_rows = c_rows.shape[0]
    width = w_mod.shape[1]
    tile = D_MODEL
    return pl.pallas_call(
        _mod_kernel,
        grid=(width // tile,),
        in_specs=[
            pl.BlockSpec((n_rows, D_MODEL), lambda j: (0, 0)),
            pl.BlockSpec((D_MODEL, tile), lambda j: (0, j)),
            pl.BlockSpec((1, tile), lambda j: (0, j)),
        ],
        out_specs=pl.BlockSpec((n_rows, tile), lambda j: (0, j)),
        out_shape=jax.ShapeDtypeStruct((n_rows, width), F32),
        compiler_params=pltpu.CompilerParams(dimension_semantics=("parallel",)),
        name="modulation",
    )(c_rows, w_mod, b_mod)


G_HQ, G_HV, G_ZF, G_ZB, G_HG, G_RQ, G_RK, G_RV, G_RG = range(N_GROUPS)
GROUP_ORDER = (G_RQ, G_RK, G_HQ, G_HV, G_ZF, G_ZB, G_HG, G_RV, G_RG)


def _rope(t, cos, sin_signed, first_quarter):
    swapped = jnp.where(first_quarter, pltpu.roll(t, DH - DH // 4, axis=1),
                        pltpu.roll(t, DH // 4, axis=1))
    return t * cos + swapped * sin_signed


def _modulated_norm(x, gain, scale, shift):
    return (_rms(x, gain) * (1.0 + scale) + shift).astype(BF16)


def _inproj_kernel(x_ref, sh_ref, sc_ref, g_ref, w_ref, cos_ref, sin_ref, o_ref):
    hb = _modulated_norm(x_ref[...], g_ref[...], sc_ref[0], sh_ref[0])
    cos = cos_ref[...]
    sin = sin_ref[...]
    lane = lax.broadcasted_iota(jnp.int32, cos.shape, 1)
    first_quarter = (lane % (DH // 2)) < (DH // 4)
    for g in GROUP_ORDER:
        cols = slice(g * GROUP_W, (g + 1) * GROUP_W)
        r = _dot(hb, w_ref[:, cols])
        if g == G_RK:
            r = r * (DH ** -0.5)
        for h in range(HEADS):
            t = r[:, h * DH:(h + 1) * DH]
            if g in (G_RQ, G_RK):
                t = _rope(t, cos, sin, first_quarter)
            o_ref[0, g * HEADS + h] = t.astype(BF16)


def _in_projection(x2d, mod3, seq, norm_g, w_in_bf16, cos_t, sin_t):
    n_rows = x2d.shape[0]
    tiles_per_seq = seq // ROW_TILE

    def mod_spec(chunk):
        return pl.BlockSpec((1, 1, D_MODEL), lambda i: (i // tiles_per_seq, 0, chunk))

    table = pl.BlockSpec((ROW_TILE, DH), lambda i: (i % tiles_per_seq, 0))
    return pl.pallas_call(
        _inproj_kernel,
        grid=(n_rows // ROW_TILE,),
        in_specs=[
            pl.BlockSpec((ROW_TILE, D_MODEL), lambda i: (i, 0)),
            mod_spec(0),
            mod_spec(1),
            pl.BlockSpec((1, D_MODEL), lambda i: (0, 0)),
            pl.BlockSpec((D_MODEL, IN_WIDTH), lambda i: (0, 0)),
            table, table,
        ],
        out_specs=pl.BlockSpec((1, N_GROUPS * HEADS, ROW_TILE, DH),
                               lambda i: (i // tiles_per_seq, 0, i % tiles_per_seq, 0)),
        out_shape=jax.ShapeDtypeStruct((n_rows // seq, N_GROUPS * HEADS, seq, DH), BF16),
        compiler_params=pltpu.CompilerParams(
            dimension_semantics=("parallel",), vmem_limit_bytes=VMEM_LIMIT),
        name="in_projection",
    )(x2d, mod3, mod3, norm_g, w_in_bf16, cos_t, sin_t)


def _forget(z, lb):
    f = lb + (1.0 - lb) * jax.nn.sigmoid(z)
    return f, jnp.log(f)


def _ctx_kernel(lg_ref, c_ref, sh_ref, sc_ref, g_ref, w_ref, lb_ref,
                sf_ref, sb_ref, rf_ref, rb_ref):
    n = c_ref.shape[1]
    hb = _modulated_norm(c_ref[0], g_ref[...], sc_ref[0], sh_ref[0])

    def group(g):
        return _dot(hb, w_ref[:, g * GROUP_W:(g + 1) * GROUP_W])

    v = group(G_HV).astype(BF16)
    ff, lff = _forget(group(G_ZF), lb_ref[0:1, :])
    fb, lfb = _forget(group(G_ZB), lb_ref[1:2, :])
    bf = _cumsum_rows(lff)
    bb = _cumsum_rows(lfb)
    kf = ((1.0 - ff) * jnp.exp(bf[n - 1:n, :] - bf)).astype(BF16)
    kb = ((1.0 - fb) * jnp.exp(bb - lfb)).astype(BF16)
    rk = group(G_RK) * (DH ** -0.5)
    rv = group(G_RV).astype(BF16)
    pos = lax.broadcasted_iota(jnp.int32, (n, DH), 0).astype(F32)
    for h in range(HEADS):
        cols = slice(h * DH, (h + 1) * DH)
        sf_ref[0, h] = _dot_tn(v[:, cols], kf[:, cols])
        sb_ref[0, h] = _dot_tn(v[:, cols], kb[:, cols])
        wf = jnp.exp((n - 1.0 - pos) * lg_ref[0, h])
        wb = jnp.exp(pos * lg_ref[1, h])
        rf_ref[0, h] = _dot_tn(rv[:, cols], (rk[:, cols] * wf).astype(BF16))
        rb_ref[0, h] = _dot_tn(rv[:, cols], (rk[:, cols] * wb).astype(BF16))


def _context_states(ctx, mod_ctx, norm_g, w_in_bf16, lb, log_gamma):
    batch, n_ctx, _ = ctx.shape

    def mod_spec(chunk):
        return pl.BlockSpec((1, 1, D_MODEL), lambda b: (0, 0, chunk))

    state = pl.BlockSpec((1, HEADS, DH, DH), lambda b: (b, 0, 0, 0))
    state_shape = jax.ShapeDtypeStruct((batch, HEADS, DH, DH), F32)
    return pl.pallas_call(
        _ctx_kernel,
        grid=(batch,),
        in_specs=[
            pl.BlockSpec(memory_space=pltpu.SMEM),
            pl.BlockSpec((1, n_ctx, D_MODEL), lambda b: (b, 0, 0)),
            mod_spec(0), mod_spec(1),
            pl.BlockSpec((1, D_MODEL), lambda b: (0, 0)),
            pl.BlockSpec((D_MODEL, IN_WIDTH), lambda b: (0, 0), pipeline_mode=pl.Buffered(1)),
            pl.BlockSpec((2, GROUP_W), lambda b: (0, 0)),
        ],
        out_specs=[state, state, state, state],
        out_shape=[state_shape] * 4,
        compiler_params=pltpu.CompilerParams(
            dimension_semantics=("parallel",), vmem_limit_bytes=VMEM_LIMIT),
        name="context_states",
    )(log_gamma, ctx, mod_ctx, mod_ctx, norm_g, w_in_bf16, lb)


def _anchor(phi, half, forward):
    rows = phi.shape[0]
    pick = half - 1 if forward else half
    block = 2 * half
    p3 = phi.reshape(rows // 8, 8, DH)
    sub = lax.broadcasted_iota(jnp.int32, p3.shape, 1)
    out = None
    for start in range(0, 8, block):
        a = jnp.broadcast_to(p3[:, start + pick:start + pick + 1, :], p3.shape)
        out = a if out is None else jnp.where(sub >= start, a, out)
    return out.reshape(rows, DH)


def _level_operands(q, kf, kb, ff, fb, phif, phib, half, row):
    n = q.shape[0]
    if half == 1:
        odd = (row & 1) != 0
        zz = q * jnp.where(odd, ff, fb)
        ww = jnp.where(odd, kb, kf)
    elif half >= 8:
        zs, ws = [], []
        for b0 in range(0, n, 2 * half):
            first = slice(b0, b0 + half)
            second = slice(b0 + half, b0 + 2 * half)
            af = phif[b0 + half - 1:b0 + half, :]
            ab = phib[b0 + half:b0 + half + 1, :]
            zs += [q[first] * jnp.exp2(phib[first] - ab), q[second] * jnp.exp2(phif[second] - af)]
            ws += [kf[first] * jnp.exp2(af - phif[first]), kb[second] * jnp.exp2(ab - phib[second])]
        zz = jnp.concatenate(zs, axis=0)
        ww = jnp.concatenate(ws, axis=0)
    else:
        af = _anchor(phif, half, True)
        ab = _anchor(phib, half, False)
        second = (row & half) != 0
        zz = q * jnp.exp2(jnp.where(second, phif - af, phib - ab))
        ww = jnp.where(second, kb, kf) * jnp.exp2(jnp.where(second, ab - phib, af - phif))
    return zz.astype(BF16), ww.astype(BF16)


def _mixer_kernel(lg_ref, q_ref, v_ref, zf_ref, zb_ref, og_ref, rq_ref, rk_ref, rv_ref, rg_ref,
                  lb_ref, hgn_ref, rn_ref,
                  sf0_ref, sb0_ref, rf0_ref, rb0_ref,
                  hgo_ref, reto_ref,
                  oh_s, or_s, qcat_s, rqcat_s, sh_s, sr_s, shb_s, srb_s, dec_s):
    head = pl.program_id(1)
    seq = q_ref.shape[2]
    n_chunks = seq // CHUNK
    lgf = lg_ref[0, head]
    lgb = lg_ref[1, head]
    lbf = lb_ref[0:1, :]
    lbb = lb_ref[1:2, :]
    fwd = slice(0, DH)
    bwd = slice(DH, 2 * DH)

    row = lax.broadcasted_iota(jnp.int32, (CHUNK, DH), 0)
    t_idx = lax.broadcasted_iota(jnp.int32, (CHUNK, CHUNK), 0)
    s_idx = lax.broadcasted_iota(jnp.int32, (CHUNK, CHUNK), 1)
    split = t_idx ^ s_idx
    rel = (t_idx - s_idx).astype(F32)
    ret_decay = jnp.where(rel >= 0, jnp.exp(rel * lgf), 0.0) + jnp.where(rel <= 0, jnp.exp(-rel * lgb), 0.0)
    pos = row.astype(F32)
    qdec_f = jnp.exp((pos + 1.0) * lgf)
    kdec_f = jnp.exp((CHUNK - 1.0 - pos) * lgf)
    qdec_b = jnp.exp((CHUNK - pos) * lgb)
    kdec_b = jnp.exp(pos * lgb)

    def pass_a(i, carry):
        rows = pl.ds(pl.multiple_of(i * CHUNK, CHUNK), CHUNK)
        q = q_ref[0, 0, rows, :].astype(F32)
        v = v_ref[0, 0, rows, :]
        ff, lff = _forget(zf_ref[0, 0, rows, :].astype(F32), lbf)
        fb, lfb = _forget(zb_ref[0, 0, rows, :].astype(F32), lbb)
        lff = lff * LOG2E
        lfb = lfb * LOG2E
        kf = 1.0 - ff
        kb = 1.0 - fb
        cf = _cumsum_rows(lff)
        cb = _cumsum_rows(lfb)
        phif = cf
        phib = lfb - cb

        attn = None
        half = CHUNK // 2
        while half >= 1:
            zz, ww = _level_operands(q, kf, kb, ff, fb, phif, phib, half, row)
            a = _dot_nt(zz, ww)
            attn = a if attn is None else jnp.where(split < 2 * half, a, attn)
            half //= 2
        attn = jnp.where(split == 0, 0.0, attn)
        diag = jnp.sum(q * (kf + kb), axis=-1, keepdims=True)
        oh_s[rows, :] = _dot(attn.astype(BF16), v) + diag * v.astype(F32)

        last_f = cf[CHUNK - 1:CHUNK, :]
        last_b = cb[CHUNK - 1:CHUNK, :]
        qcat_s[rows, fwd] = (q * jnp.exp2(cf)).astype(BF16)
        qcat_s[rows, bwd] = (q * jnp.exp2(last_b + phib)).astype(BF16)
        kcat = jnp.concatenate([(kf * jnp.exp2(last_f - cf)).astype(BF16),
                                (kb * jnp.exp2(cb - lfb)).astype(BF16)], axis=1)
        sh_s[i] = _dot_tn(v, kcat)
        dec_s[i, :, fwd] = jnp.exp2(last_f)
        dec_s[i, :, bwd] = jnp.exp2(last_b)

        rqb = rq_ref[0, 0, rows, :]
        rkb = rk_ref[0, 0, rows, :]
        rq = rqb.astype(F32)
        rk = rkb.astype(F32)
        rv = rv_ref[0, 0, rows, :]
        scores = _dot_nt(rqb, rkb) * ret_decay
        or_s[rows, :] = _dot(scores.astype(BF16), rv)
        rqcat_s[rows, fwd] = (rq * qdec_f).astype(BF16)
        rqcat_s[rows, bwd] = (rq * qdec_b).astype(BF16)
        rkcat = jnp.concatenate([(rk * kdec_f).astype(BF16), (rk * kdec_b).astype(BF16)], axis=1)
        sr_s[i] = _dot_tn(rv, rkcat)
        return carry

    lax.fori_loop(0, n_chunks, pass_a, 0, unroll=4)

    cdec_f = jnp.exp(CHUNK * lgf)
    cdec_b = jnp.exp(CHUNK * lgb)
    sf, rf = sf0_ref[0, 0], rf0_ref[0, 0]
    for i in range(n_chunks):
        shb_s[i, :, fwd] = sf.astype(BF16)
        srb_s[i, :, fwd] = rf.astype(BF16)
        sf = sf * dec_s[i, :, fwd] + sh_s[i, :, fwd]
        rf = rf * cdec_f + sr_s[i, :, fwd]
    sb, rb = sb0_ref[0, 0], rb0_ref[0, 0]
    for i in reversed(range(n_chunks)):
        shb_s[i, :, bwd] = sb.astype(BF16)
        srb_s[i, :, bwd] = rb.astype(BF16)
        sb = sb * dec_s[i, :, bwd] + sh_s[i, :, bwd]
        rb = rb * cdec_b + sr_s[i, :, bwd]

    def pass_c(i, carry):
        rows = pl.ds(pl.multiple_of(i * CHUNK, CHUNK), CHUNK)
        o = oh_s[rows, :] + _dot_nt(qcat_s[rows, :], shb_s[i])
        hg = o * jax.nn.sigmoid(og_ref[0, 0, rows, :].astype(F32))
        hgo_ref[0, 0, rows, :] = _rms(hg, hgn_ref[...]).astype(BF16)
        y = or_s[rows, :] + _dot_nt(rqcat_s[rows, :], srb_s[i])
        rg = rg_ref[0, 0, rows, :].astype(F32)
        ret = _rms(y, rn_ref[...]) * (rg * jax.nn.sigmoid(rg))
        reto_ref[0, 0, rows, :] = ret.astype(BF16)
        return carry

    lax.fori_loop(0, n_chunks, pass_c, 0, unroll=8)


def _mixer(p, lb, log_gamma, hg_norm, ret_norm, states):
    batch, _, seq, _ = p.shape
    n_chunks = seq // CHUNK

    def group(g):
        return pl.BlockSpec((1, 1, seq, DH), lambda b, h: (b, g * HEADS + h, 0, 0))

    per_head_row = pl.BlockSpec((1, DH), lambda b, h: (0, h))
    state = pl.BlockSpec((1, 1, DH, DH), lambda b, h: (b, h, 0, 0))
    out_spec = pl.BlockSpec((1, 1, seq, DH), lambda b, h: (b, h, 0, 0))
    out_shape = jax.ShapeDtypeStruct((batch, HEADS, seq, DH), BF16)
    return pl.pallas_call(
        _mixer_kernel,
        grid=(batch, HEADS),
        in_specs=[pl.BlockSpec(memory_space=pltpu.SMEM)]
        + [group(g) for g in range(N_GROUPS)]
        + [pl.BlockSpec((2, DH), lambda b, h: (0, h)), per_head_row, per_head_row,
           state, state, state, state],
        out_specs=[out_spec, out_spec],
        out_shape=[out_shape, out_shape],
        scratch_shapes=[
            pltpu.VMEM((seq, DH), F32), pltpu.VMEM((seq, DH), F32),
            pltpu.VMEM((seq, 2 * DH), BF16), pltpu.VMEM((seq, 2 * DH), BF16),
            pltpu.VMEM((n_chunks, DH, 2 * DH), F32),
            pltpu.VMEM((n_chunks, DH, 2 * DH), F32),
            pltpu.VMEM((n_chunks, DH, 2 * DH), BF16),
            pltpu.VMEM((n_chunks, DH, 2 * DH), BF16),
            pltpu.VMEM((n_chunks, 1, 2 * DH), F32),
        ],
        compiler_params=pltpu.CompilerParams(
            dimension_semantics=("parallel", "parallel"), vmem_limit_bytes=VMEM_LIMIT),
        name="token_mixer",
    )(log_gamma, p, p, p, p, p, p, p, p, p, lb, hg_norm, ret_norm, *states)


FFN_ROWS = 1024
HALO = 16


def _ffn_kernel(x_ref, xp_ref, xn_ref, hg_ref, hgp_ref, hgn_ref, rt_ref, rtp_ref, rtn_ref,
                g1_ref, sh2_ref, sc2_ref, g2_ref, woa_ref, wob_ref, n2_ref, fin_ref,
                wup_ref, cw_ref, cb_ref, wd_ref,
                o_ref, h2_s, gate2_s, gate3_s, up2_s, up3_s, acc_s):
    m = pl.program_id(1)
    rows = x_ref.shape[1]
    n_sub = rows // ROW_TILE
    n_ff = D_FF // FF_TILE

    def ext(k):
        lo = 0 if k == 0 else HALO + k * ROW_TILE
        hi = HALO + (k + 1) * ROW_TILE + (HALO if k == n_sub - 1 else 0)
        return lo, hi

    def piece(main_ref, prev_ref, next_ref, k):
        def rows_of(ref, sl):
            if len(ref.shape) == 3:
                return ref[0, sl, :]
            return jnp.concatenate([ref[0, h, sl, :] for h in range(HEADS)], axis=1)

        parts = ([rows_of(prev_ref, slice(None))] if k == 0 else []) \
            + [rows_of(main_ref, slice(k * ROW_TILE, (k + 1) * ROW_TILE))] \
            + ([rows_of(next_ref, slice(None))] if k == n_sub - 1 else [])
        return jnp.concatenate(parts, axis=0) if len(parts) > 1 else parts[0]

    for k in range(n_sub):
        lo, hi = ext(k)
        skip = HALO if k == 0 else 0
        proj = (_dot(piece(hg_ref, hgp_ref, hgn_ref, k), woa_ref[...])
                + _dot(piece(rt_ref, rtp_ref, rtn_ref, k), wob_ref[...]))
        x1 = piece(x_ref, xp_ref, xn_ref, k) + g1_ref[0] * proj
        o_ref[0, k * ROW_TILE:(k + 1) * ROW_TILE, :] = x1[skip:skip + ROW_TILE]
        h2 = _rms(x1, n2_ref[...]) * (1.0 + sc2_ref[0]) + sh2_ref[0]
        h2_s[lo:hi, :] = h2.astype(BF16)

    prev_valid = (m > 0).astype(F32)
    next_valid = (m < pl.num_programs(1) - 1).astype(F32)
    edge_lo = slice(HALO - 8, HALO)
    edge_hi = slice(HALO + rows, HALO + rows + 8)

    gate_bufs = (gate2_s, gate3_s)
    up_bufs = (up2_s, up3_s)

    def gate_up(j):
        gate_s, up_s = gate_bufs[j % 2], up_bufs[j % 2]
        w_gu = jnp.concatenate([wup_ref[:, j * FF_TILE:(j + 1) * FF_TILE],
                                wup_ref[:, D_FF + j * FF_TILE:D_FF + (j + 1) * FF_TILE]], axis=1)
        for k in range(n_sub):
            lo, hi = ext(k)
            skip = HALO if k == 0 else 0
            gu = _dot(h2_s[lo:hi, :], w_gu)
            gate_s[lo:hi, :] = gu[:, :FF_TILE]
            up_s[k * ROW_TILE:(k + 1) * ROW_TILE, :] = gu[skip:skip + ROW_TILE, FF_TILE:]
        gate_s[edge_lo, :] = gate_s[edge_lo, :] * prev_valid
        gate_s[edge_hi, :] = gate_s[edge_hi, :] * next_valid

    gate_up(0)
    for j in range(n_ff):
        if j + 1 < n_ff:
            gate_up(j + 1)
        cols = slice(j * FF_TILE, (j + 1) * FF_TILE)
        gate_s, up_s = gate_bufs[j % 2], up_bufs[j % 2]
        for k in range(n_sub):
            sub = slice(k * ROW_TILE, (k + 1) * ROW_TILE)
            base = HALO + k * ROW_TILE
            conv = (gate_s[base - 1:base - 1 + ROW_TILE, :] * cw_ref[0:1, cols]
                    + gate_s[base:base + ROW_TILE, :] * cw_ref[1:2, cols]
                    + gate_s[base + 1:base + 1 + ROW_TILE, :] * cw_ref[2:3, cols]
                    + cb_ref[:, cols])
            act = (conv * jax.nn.sigmoid(conv) * up_s[sub, :]).astype(BF16)
            ffn = _dot(act, wd_ref[cols, :])
            if j > 0:
                ffn += acc_s[sub, :]
            if j < n_ff - 1:
                acc_s[sub, :] = ffn
            else:
                x2 = o_ref[0, sub, :] + g2_ref[0] * ffn
                o_ref[0, sub, :] = _rms(x2, fin_ref[...])


def _out_ffn(x, hgo, reto, mod3, w_out_bf16, norm2_g, final_g, w_up_bf16, conv_w, conv_b, w_down_bf16):
    batch, seq, _ = x.shape
    rows = min(FFN_ROWS, seq)
    n_row_tiles = seq // rows
    halo_per_tile = rows // HALO
    n_halo_blocks = seq // HALO
    once = pl.Buffered(1)

    def prev_block(m):
        return jnp.maximum(m * halo_per_tile - 1, 0)

    def next_block(m):
        return jnp.minimum((m + 1) * halo_per_tile, n_halo_blocks - 1)

    x_main = pl.BlockSpec((1, rows, D_MODEL), lambda b, m: (b, m, 0))
    x_prev = pl.BlockSpec((1, HALO, D_MODEL), lambda b, m: (b, prev_block(m), 0))
    x_next = pl.BlockSpec((1, HALO, D_MODEL), lambda b, m: (b, next_block(m), 0))
    h_main = pl.BlockSpec((1, HEADS, rows, DH), lambda b, m: (b, 0, m, 0))
    h_prev = pl.BlockSpec((1, HEADS, HALO, DH), lambda b, m: (b, 0, prev_block(m), 0))
    h_next = pl.BlockSpec((1, HEADS, HALO, DH), lambda b, m: (b, 0, next_block(m), 0))

    def mod_spec(chunk):
        return pl.BlockSpec((1, 1, D_MODEL), lambda b, m: (b, 0, chunk))

    def const(shape, index=(0, 0)):
        return pl.BlockSpec(shape, lambda b, m: index, pipeline_mode=once)

    return pl.pallas_call(
        _ffn_kernel,
        grid=(batch, n_row_tiles),
        in_specs=[
            x_main, x_prev, x_next,
            h_main, h_prev, h_next,
            h_main, h_prev, h_next,
            mod_spec(2), mod_spec(3), mod_spec(4), mod_spec(5),
            const((GROUP_W, D_MODEL), (0, 0)),
            const((GROUP_W, D_MODEL), (1, 0)),
            const((1, D_MODEL)), const((1, D_MODEL)),
            const((D_MODEL, 2 * D_FF)),
            const((3, D_FF)), const((1, D_FF)),
            const((D_FF, D_MODEL)),
        ],
        out_specs=pl.BlockSpec((1, rows, D_MODEL), lambda b, m: (b, m, 0)),
        out_shape=jax.ShapeDtypeStruct((batch, seq, D_MODEL), F32),
        scratch_shapes=[
            pltpu.VMEM((rows + 2 * HALO, D_MODEL), BF16),
            pltpu.VMEM((rows + 2 * HALO, FF_TILE), F32),
            pltpu.VMEM((rows + 2 * HALO, FF_TILE), F32),
            pltpu.VMEM((rows, FF_TILE), F32),
            pltpu.VMEM((rows, FF_TILE), F32),
            pltpu.VMEM((rows, D_MODEL), F32),
        ],
        compiler_params=pltpu.CompilerParams(
            dimension_semantics=("parallel", "parallel"), vmem_limit_bytes=VMEM_LIMIT),
        name="out_proj_ffn",
    )(x, x, x, hgo, hgo, hgo, reto, reto, reto, mod3, mod3, mod3, mod3,
      w_out_bf16, w_out_bf16, norm2_g, final_g, w_up_bf16, conv_w, conv_b, w_down_bf16)


def _rope_tables(seq):
    quarter = DH // 4
    freqs = np.float32(ROPE_THETA) ** (-np.arange(quarter, dtype=np.float32) / np.float32(quarter))
    t = np.arange(seq)
    ang_r = (t // GRID_W).astype(np.float32)[:, None] * freqs[None, :]
    ang_c = (t % GRID_W).astype(np.float32)[:, None] * freqs[None, :]
    cos = np.concatenate([np.cos(ang_r)] * 2 + [np.cos(ang_c)] * 2, axis=-1)
    sin = np.concatenate([-np.sin(ang_r), np.sin(ang_r), -np.sin(ang_c), np.sin(ang_c)], axis=-1)
    return jnp.asarray(cos, F32), jnp.asarray(sin, F32)


def kernel(x, c, ctx, c_ctx, w_mod, b_mod, norm1_g, w_in, hgrn_lb, hgrn_norm_g, ret_decay,
           ret_norm_g, w_out, norm2_g, w_up, conv_w, conv_b, w_down, final_g):
    batch, seq, _ = x.shape
    n_ctx = ctx.shape[1]
    assert w_mod.shape[0] == 1, "single-layer block"
    assert seq % CHUNK == 0 and seq % ROW_TILE == 0 and (n_ctx & (n_ctx - 1)) == 0

    lb = jnp.cumsum(jax.nn.softmax(hgrn_lb.astype(F32), axis=1), axis=1)[:, 0]
    log_gamma = jax.nn.log_sigmoid(ret_decay[0].astype(F32))
    cos_t, sin_t = _rope_tables(seq)

    n_mod_rows = -(-(batch + 1) // 8) * 8
    c_rows = jnp.zeros((n_mod_rows, D_MODEL), F32).at[:batch].set(c).at[batch].set(c_ctx)
    mod = _modulation(c_rows, w_mod[0], b_mod[0][None, :])
    mod3 = mod.reshape(n_mod_rows, 1, 6 * D_MODEL)

    w_in_b = w_in[0].astype(BF16)
    norm1 = norm1_g[0][None, :]
    states = _context_states(ctx, mod3[batch:batch + 1], norm1, w_in_b, lb, log_gamma)
    p = _in_projection(x.reshape(batch * seq, D_MODEL), mod3[:batch], seq, norm1, w_in_b, cos_t, sin_t)
    hgo, reto = _mixer(p, lb, log_gamma,
                       hgrn_norm_g[0][None, :], ret_norm_g[0][None, :], states)

    return _out_ffn(x, hgo, reto, mod3[:batch], w_out[0].astype(BF16), norm2_g[0][None, :],
                    final_g[None, :], w_up[0].astype(BF16), conv_w[0], conv_b[0][None, :],
                    w_down[0].astype(BF16))
```

```python
import jax
import jax.numpy as jnp
import numpy as np
from jax import lax
from jax.experimental import pallas as pl
from jax.experimental.pallas import tpu as pltpu

F32 = jnp.float32
BF16 = jnp.bfloat16

D_MODEL = 1024
HEADS = 4
DH = 128
GROUP_W = HEADS * DH
N_GROUPS = 9
IN_WIDTH = N_GROUPS * GROUP_W
D_FF = 2816
GRID_W = 64
ROPE_THETA = 10000.0
EPS = 1e-6
LOG2E = 1.4426950408889634

CHUNK = 128
FF_TILE = 256
ROW_TILE = 512
IN_ROW_TILE = 1024
PASS_A_UNROLL = 16
VMEM_LIMIT = 56 * 1024 * 1024


def _dot(a, b):
    return jnp.dot(a, b, preferred_element_type=F32)


def _dot_nt(a, b):
    return lax.dot_general(a, b, (((1,), (1,)), ((), ())), preferred_element_type=F32)


def _dot_tn(a, b):
    return lax.dot_general(a, b, (((0,), (0,)), ((), ())), preferred_element_type=F32)


def _rms(x, gain):
    return x * lax.rsqrt(jnp.mean(x * x, axis=-1, keepdims=True) + EPS) * gain


def _cumsum_rows(x):
    rows = x.shape[0]
    row = lax.broadcasted_iota(jnp.int32, x.shape, 0)
    shift = 1
    while shift < rows:
        x = x + jnp.where(row >= shift, pltpu.roll(x, shift, axis=0), 0.0)
        shift *= 2
    return x


def _mod_kernel(c_ref, w_ref, b_ref, o_ref):
    c = c_ref[...]
    a = (c * jax.nn.sigmoid(c)).astype(BF16)
    o_ref[...] = _dot(a, w_ref[...].astype(BF16)) + b_ref[...]


def _modulation(c_rows, w_mod, b_mod):
    n_rows = c_rows.shape[0]
    width = w_mod.shape[1]
    tile = D_MODEL
    return pl.pallas_call(
        _mod_kernel,
        grid=(width // tile,),
        in_specs=[
            pl.BlockSpec((n_rows, D_MODEL), lambda j: (0, 0)),
            pl.BlockSpec((D_MODEL, tile), lambda j: (0, j)),
            pl.BlockSpec((1, tile), lambda j: (0, j)),
        ],
        out_specs=pl.BlockSpec((n_rows, tile), lambda j: (0, j)),
        out_shape=jax.ShapeDtypeStruct((n_rows, width), F32),
        compiler_params=pltpu.CompilerParams(dimension_semantics=("parallel",)),
        name="modulation",
    )(c_rows, w_mod, b_mod)


G_HQ, G_HV, G_ZF, G_ZB, G_HG, G_RQ, G_RK, G_RV, G_RG = range(N_GROUPS)
GROUP_ORDER = (G_RQ, G_RK, G_HQ, G_HV, G_ZF, G_ZB, G_HG, G_RV, G_RG)


def _rope(t, cos, sin_signed, first_quarter):
    swapped = jnp.where(first_quarter, pltpu.roll(t, DH - DH // 4, axis=1),
                        pltpu.roll(t, DH // 4, axis=1))
    return t * cos + swapped * sin_signed


def _modulated_norm(x, gain, scale, shift):
    return (_rms(x, gain) * (1.0 + scale) + shift).astype(BF16)


def _inproj_kernel(x_ref, sh_ref, sc_ref, g_ref, w_ref, cos_ref, sin_ref, o_ref):
    hb = _modulated_norm(x_ref[...], g_ref[...], sc_ref[0], sh_ref[0])
    cos = cos_ref[...]
    sin = sin_ref[...]
    lane = lax.broadcasted_iota(jnp.int32, cos.shape, 1)
    first_quarter = (lane % (DH // 2)) < (DH // 4)
    for g in GROUP_ORDER:
        cols = slice(g * GROUP_W, (g + 1) * GROUP_W)
        r = _dot(hb, w_ref[:, cols])
        if g == G_RK:
            r = r * (DH ** -0.5)
        for h in range(HEADS):
            t = r[:, h * DH:(h + 1) * DH]
            if g in (G_RQ, G_RK):
                t = _rope(t, cos, sin, first_quarter)
            o_ref[0, g * HEADS + h] = t.astype(BF16)


def _in_projection(x2d, mod3, seq, norm_g, w_in_bf16, cos_t, sin_t):
    n_rows = x2d.shape[0]
    row_tile = min(IN_ROW_TILE, seq)
    tiles_per_seq = seq // row_tile

    def mod_spec(chunk):
        return pl.BlockSpec((1, 1, D_MODEL), lambda i: (i // tiles_per_seq, 0, chunk))

    table = pl.BlockSpec((row_tile, DH), lambda i: (i % tiles_per_seq, 0))
    return pl.pallas_call(
        _inproj_kernel,
        grid=(n_rows // row_tile,),
        in_specs=[
            pl.BlockSpec((row_tile, D_MODEL), lambda i: (i, 0)),
            mod_spec(0),
            mod_spec(1),
            pl.BlockSpec((1, D_MODEL), lambda i: (0, 0)),
            pl.BlockSpec((D_MODEL, IN_WIDTH), lambda i: (0, 0), pipeline_mode=pl.Buffered(1)),
            table, table,
        ],
        out_specs=pl.BlockSpec((1, N_GROUPS * HEADS, row_tile, DH),
                               lambda i: (i // tiles_per_seq, 0, i % tiles_per_seq, 0)),
        out_shape=jax.ShapeDtypeStruct((n_rows // seq, N_GROUPS * HEADS, seq, DH), BF16),
        compiler_params=pltpu.CompilerParams(
            dimension_semantics=("parallel",), vmem_limit_bytes=VMEM_LIMIT),
        name="in_projection",
    )(x2d, mod3, mod3, norm_g, w_in_bf16, cos_t, sin_t)


def _forget(z, lb):
    f = lb + (1.0 - lb) * jax.nn.sigmoid(z)
    return f, jnp.log(f)


def _ctx_kernel(lg_ref, c_ref, sh_ref, sc_ref, g_ref, w_ref, lb_ref,
                sf_ref, sb_ref, rf_ref, rb_ref):
    n = c_ref.shape[1]
    hb = _modulated_norm(c_ref[0], g_ref[...], sc_ref[0], sh_ref[0])

    def group(g):
        return _dot(hb, w_ref[:, g * GROUP_W:(g + 1) * GROUP_W])

    v = group(G_HV).astype(BF16)
    ff, lff = _forget(group(G_ZF), lb_ref[0:1, :])
    fb, lfb = _forget(group(G_ZB), lb_ref[1:2, :])
    bf = _cumsum_rows(lff)
    bb = _cumsum_rows(lfb)
    kf = ((1.0 - ff) * jnp.exp(bf[n - 1:n, :] - bf)).astype(BF16)
    kb = ((1.0 - fb) * jnp.exp(bb - lfb)).astype(BF16)
    rk = group(G_RK) * (DH ** -0.5)
    rv = group(G_RV).astype(BF16)
    pos = lax.broadcasted_iota(jnp.int32, (n, DH), 0).astype(F32)
    for h in range(HEADS):
        cols = slice(h * DH, (h + 1) * DH)
        sf_ref[0, h] = _dot_tn(v[:, cols], kf[:, cols])
        sb_ref[0, h] = _dot_tn(v[:, cols], kb[:, cols])
        wf = jnp.exp((n - 1.0 - pos) * lg_ref[0, h])
        wb = jnp.exp(pos * lg_ref[1, h])
        rf_ref[0, h] = _dot_tn(rv[:, cols], (rk[:, cols] * wf).astype(BF16))
        rb_ref[0, h] = _dot_tn(rv[:, cols], (rk[:, cols] * wb).astype(BF16))


def _context_states(ctx, mod_ctx, norm_g, w_in_bf16, lb, log_gamma):
    batch, n_ctx, _ = ctx.shape

    def mod_spec(chunk):
        return pl.BlockSpec((1, 1, D_MODEL), lambda b: (0, 0, chunk))

    state = pl.BlockSpec((1, HEADS, DH, DH), lambda b: (b, 0, 0, 0))
    state_shape = jax.ShapeDtypeStruct((batch, HEADS, DH, DH), F32)
    return pl.pallas_call(
        _ctx_kernel,
        grid=(batch,),
        in_specs=[
            pl.BlockSpec(memory_space=pltpu.SMEM),
            pl.BlockSpec((1, n_ctx, D_MODEL), lambda b: (b, 0, 0)),
            mod_spec(0), mod_spec(1),
            pl.BlockSpec((1, D_MODEL), lambda b: (0, 0)),
            pl.BlockSpec((D_MODEL, IN_WIDTH), lambda b: (0, 0), pipeline_mode=pl.Buffered(1)),
            pl.BlockSpec((2, GROUP_W), lambda b: (0, 0)),
        ],
        out_specs=[state, state, state, state],
        out_shape=[state_shape] * 4,
        compiler_params=pltpu.CompilerParams(
            dimension_semantics=("parallel",), vmem_limit_bytes=VMEM_LIMIT),
        name="context_states",
    )(log_gamma, ctx, mod_ctx, mod_ctx, norm_g, w_in_bf16, lb)


def _anchor(phi, half, forward):
    rows = phi.shape[0]
    pick = half - 1 if forward else half
    block = 2 * half
    p3 = phi.reshape(rows // 8, 8, DH)
    sub = lax.broadcasted_iota(jnp.int32, p3.shape, 1)
    out = None
    for start in range(0, 8, block):
        a = jnp.broadcast_to(p3[:, start + pick:start + pick + 1, :], p3.shape)
        out = a if out is None else jnp.where(sub >= start, a, out)
    return out.reshape(rows, DH)


def _level_operands(q, kf, kb, ff, fb, phif, phib, half, row):
    n = q.shape[0]
    if half == 1:
        odd = (row & 1) != 0
        zz = q * jnp.where(odd, ff, fb)
        ww = jnp.where(odd, kb, kf)
    elif half >= 8:
        zs, ws = [], []
        for b0 in range(0, n, 2 * half):
            first = slice(b0, b0 + half)
            second = slice(b0 + half, b0 + 2 * half)
            af = phif[b0 + half - 1:b0 + half, :]
            ab = phib[b0 + half:b0 + half + 1, :]
            zs += [q[first] * jnp.exp2(phib[first] - ab), q[second] * jnp.exp2(phif[second] - af)]
            ws += [kf[first] * jnp.exp2(af - phif[first]), kb[second] * jnp.exp2(ab - phib[second])]
        zz = jnp.concatenate(zs, axis=0)
        ww = jnp.concatenate(ws, axis=0)
    else:
        af = _anchor(phif, half, True)
        ab = _anchor(phib, half, False)
        second = (row & half) != 0
        zz = q * jnp.exp2(jnp.where(second, phif - af, phib - ab))
        ww = jnp.where(second, kb, kf) * jnp.exp2(jnp.where(second, ab - phib, af - phif))
    return zz.astype(BF16), ww.astype(BF16)


def _mixer_kernel(lg_ref, q_ref, v_ref, zf_ref, zb_ref, og_ref, rq_ref, rk_ref, rv_ref, rg_ref,
                  lb_ref, hgn_ref, rn_ref,
                  sf0_ref, sb0_ref, rf0_ref, rb0_ref,
                  hgo_ref, reto_ref,
                  oh_s, or_s, qcat_s, rqcat_s, sh_s, sr_s, shb_s, srb_s, dec_s):
    head = pl.program_id(1)
    seq = q_ref.shape[2]
    n_chunks = seq // CHUNK
    lgf = lg_ref[0, head]
    lgb = lg_ref[1, head]
    lbf = lb_ref[0:1, :]
    lbb = lb_ref[1:2, :]
    fwd = slice(0, DH)
    bwd = slice(DH, 2 * DH)

    row = lax.broadcasted_iota(jnp.int32, (CHUNK, DH), 0)
    t_idx = lax.broadcasted_iota(jnp.int32, (CHUNK, CHUNK), 0)
    s_idx = lax.broadcasted_iota(jnp.int32, (CHUNK, CHUNK), 1)
    split = t_idx ^ s_idx
    rel = (t_idx - s_idx).astype(F32)
    ret_decay = jnp.where(rel >= 0, jnp.exp(rel * lgf), 0.0) + jnp.where(rel <= 0, jnp.exp(-rel * lgb), 0.0)
    pos = row.astype(F32)
    qdec_f = jnp.exp((pos + 1.0) * lgf)
    kdec_f = jnp.exp((CHUNK - 1.0 - pos) * lgf)
    qdec_b = jnp.exp((CHUNK - pos) * lgb)
    kdec_b = jnp.exp(pos * lgb)

    def pass_a(i, carry):
        rows = pl.ds(pl.multiple_of(i * CHUNK, CHUNK), CHUNK)
        q = q_ref[0, 0, rows, :].astype(F32)
        v = v_ref[0, 0, rows, :]
        ff, lff = _forget(zf_ref[0, 0, rows, :].astype(F32), lbf)
        fb, lfb = _forget(zb_ref[0, 0, rows, :].astype(F32), lbb)
        lff = lff * LOG2E
        lfb = lfb * LOG2E
        kf = 1.0 - ff
        kb = 1.0 - fb
        cf = _cumsum_rows(lff)
        cb = _cumsum_rows(lfb)
        phif = cf
        phib = lfb - cb

        attn = None
        half = CHUNK // 2
        while half >= 1:
            zz, ww = _level_operands(q, kf, kb, ff, fb, phif, phib, half, row)
            a = _dot_nt(zz, ww)
            attn = a if attn is None else jnp.where(split < 2 * half, a, attn)
            half //= 2
        attn = jnp.where(split == 0, 0.0, attn)
        diag = jnp.sum(q * (kf + kb), axis=-1, keepdims=True)
        oh_s[rows, :] = _dot(attn.astype(BF16), v) + diag * v.astype(F32)

        last_f = cf[CHUNK - 1:CHUNK, :]
        last_b = cb[CHUNK - 1:CHUNK, :]
        qcat_s[rows, fwd] = (q * jnp.exp2(cf)).astype(BF16)
        qcat_s[rows, bwd] = (q * jnp.exp2(last_b + phib)).astype(BF16)
        kcat = jnp.concatenate([(kf * jnp.exp2(last_f - cf)).astype(BF16),
                                (kb * jnp.exp2(cb - lfb)).astype(BF16)], axis=1)
        sh_s[i] = _dot_tn(v, kcat)
        dec_s[i, :, fwd] = jnp.exp2(last_f)
        dec_s[i, :, bwd] = jnp.exp2(last_b)

        rqb = rq_ref[0, 0, rows, :]
        rkb = rk_ref[0, 0, rows, :]
        rq = rqb.astype(F32)
        rk = rkb.astype(F32)
        rv = rv_ref[0, 0, rows, :]
        scores = _dot_nt(rqb, rkb) * ret_decay
        or_s[rows, :] = _dot(scores.astype(BF16), rv)
        rqcat_s[rows, fwd] = (rq * qdec_f).astype(BF16)
        rqcat_s[rows, bwd] = (rq * qdec_b).astype(BF16)
        rkcat = jnp.concatenate([(rk * kdec_f).astype(BF16), (rk * kdec_b).astype(BF16)], axis=1)
        sr_s[i] = _dot_tn(rv, rkcat)
        return carry

    lax.fori_loop(0, n_chunks, pass_a, 0, unroll=min(PASS_A_UNROLL, n_chunks))

    cdec_f = jnp.exp(CHUNK * lgf)
    cdec_b = jnp.exp(CHUNK * lgb)
    sf, rf = sf0_ref[0, 0], rf0_ref[0, 0]
    for i in range(n_chunks):
        shb_s[i, :, fwd] = sf.astype(BF16)
        srb_s[i, :, fwd] = rf.astype(BF16)
        sf = sf * dec_s[i, :, fwd] + sh_s[i, :, fwd]
        rf = rf * cdec_f + sr_s[i, :, fwd]
    sb, rb = sb0_ref[0, 0], rb0_ref[0, 0]
    for i in reversed(range(n_chunks)):
        shb_s[i, :, bwd] = sb.astype(BF16)
        srb_s[i, :, bwd] = rb.astype(BF16)
        sb = sb * dec_s[i, :, bwd] + sh_s[i, :, bwd]
        rb = rb * cdec_b + sr_s[i, :, bwd]

    def pass_c(i, carry):
        rows = pl.ds(pl.multiple_of(i * CHUNK, CHUNK), CHUNK)
        o = oh_s[rows, :] + _dot_nt(qcat_s[rows, :], shb_s[i])
        hg = o * jax.nn.sigmoid(og_ref[0, 0, rows, :].astype(F32))
        hgo_ref[0, 0, rows, :] = _rms(hg, hgn_ref[...]).astype(BF16)
        y = or_s[rows, :] + _dot_nt(rqcat_s[rows, :], srb_s[i])
        rg = rg_ref[0, 0, rows, :].astype(F32)
        ret = _rms(y, rn_ref[...]) * (rg * jax.nn.sigmoid(rg))
        reto_ref[0, 0, rows, :] = ret.astype(BF16)
        return carry

    lax.fori_loop(0, n_chunks, pass_c, 0, unroll=min(8, n_chunks))


def _mixer(p, lb, log_gamma, hg_norm, ret_norm, states):
    batch, _, seq, _ = p.shape
    n_chunks = seq // CHUNK

    def group(g):
        return pl.BlockSpec((1, 1, seq, DH), lambda b, h: (b, g * HEADS + h, 0, 0))

    per_head_row = pl.BlockSpec((1, DH), lambda b, h: (0, h))
    state = pl.BlockSpec((1, 1, DH, DH), lambda b, h: (b, h, 0, 0))
    out_spec = pl.BlockSpec((1, 1, seq, DH), lambda b, h: (b, h, 0, 0))
    out_shape = jax.ShapeDtypeStruct((batch, HEADS, seq, DH), BF16)
    return pl.pallas_call(
        _mixer_kernel,
        grid=(batch, HEADS),
        in_specs=[pl.BlockSpec(memory_space=pltpu.SMEM)]
        + [group(g) for g in range(N_GROUPS)]
        + [pl.BlockSpec((2, DH), lambda b, h: (0, h)), per_head_row, per_head_row,
           state, state, state, state],
        out_specs=[out_spec, out_spec],
        out_shape=[out_shape, out_shape],
        scratch_shapes=[
            pltpu.VMEM((seq, DH), F32), pltpu.VMEM((seq, DH), F32),
            pltpu.VMEM((seq, 2 * DH), BF16), pltpu.VMEM((seq, 2 * DH), BF16),
            pltpu.VMEM((n_chunks, DH, 2 * DH), F32),
            pltpu.VMEM((n_chunks, DH, 2 * DH), F32),
            pltpu.VMEM((n_chunks, DH, 2 * DH), BF16),
            pltpu.VMEM((n_chunks, DH, 2 * DH), BF16),
            pltpu.VMEM((n_chunks, 1, 2 * DH), F32),
        ],
        compiler_params=pltpu.CompilerParams(
            dimension_semantics=("parallel", "parallel"), vmem_limit_bytes=VMEM_LIMIT),
        name="token_mixer",
    )(log_gamma, p, p, p, p, p, p, p, p, p, lb, hg_norm, ret_norm, *states)


FFN_ROWS = 1024
HALO = 16


def _ffn_kernel(x_ref, xp_ref, xn_ref, hg_ref, hgp_ref, hgn_ref, rt_ref, rtp_ref, rtn_ref,
                g1_ref, sh2_ref, sc2_ref, g2_ref, woa_ref, wob_ref, n2_ref, fin_ref,
                wup_ref, cw_ref, cb_ref, wd_ref,
                o_ref, h2_s, gate2_s, gate3_s, up2_s, up3_s, acc_s):
    m = pl.program_id(1)
    rows = x_ref.shape[1]
    n_sub = rows // ROW_TILE
    n_ff = D_FF // FF_TILE

    def ext(k):
        lo = 0 if k == 0 else HALO + k * ROW_TILE
        hi = HALO + (k + 1) * ROW_TILE + (HALO if k == n_sub - 1 else 0)
        return lo, hi

    def piece(main_ref, prev_ref, next_ref, k):
        def rows_of(ref, sl):
            if len(ref.shape) == 3:
                return ref[0, sl, :]
            return jnp.concatenate([ref[0, h, sl, :] for h in range(HEADS)], axis=1)

        parts = ([rows_of(prev_ref, slice(None))] if k == 0 else []) \
            + [rows_of(main_ref, slice(k * ROW_TILE, (k + 1) * ROW_TILE))] \
            + ([rows_of(next_ref, slice(None))] if k == n_sub - 1 else [])
        return jnp.concatenate(parts, axis=0) if len(parts) > 1 else parts[0]

    for k in range(n_sub):
        lo, hi = ext(k)
        skip = HALO if k == 0 else 0
        proj = (_dot(piece(hg_ref, hgp_ref, hgn_ref, k), woa_ref[...])
                + _dot(piece(rt_ref, rtp_ref, rtn_ref, k), wob_ref[...]))
        x1 = piece(x_ref, xp_ref, xn_ref, k) + g1_ref[0] * proj
        o_ref[0, k * ROW_TILE:(k + 1) * ROW_TILE, :] = x1[skip:skip + ROW_TILE]
        h2 = _rms(x1, n2_ref[...]) * (1.0 + sc2_ref[0]) + sh2_ref[0]
        h2_s[lo:hi, :] = h2.astype(BF16)

    prev_valid = (m > 0).astype(F32)
    next_valid = (m < pl.num_programs(1) - 1).astype(F32)
    edge_lo = slice(HALO - 8, HALO)
    edge_hi = slice(HALO + rows, HALO + rows + 8)

    gate_bufs = (gate2_s, gate3_s)
    up_bufs = (up2_s, up3_s)

    def gate_up(j):
        gate_s, up_s = gate_bufs[j % 2], up_bufs[j % 2]
        w_gu = jnp.concatenate([wup_ref[:, j * FF_TILE:(j + 1) * FF_TILE],
                                wup_ref[:, D_FF + j * FF_TILE:D_FF + (j + 1) * FF_TILE]], axis=1)
        for k in range(n_sub):
            lo, hi = ext(k)
            skip = HALO if k == 0 else 0
            gu = _dot(h2_s[lo:hi, :], w_gu)
            gate_s[lo:hi, :] = gu[:, :FF_TILE]
            up_s[k * ROW_TILE:(k + 1) * ROW_TILE, :] = gu[skip:skip + ROW_TILE, FF_TILE:]
        gate_s[edge_lo, :] = gate_s[edge_lo, :] * prev_valid
        gate_s[edge_hi, :] = gate_s[edge_hi, :] * next_valid

    gate_up(0)
    for j in range(n_ff):
        if j + 1 < n_ff:
            gate_up(j + 1)
        cols = slice(j * FF_TILE, (j + 1) * FF_TILE)
        gate_s, up_s = gate_bufs[j % 2], up_bufs[j % 2]
        for k in range(n_sub):
            sub = slice(k * ROW_TILE, (k + 1) * ROW_TILE)
            base = HALO + k * ROW_TILE
            conv = (gate_s[base - 1:base - 1 + ROW_TILE, :] * cw_ref[0:1, cols]
                    + gate_s[base:base + ROW_TILE, :] * cw_ref[1:2, cols]
                    + gate_s[base + 1:base + 1 + ROW_TILE, :] * cw_ref[2:3, cols]
                    + cb_ref[:, cols])
            act = (conv * jax.nn.sigmoid(conv) * up_s[sub, :]).astype(BF16)
            ffn = _dot(act, wd_ref[cols, :])
            if j > 0:
                ffn += acc_s[sub, :]
            if j < n_ff - 1:
                acc_s[sub, :] = ffn
            else:
                x2 = o_ref[0, sub, :] + g2_ref[0] * ffn
                o_ref[0, sub, :] = _rms(x2, fin_ref[...])


def _out_ffn(x, hgo, reto, mod3, w_out_bf16, norm2_g, final_g, w_up_bf16, conv_w, conv_b, w_down_bf16):
    batch, seq, _ = x.shape
    rows = min(FFN_ROWS, seq)
    n_row_tiles = seq // rows
    halo_per_tile = rows // HALO
    n_halo_blocks = seq // HALO
    once = pl.Buffered(1)

    def prev_block(m):
        return jnp.maximum(m * halo_per_tile - 1, 0)

    def next_block(m):
        return jnp.minimum((m + 1) * halo_per_tile, n_halo_blocks - 1)

    x_main = pl.BlockSpec((1, rows, D_MODEL), lambda b, m: (b, m, 0))
    x_prev = pl.BlockSpec((1, HALO, D_MODEL), lambda b, m: (b, prev_block(m), 0))
    x_next = pl.BlockSpec((1, HALO, D_MODEL), lambda b, m: (b, next_block(m), 0))
    h_main = pl.BlockSpec((1, HEADS, rows, DH), lambda b, m: (b, 0, m, 0))
    h_prev = pl.BlockSpec((1, HEADS, HALO, DH), lambda b, m: (b, 0, prev_block(m), 0))
    h_next = pl.BlockSpec((1, HEADS, HALO, DH), lambda b, m: (b, 0, next_block(m), 0))

    def mod_spec(chunk):
        return pl.BlockSpec((1, 1, D_MODEL), lambda b, m: (b, 0, chunk))

    def const(shape, index=(0, 0)):
        return pl.BlockSpec(shape, lambda b, m: index, pipeline_mode=once)

    return pl.pallas_call(
        _ffn_kernel,
        grid=(batch, n_row_tiles),
        in_specs=[
            x_main, x_prev, x_next,
            h_main, h_prev, h_next,
            h_main, h_prev, h_next,
            mod_spec(2), mod_spec(3), mod_spec(4), mod_spec(5),
            const((GROUP_W, D_MODEL), (0, 0)),
            const((GROUP_W, D_MODEL), (1, 0)),
            const((1, D_MODEL)), const((1, D_MODEL)),
            const((D_MODEL, 2 * D_FF)),
            const((3, D_FF)), const((1, D_FF)),
            const((D_FF, D_MODEL)),
        ],
        out_specs=pl.BlockSpec((1, rows, D_MODEL), lambda b, m: (b, m, 0)),
        out_shape=jax.ShapeDtypeStruct((batch, seq, D_MODEL), F32),
        scratch_shapes=[
            pltpu.VMEM((rows + 2 * HALO, D_MODEL), BF16),
            pltpu.VMEM((rows + 2 * HALO, FF_TILE), F32),
            pltpu.VMEM((rows + 2 * HALO, FF_TILE), F32),
            pltpu.VMEM((rows, FF_TILE), F32),
            pltpu.VMEM((rows, FF_TILE), F32),
            pltpu.VMEM((rows, D_MODEL), F32),
        ],
        compiler_params=pltpu.CompilerParams(
            dimension_semantics=("parallel", "parallel"), vmem_limit_bytes=VMEM_LIMIT),
        name="out_proj_ffn",
    )(x, x, x, hgo, hgo, hgo, reto, reto, reto, mod3, mod3, mod3, mod3,
      w_out_bf16, w_out_bf16, norm2_g, final_g, w_up_bf16, conv_w, conv_b, w_down_bf16)


def _rope_tables(seq):
    quarter = DH // 4
    freqs = np.float32(ROPE_THETA) ** (-np.arange(quarter, dtype=np.float32) / np.float32(quarter))
    t = np.arange(seq)
    ang_r = (t // GRID_W).astype(np.float32)[:, None] * freqs[None, :]
    ang_c = (t % GRID_W).astype(np.float32)[:, None] * freqs[None, :]
    cos = np.concatenate([np.cos(ang_r)] * 2 + [np.cos(ang_c)] * 2, axis=-1)
    sin = np.concatenate([-np.sin(ang_r), np.sin(ang_r), -np.sin(ang_c), np.sin(ang_c)], axis=-1)
    return jnp.asarray(cos, F32), jnp.asarray(sin, F32)


def kernel(x, c, ctx, c_ctx, w_mod, b_mod, norm1_g, w_in, hgrn_lb, hgrn_norm_g, ret_decay,
           ret_norm_g, w_out, norm2_g, w_up, conv_w, conv_b, w_down, final_g):
    batch, seq, _ = x.shape
    n_ctx = ctx.shape[1]
    assert w_mod.shape[0] == 1, "single-layer block"
    assert seq % CHUNK == 0 and seq % ROW_TILE == 0 and (n_ctx & (n_ctx - 1)) == 0

    lb = jnp.cumsum(jax.nn.softmax(hgrn_lb.astype(F32), axis=1), axis=1)[:, 0]
    log_gamma = jax.nn.log_sigmoid(ret_decay[0].astype(F32))
    cos_t, sin_t = _rope_tables(seq)

    n_mod_rows = -(-(batch + 1) // 8) * 8
    c_rows = jnp.zeros((n_mod_rows, D_MODEL), F32).at[:batch].set(c).at[batch].set(c_ctx)
    mod = _modulation(c_rows, w_mod[0], b_mod[0][None, :])
    mod3 = mod.reshape(n_mod_rows, 1, 6 * D_MODEL)

    w_in_b = w_in[0].astype(BF16)
    norm1 = norm1_g[0][None, :]
    states = _context_states(ctx, mod3[batch:batch + 1], norm1, w_in_b, lb, log_gamma)
    p = _in_projection(x.reshape(batch * seq, D_MODEL), mod3[:batch], seq, norm1, w_in_b, cos_t, sin_t)
    hgo, reto = _mixer(p, lb, log_gamma,
                       hgrn_norm_g[0][None, :], ret_norm_g[0][None, :], states)

    return _out_ffn(x, hgo, reto, mod3[:batch], w_out[0].astype(BF16), norm2_g[0][None, :],
                    final_g[None, :], w_up[0].astype(BF16), conv_w[0], conv_b[0][None, :],
                    w_down[0].astype(BF16))
```

```python
import jax
import jax.numpy as jnp
import numpy as np
from jax import lax
from jax.experimental import pallas as pl
from jax.experimental.pallas import tpu as pltpu

F32 = jnp.float32
BF16 = jnp.bfloat16

D_MODEL = 1024
HEADS = 4
DH = 128
GROUP_W = HEADS * DH
N_GROUPS = 9
IN_WIDTH = N_GROUPS * GROUP_W
D_FF = 2816
GRID_W = 64
ROPE_THETA = 10000.0
EPS = 1e-6
LOG2E = 1.4426950408889634

CHUNK = 128
FF_TILE = 256
ROW_TILE = 512
IN_ROW_TILE = 1024
PASS_A_UNROLL = 16
PASS_C_UNROLL = 16
VMEM_LIMIT = 56 * 1024 * 1024


def _dot(a, b):
    return jnp.dot(a, b, preferred_element_type=F32)


def _dot_nt(a, b):
    return lax.dot_general(a, b, (((1,), (1,)), ((), ())), preferred_element_type=F32)


def _dot_tn(a, b):
    return lax.dot_general(a, b, (((0,), (0,)), ((), ())), preferred_element_type=F32)


def _rms(x, gain):
    return x * lax.rsqrt(jnp.mean(x * x, axis=-1, keepdims=True) + EPS) * gain


def _cumsum_rows(x):
    rows = x.shape[0]
    row = lax.broadcasted_iota(jnp.int32, x.shape, 0)
    shift = 1
    while shift < rows:
        x = x + jnp.where(row >= shift, pltpu.roll(x, shift, axis=0), 0.0)
        shift *= 2
    return x


def _mod_kernel(c_ref, w_ref, b_ref, o_ref):
    c = c_ref[...]
    a = (c * jax.nn.sigmoid(c)).astype(BF16)
    o_ref[...] = _dot(a, w_ref[...].astype(BF16)) + b_ref[...]


def _modulation(c_rows, w_mod, b_mod):
    n_rows = c_rows.shape[0]
    width = w_mod.shape[1]
    tile = D_MODEL
    return pl.pallas_call(
        _mod_kernel,
        grid=(width // tile,),
        in_specs=[
            pl.BlockSpec((n_rows, D_MODEL), lambda j: (0, 0)),
            pl.BlockSpec((D_MODEL, tile), lambda j: (0, j)),
            pl.BlockSpec((1, tile), lambda j: (0, j)),
        ],
        out_specs=pl.BlockSpec((n_rows, tile), lambda j: (0, j)),
        out_shape=jax.ShapeDtypeStruct((n_rows, width), F32),
        compiler_params=pltpu.CompilerParams(dimension_semantics=("parallel",)),
        name="modulation",
    )(c_rows, w_mod, b_mod)


G_HQ, G_HV, G_ZF, G_ZB, G_HG, G_RQ, G_RK, G_RV, G_RG = range(N_GROUPS)
GROUP_ORDER = (G_RQ, G_RK, G_HQ, G_HV, G_ZF, G_ZB, G_HG, G_RV, G_RG)


def _rope(t, cos, sin_signed, first_quarter):
    swapped = jnp.where(first_quarter, pltpu.roll(t, DH - DH // 4, axis=1),
                        pltpu.roll(t, DH // 4, axis=1))
    return t * cos + swapped * sin_signed


def _modulated_norm(x, gain, scale, shift):
    return (_rms(x, gain) * (1.0 + scale) + shift).astype(BF16)


def _inproj_kernel(x_ref, sh_ref, sc_ref, g_ref, w_ref, cos_ref, sin_ref, o_ref):
    hb = _modulated_norm(x_ref[...], g_ref[...], sc_ref[0], sh_ref[0])
    cos = cos_ref[...]
    sin = sin_ref[...]
    lane = lax.broadcasted_iota(jnp.int32, cos.shape, 1)
    first_quarter = (lane % (DH // 2)) < (DH // 4)
    for g in GROUP_ORDER:
        cols = slice(g * GROUP_W, (g + 1) * GROUP_W)
        r = _dot(hb, w_ref[:, cols])
        if g == G_RK:
            r = r * (DH ** -0.5)
        for h in range(HEADS):
            t = r[:, h * DH:(h + 1) * DH]
            if g in (G_RQ, G_RK):
                t = _rope(t, cos, sin, first_quarter)
            o_ref[0, g * HEADS + h] = t.astype(BF16)


def _in_projection(x2d, mod3, seq, norm_g, w_in_bf16, cos_t, sin_t):
    n_rows = x2d.shape[0]
    row_tile = min(IN_ROW_TILE, seq)
    tiles_per_seq = seq // row_tile

    def mod_spec(chunk):
        return pl.BlockSpec((1, 1, D_MODEL), lambda i: (i // tiles_per_seq, 0, chunk))

    table = pl.BlockSpec((row_tile, DH), lambda i: (i % tiles_per_seq, 0))
    return pl.pallas_call(
        _inproj_kernel,
        grid=(n_rows // row_tile,),
        in_specs=[
            pl.BlockSpec((row_tile, D_MODEL), lambda i: (i, 0)),
            mod_spec(0),
            mod_spec(1),
            pl.BlockSpec((1, D_MODEL), lambda i: (0, 0)),
            pl.BlockSpec((D_MODEL, IN_WIDTH), lambda i: (0, 0), pipeline_mode=pl.Buffered(1)),
            table, table,
        ],
        out_specs=pl.BlockSpec((1, N_GROUPS * HEADS, row_tile, DH),
                               lambda i: (i // tiles_per_seq, 0, i % tiles_per_seq, 0)),
        out_shape=jax.ShapeDtypeStruct((n_rows // seq, N_GROUPS * HEADS, seq, DH), BF16),
        compiler_params=pltpu.CompilerParams(
            dimension_semantics=("parallel",), vmem_limit_bytes=VMEM_LIMIT),
        name="in_projection",
    )(x2d, mod3, mod3, norm_g, w_in_bf16, cos_t, sin_t)


def _forget(z, lb):
    f = lb + (1.0 - lb) * jax.nn.sigmoid(z)
    return f, jnp.log(f)


def _ctx_kernel(lg_ref, c_ref, sh_ref, sc_ref, g_ref, w_ref, lb_ref,
                sf_ref, sb_ref, rf_ref, rb_ref):
    n = c_ref.shape[1]
    hb = _modulated_norm(c_ref[0], g_ref[...], sc_ref[0], sh_ref[0])

    def group(g):
        return _dot(hb, w_ref[:, g * GROUP_W:(g + 1) * GROUP_W])

    v = group(G_HV).astype(BF16)
    ff, lff = _forget(group(G_ZF), lb_ref[0:1, :])
    fb, lfb = _forget(group(G_ZB), lb_ref[1:2, :])
    bf = _cumsum_rows(lff)
    bb = _cumsum_rows(lfb)
    kf = ((1.0 - ff) * jnp.exp(bf[n - 1:n, :] - bf)).astype(BF16)
    kb = ((1.0 - fb) * jnp.exp(bb - lfb)).astype(BF16)
    rk = group(G_RK) * (DH ** -0.5)
    rv = group(G_RV).astype(BF16)
    pos = lax.broadcasted_iota(jnp.int32, (n, DH), 0).astype(F32)
    for h in range(HEADS):
        cols = slice(h * DH, (h + 1) * DH)
        sf_ref[0, h] = _dot_tn(v[:, cols], kf[:, cols])
        sb_ref[0, h] = _dot_tn(v[:, cols], kb[:, cols])
        wf = jnp.exp((n - 1.0 - pos) * lg_ref[0, h])
        wb = jnp.exp(pos * lg_ref[1, h])
        rf_ref[0, h] = _dot_tn(rv[:, cols], (rk[:, cols] * wf).astype(BF16))
        rb_ref[0, h] = _dot_tn(rv[:, cols], (rk[:, cols] * wb).astype(BF16))


def _context_states(ctx, mod_ctx, norm_g, w_in_bf16, lb, log_gamma):
    batch, n_ctx, _ = ctx.shape

    def mod_spec(chunk):
        return pl.BlockSpec((1, 1, D_MODEL), lambda b: (0, 0, chunk))

    state = pl.BlockSpec((1, HEADS, DH, DH), lambda b: (b, 0, 0, 0))
    state_shape = jax.ShapeDtypeStruct((batch, HEADS, DH, DH), F32)
    return pl.pallas_call(
        _ctx_kernel,
        grid=(batch,),
        in_specs=[
            pl.BlockSpec(memory_space=pltpu.SMEM),
            pl.BlockSpec((1, n_ctx, D_MODEL), lambda b: (b, 0, 0)),
            mod_spec(0), mod_spec(1),
            pl.BlockSpec((1, D_MODEL), lambda b: (0, 0)),
            pl.BlockSpec((D_MODEL, IN_WIDTH), lambda b: (0, 0), pipeline_mode=pl.Buffered(1)),
            pl.BlockSpec((2, GROUP_W), lambda b: (0, 0)),
        ],
        out_specs=[state, state, state, state],
        out_shape=[state_shape] * 4,
        compiler_params=pltpu.CompilerParams(
            dimension_semantics=("parallel",), vmem_limit_bytes=VMEM_LIMIT),
        name="context_states",
    )(log_gamma, ctx, mod_ctx, mod_ctx, norm_g, w_in_bf16, lb)


def _anchor(phi, half, forward):
    rows = phi.shape[0]
    pick = half - 1 if forward else half
    block = 2 * half
    p3 = phi.reshape(rows // 8, 8, DH)
    sub = lax.broadcasted_iota(jnp.int32, p3.shape, 1)
    out = None
    for start in range(0, 8, block):
        a = jnp.broadcast_to(p3[:, start + pick:start + pick + 1, :], p3.shape)
        out = a if out is None else jnp.where(sub >= start, a, out)
    return out.reshape(rows, DH)


def _level_operands(q, kf, kb, ff, fb, phif, phib, half, row):
    n = q.shape[0]
    if half == 1:
        odd = (row & 1) != 0
        zz = q * jnp.where(odd, ff, fb)
        ww = jnp.where(odd, kb, kf)
    elif half >= 8:
        zs, ws = [], []
        for b0 in range(0, n, 2 * half):
            first = slice(b0, b0 + half)
            second = slice(b0 + half, b0 + 2 * half)
            af = phif[b0 + half - 1:b0 + half, :]
            ab = phib[b0 + half:b0 + half + 1, :]
            zs += [q[first] * jnp.exp2(phib[first] - ab), q[second] * jnp.exp2(phif[second] - af)]
            ws += [kf[first] * jnp.exp2(af - phif[first]), kb[second] * jnp.exp2(ab - phib[second])]
        zz = jnp.concatenate(zs, axis=0)
        ww = jnp.concatenate(ws, axis=0)
    else:
        af = _anchor(phif, half, True)
        ab = _anchor(phib, half, False)
        second = (row & half) != 0
        zz = q * jnp.exp2(jnp.where(second, phif - af, phib - ab))
        ww = jnp.where(second, kb, kf) * jnp.exp2(jnp.where(second, ab - phib, af - phif))
    return zz.astype(BF16), ww.astype(BF16)


def _mixer_kernel(lg_ref, q_ref, v_ref, zf_ref, zb_ref, og_ref, rq_ref, rk_ref, rv_ref, rg_ref,
                  lb_ref, hgn_ref, rn_ref,
                  sf0_ref, sb0_ref, rf0_ref, rb0_ref,
                  hgo_ref, reto_ref,
                  oh_s, or_s, qcat_s, rqcat_s, sh_s, sr_s, shb_s, srb_s, dec_s):
    head = pl.program_id(1)
    seq = q_ref.shape[2]
    n_chunks = seq // CHUNK
    lgf = lg_ref[0, head]
    lgb = lg_ref[1, head]
    lbf = lb_ref[0:1, :]
    lbb = lb_ref[1:2, :]
    fwd = slice(0, DH)
    bwd = slice(DH, 2 * DH)

    row = lax.broadcasted_iota(jnp.int32, (CHUNK, DH), 0)
    t_idx = lax.broadcasted_iota(jnp.int32, (CHUNK, CHUNK), 0)
    s_idx = lax.broadcasted_iota(jnp.int32, (CHUNK, CHUNK), 1)
    split = t_idx ^ s_idx
    rel = (t_idx - s_idx).astype(F32)
    ret_decay = jnp.where(rel >= 0, jnp.exp(rel * lgf), 0.0) + jnp.where(rel <= 0, jnp.exp(-rel * lgb), 0.0)
    pos = row.astype(F32)
    qdec_f = jnp.exp((pos + 1.0) * lgf)
    kdec_f = jnp.exp((CHUNK - 1.0 - pos) * lgf)
    qdec_b = jnp.exp((CHUNK - pos) * lgb)
    kdec_b = jnp.exp(pos * lgb)

    def pass_a(i, carry):
        rows = pl.ds(pl.multiple_of(i * CHUNK, CHUNK), CHUNK)
        q = q_ref[0, 0, rows, :].astype(F32)
        v = v_ref[0, 0, rows, :]
        ff, lff = _forget(zf_ref[0, 0, rows, :].astype(F32), lbf)
        fb, lfb = _forget(zb_ref[0, 0, rows, :].astype(F32), lbb)
        lff = lff * LOG2E
        lfb = lfb * LOG2E
        kf = 1.0 - ff
        kb = 1.0 - fb
        cf = _cumsum_rows(lff)
        cb = _cumsum_rows(lfb)
        phif = cf
        phib = lfb - cb

        attn = None
        half = CHUNK // 2
        while half >= 1:
            zz, ww = _level_operands(q, kf, kb, ff, fb, phif, phib, half, row)
            a = _dot_nt(zz, ww)
            attn = a if attn is None else jnp.where(split < 2 * half, a, attn)
            half //= 2
        attn = jnp.where(split == 0, 0.0, attn)
        diag = jnp.sum(q * (kf + kb), axis=-1, keepdims=True)
        oh_s[rows, :] = _dot(attn.astype(BF16), v) + diag * v.astype(F32)

        last_f = cf[CHUNK - 1:CHUNK, :]
        last_b = cb[CHUNK - 1:CHUNK, :]
        qcat_s[rows, fwd] = (q * jnp.exp2(cf)).astype(BF16)
        qcat_s[rows, bwd] = (q * jnp.exp2(last_b + phib)).astype(BF16)
        kcat = jnp.concatenate([(kf * jnp.exp2(last_f - cf)).astype(BF16),
                                (kb * jnp.exp2(cb - lfb)).astype(BF16)], axis=1)
        sh_s[i] = _dot_tn(v, kcat)
        dec_s[i, :, fwd] = jnp.exp2(last_f)
        dec_s[i, :, bwd] = jnp.exp2(last_b)

        rqb = rq_ref[0, 0, rows, :]
        rkb = rk_ref[0, 0, rows, :]
        rq = rqb.astype(F32)
        rk = rkb.astype(F32)
        rv = rv_ref[0, 0, rows, :]
        scores = _dot_nt(rqb, rkb) * ret_decay
        or_s[rows, :] = _dot(scores.astype(BF16), rv)
        rqcat_s[rows, fwd] = (rq * qdec_f).astype(BF16)
        rqcat_s[rows, bwd] = (rq * qdec_b).astype(BF16)
        rkcat = jnp.concatenate([(rk * kdec_f).astype(BF16), (rk * kdec_b).astype(BF16)], axis=1)
        sr_s[i] = _dot_tn(rv, rkcat)
        return carry

    lax.fori_loop(0, n_chunks, pass_a, 0, unroll=min(PASS_A_UNROLL, n_chunks))

    cdec_f = jnp.exp(CHUNK * lgf)
    cdec_b = jnp.exp(CHUNK * lgb)
    sf, rf = sf0_ref[0, 0], rf0_ref[0, 0]
    for i in range(n_chunks):
        shb_s[i, :, fwd] = sf.astype(BF16)
        srb_s[i, :, fwd] = rf.astype(BF16)
        sf = sf * dec_s[i, :, fwd] + sh_s[i, :, fwd]
        rf = rf * cdec_f + sr_s[i, :, fwd]
    sb, rb = sb0_ref[0, 0], rb0_ref[0, 0]
    for i in reversed(range(n_chunks)):
        shb_s[i, :, bwd] = sb.astype(BF16)
        srb_s[i, :, bwd] = rb.astype(BF16)
        sb = sb * dec_s[i, :, bwd] + sh_s[i, :, bwd]
        rb = rb * cdec_b + sr_s[i, :, bwd]

    def pass_c(i, carry):
        rows = pl.ds(pl.multiple_of(i * CHUNK, CHUNK), CHUNK)
        o = oh_s[rows, :] + _dot_nt(qcat_s[rows, :], shb_s[i])
        hg = o * jax.nn.sigmoid(og_ref[0, 0, rows, :].astype(F32))
        hgo_ref[0, 0, rows, :] = _rms(hg, hgn_ref[...]).astype(BF16)
        y = or_s[rows, :] + _dot_nt(rqcat_s[rows, :], srb_s[i])
        rg = rg_ref[0, 0, rows, :].astype(F32)
        ret = _rms(y, rn_ref[...]) * (rg * jax.nn.sigmoid(rg))
        reto_ref[0, 0, rows, :] = ret.astype(BF16)
        return carry

    lax.fori_loop(0, n_chunks, pass_c, 0, unroll=min(PASS_C_UNROLL, n_chunks))


def _mixer(p, lb, log_gamma, hg_norm, ret_norm, states):
    batch, _, seq, _ = p.shape
    n_chunks = seq // CHUNK

    def group(g):
        return pl.BlockSpec((1, 1, seq, DH), lambda b, h: (b, g * HEADS + h, 0, 0))

    per_head_row = pl.BlockSpec((1, DH), lambda b, h: (0, h))
    state = pl.BlockSpec((1, 1, DH, DH), lambda b, h: (b, h, 0, 0))
    out_spec = pl.BlockSpec((1, 1, seq, DH), lambda b, h: (b, h, 0, 0))
    out_shape = jax.ShapeDtypeStruct((batch, HEADS, seq, DH), BF16)
    return pl.pallas_call(
        _mixer_kernel,
        grid=(batch, HEADS),
        in_specs=[pl.BlockSpec(memory_space=pltpu.SMEM)]
        + [group(g) for g in range(N_GROUPS)]
        + [pl.BlockSpec((2, DH), lambda b, h: (0, h)), per_head_row, per_head_row,
           state, state, state, state],
        out_specs=[out_spec, out_spec],
        out_shape=[out_shape, out_shape],
        scratch_shapes=[
            pltpu.VMEM((seq, DH), F32), pltpu.VMEM((seq, DH), F32),
            pltpu.VMEM((seq, 2 * DH), BF16), pltpu.VMEM((seq, 2 * DH), BF16),
            pltpu.VMEM((n_chunks, DH, 2 * DH), F32),
            pltpu.VMEM((n_chunks, DH, 2 * DH), F32),
            pltpu.VMEM((n_chunks, DH, 2 * DH), BF16),
            pltpu.VMEM((n_chunks, DH, 2 * DH), BF16),
            pltpu.VMEM((n_chunks, 1, 2 * DH), F32),
        ],
        compiler_params=pltpu.CompilerParams(
            dimension_semantics=("parallel", "parallel"), vmem_limit_bytes=VMEM_LIMIT),
        name="token_mixer",
    )(log_gamma, p, p, p, p, p, p, p, p, p, lb, hg_norm, ret_norm, *states)


FFN_ROWS = 1024
HALO = 16


def _ffn_kernel(x_ref, xp_ref, xn_ref, hg_ref, hgp_ref, hgn_ref, rt_ref, rtp_ref, rtn_ref,
                g1_ref, sh2_ref, sc2_ref, g2_ref, woa_ref, wob_ref, n2_ref, fin_ref,
                wup_ref, cw_ref, cb_ref, wd_ref,
                o_ref, h2_s, gate2_s, gate3_s, up2_s, up3_s, acc_s):
    m = pl.program_id(1)
    rows = x_ref.shape[1]
    n_sub = rows // ROW_TILE
    n_ff = D_FF // FF_TILE

    def ext(k):
        lo = 0 if k == 0 else HALO + k * ROW_TILE
        hi = HALO + (k + 1) * ROW_TILE + (HALO if k == n_sub - 1 else 0)
        return lo, hi

    def piece(main_ref, prev_ref, next_ref, k):
        def rows_of(ref, sl):
            if len(ref.shape) == 3:
                return ref[0, sl, :]
            return jnp.concatenate([ref[0, h, sl, :] for h in range(HEADS)], axis=1)

        parts = ([rows_of(prev_ref, slice(None))] if k == 0 else []) \
            + [rows_of(main_ref, slice(k * ROW_TILE, (k + 1) * ROW_TILE))] \
            + ([rows_of(next_ref, slice(None))] if k == n_sub - 1 else [])
        return jnp.concatenate(parts, axis=0) if len(parts) > 1 else parts[0]

    for k in range(n_sub):
        lo, hi = ext(k)
        skip = HALO if k == 0 else 0
        proj = (_dot(piece(hg_ref, hgp_ref, hgn_ref, k), woa_ref[...])
                + _dot(piece(rt_ref, rtp_ref, rtn_ref, k), wob_ref[...]))
        x1 = piece(x_ref, xp_ref, xn_ref, k) + g1_ref[0] * proj
        o_ref[0, k * ROW_TILE:(k + 1) * ROW_TILE, :] = x1[skip:skip + ROW_TILE]
        h2 = _rms(x1, n2_ref[...]) * (1.0 + sc2_ref[0]) + sh2_ref[0]
        h2_s[lo:hi, :] = h2.astype(BF16)

    prev_valid = (m > 0).astype(F32)
    next_valid = (m < pl.num_programs(1) - 1).astype(F32)
    edge_lo = slice(HALO - 8, HALO)
    edge_hi = slice(HALO + rows, HALO + rows + 8)

    gate_bufs = (gate2_s, gate3_s)
    up_bufs = (up2_s, up3_s)

    def gate_up(j):
        gate_s, up_s = gate_bufs[j % 2], up_bufs[j % 2]
        w_gu = jnp.concatenate([wup_ref[:, j * FF_TILE:(j + 1) * FF_TILE],
                                wup_ref[:, D_FF + j * FF_TILE:D_FF + (j + 1) * FF_TILE]], axis=1)
        for k in range(n_sub):
            lo, hi = ext(k)
            skip = HALO if k == 0 else 0
            gu = _dot(h2_s[lo:hi, :], w_gu)
            gate_s[lo:hi, :] = gu[:, :FF_TILE]
            up_s[k * ROW_TILE:(k + 1) * ROW_TILE, :] = gu[skip:skip + ROW_TILE, FF_TILE:]
        gate_s[edge_lo, :] = gate_s[edge_lo, :] * prev_valid
        gate_s[edge_hi, :] = gate_s[edge_hi, :] * next_valid

    gate_up(0)
    for j in range(n_ff):
        if j + 1 < n_ff:
            gate_up(j + 1)
        cols = slice(j * FF_TILE, (j + 1) * FF_TILE)
        gate_s, up_s = gate_bufs[j % 2], up_bufs[j % 2]
        for k in range(n_sub):
            sub = slice(k * ROW_TILE, (k + 1) * ROW_TILE)
            base = HALO + k * ROW_TILE
            g = gate_s[base - 8:base + ROW_TILE + 8, :]
            g_prev = pltpu.roll(g, 1, axis=0)[8:8 + ROW_TILE]
            g_next = pltpu.roll(g, ROW_TILE + 15, axis=0)[8:8 + ROW_TILE]
            conv = (g_prev * cw_ref[0:1, cols] + g[8:8 + ROW_TILE] * cw_ref[1:2, cols]
                    + g_next * cw_ref[2:3, cols] + cb_ref[:, cols])
            act = (conv * jax.nn.sigmoid(conv) * up_s[sub, :]).astype(BF16)
            ffn = _dot(act, wd_ref[cols, :])
            if j > 0:
                ffn += acc_s[sub, :]
            if j < n_ff - 1:
                acc_s[sub, :] = ffn
            else:
                x2 = o_ref[0, sub, :] + g2_ref[0] * ffn
                o_ref[0, sub, :] = _rms(x2, fin_ref[...])


def _out_ffn(x, hgo, reto, mod3, w_out_bf16, norm2_g, final_g, w_up_bf16, conv_w, conv_b, w_down_bf16):
    batch, seq, _ = x.shape
    rows = min(FFN_ROWS, seq)
    n_row_tiles = seq // rows
    halo_per_tile = rows // HALO
    n_halo_blocks = seq // HALO
    once = pl.Buffered(1)

    def prev_block(m):
        return jnp.maximum(m * halo_per_tile - 1, 0)

    def next_block(m):
        return jnp.minimum((m + 1) * halo_per_tile, n_halo_blocks - 1)

    x_main = pl.BlockSpec((1, rows, D_MODEL), lambda b, m: (b, m, 0))
    x_prev = pl.BlockSpec((1, HALO, D_MODEL), lambda b, m: (b, prev_block(m), 0))
    x_next = pl.BlockSpec((1, HALO, D_MODEL), lambda b, m: (b, next_block(m), 0))
    h_main = pl.BlockSpec((1, HEADS, rows, DH), lambda b, m: (b, 0, m, 0))
    h_prev = pl.BlockSpec((1, HEADS, HALO, DH), lambda b, m: (b, 0, prev_block(m), 0))
    h_next = pl.BlockSpec((1, HEADS, HALO, DH), lambda b, m: (b, 0, next_block(m), 0))

    def mod_spec(chunk):
        return pl.BlockSpec((1, 1, D_MODEL), lambda b, m: (b, 0, chunk))

    def const(shape, index=(0, 0)):
        return pl.BlockSpec(shape, lambda b, m: index, pipeline_mode=once)

    return pl.pallas_call(
        _ffn_kernel,
        grid=(batch, n_row_tiles),
        in_specs=[
            x_main, x_prev, x_next,
            h_main, h_prev, h_next,
            h_main, h_prev, h_next,
            mod_spec(2), mod_spec(3), mod_spec(4), mod_spec(5),
            const((GROUP_W, D_MODEL), (0, 0)),
            const((GROUP_W, D_MODEL), (1, 0)),
            const((1, D_MODEL)), const((1, D_MODEL)),
            const((D_MODEL, 2 * D_FF)),
            const((3, D_FF)), const((1, D_FF)),
            const((D_FF, D_MODEL)),
        ],
        out_specs=pl.BlockSpec((1, rows, D_MODEL), lambda b, m: (b, m, 0)),
        out_shape=jax.ShapeDtypeStruct((batch, seq, D_MODEL), F32),
        scratch_shapes=[
            pltpu.VMEM((rows + 2 * HALO, D_MODEL), BF16),
            pltpu.VMEM((rows + 2 * HALO, FF_TILE), F32),
            pltpu.VMEM((rows + 2 * HALO, FF_TILE), F32),
            pltpu.VMEM((rows, FF_TILE), F32),
            pltpu.VMEM((rows, FF_TILE), F32),
            pltpu.VMEM((rows, D_MODEL), F32),
        ],
        compiler_params=pltpu.CompilerParams(
            dimension_semantics=("parallel", "parallel"), vmem_limit_bytes=VMEM_LIMIT),
        name="out_proj_ffn",
    )(x, x, x, hgo, hgo, hgo, reto, reto, reto, mod3, mod3, mod3, mod3,
      w_out_bf16, w_out_bf16, norm2_g, final_g, w_up_bf16, conv_w, conv_b, w_down_bf16)


def _rope_tables(seq):
    quarter = DH // 4
    freqs = np.float32(ROPE_THETA) ** (-np.arange(quarter, dtype=np.float32) / np.float32(quarter))
    t = np.arange(seq)
    ang_r = (t // GRID_W).astype(np.float32)[:, None] * freqs[None, :]
    ang_c = (t % GRID_W).astype(np.float32)[:, None] * freqs[None, :]
    cos = np.concatenate([np.cos(ang_r)] * 2 + [np.cos(ang_c)] * 2, axis=-1)
    sin = np.concatenate([-np.sin(ang_r), np.sin(ang_r), -np.sin(ang_c), np.sin(ang_c)], axis=-1)
    return jnp.asarray(cos, F32), jnp.asarray(sin, F32)


def kernel(x, c, ctx, c_ctx, w_mod, b_mod, norm1_g, w_in, hgrn_lb, hgrn_norm_g, ret_decay,
           ret_norm_g, w_out, norm2_g, w_up, conv_w, conv_b, w_down, final_g):
    batch, seq, _ = x.shape
    n_ctx = ctx.shape[1]
    assert w_mod.shape[0] == 1, "single-layer block"
    assert seq % CHUNK == 0 and seq % ROW_TILE == 0 and (n_ctx & (n_ctx - 1)) == 0

    lb = jnp.cumsum(jax.nn.softmax(hgrn_lb.astype(F32), axis=1), axis=1)[:, 0]
    log_gamma = jax.nn.log_sigmoid(ret_decay[0].astype(F32))
    cos_t, sin_t = _rope_tables(seq)

    n_mod_rows = -(-(batch + 1) // 8) * 8
    c_rows = jnp.zeros((n_mod_rows, D_MODEL), F32).at[:batch].set(c).at[batch].set(c_ctx)
    mod = _modulation(c_rows, w_mod[0], b_mod[0][None, :])
    mod3 = mod.reshape(n_mod_rows, 1, 6 * D_MODEL)

    w_in_b = w_in[0].astype(BF16)
    norm1 = norm1_g[0][None, :]
    states = _context_states(ctx, mod3[batch:batch + 1], norm1, w_in_b, lb, log_gamma)
    p = _in_projection(x.reshape(batch * seq, D_MODEL), mod3[:batch], seq, norm1, w_in_b, cos_t, sin_t)
    hgo, reto = _mixer(p, lb, log_gamma,
                       hgrn_norm_g[0][None, :], ret_norm_g[0][None, :], states)

    return _out_ffn(x, hgo, reto, mod3[:batch], w_out[0].astype(BF16), norm2_g[0][None, :],
                    final_g[None, :], w_up[0].astype(BF16), conv_w[0], conv_b[0][None, :],
                    w_down[0].astype(BF16))
```

```python
import jax
import jax.numpy as jnp
import numpy as np
from jax import lax
from jax.experimental import pallas as pl
from jax.experimental.pallas import tpu as pltpu

F32 = jnp.float32
BF16 = jnp.bfloat16

D_MODEL = 1024
HEADS = 4
DH = 128
GROUP_W = HEADS * DH
N_GROUPS = 9
IN_WIDTH = N_GROUPS * GROUP_W
D_FF = 2816
GRID_W = 64
ROPE_THETA = 10000.0
EPS = 1e-6
LOG2E = 1.4426950408889634

CHUNK = 128
FF_TILE = 512
ROW_TILE = 512
IN_ROW_TILE = 1024
PASS_A_UNROLL = 16
PASS_C_UNROLL = 16
VMEM_LIMIT = 56 * 1024 * 1024


def _dot(a, b):
    return jnp.dot(a, b, preferred_element_type=F32)


def _dot_nt(a, b):
    return lax.dot_general(a, b, (((1,), (1,)), ((), ())), preferred_element_type=F32)


def _dot_tn(a, b):
    return lax.dot_general(a, b, (((0,), (0,)), ((), ())), preferred_element_type=F32)


def _rms(x, gain):
    return x * lax.rsqrt(jnp.mean(x * x, axis=-1, keepdims=True) + EPS) * gain


def _cumsum_rows(x):
    rows = x.shape[0]
    row = lax.broadcasted_iota(jnp.int32, x.shape, 0)
    shift = 1
    while shift < rows:
        x = x + jnp.where(row >= shift, pltpu.roll(x, shift, axis=0), 0.0)
        shift *= 2
    return x


def _mod_kernel(c_ref, w_ref, b_ref, o_ref):
    c = c_ref[...]
    a = (c * jax.nn.sigmoid(c)).astype(BF16)
    o_ref[...] = _dot(a, w_ref[...].astype(BF16)) + b_ref[...]


def _modulation(c_rows, w_mod, b_mod):
    n_rows = c_rows.shape[0]
    width = w_mod.shape[1]
    tile = D_MODEL
    return pl.pallas_call(
        _mod_kernel,
        grid=(width // tile,),
        in_specs=[
            pl.BlockSpec((n_rows, D_MODEL), lambda j: (0, 0)),
            pl.BlockSpec((D_MODEL, tile), lambda j: (0, j)),
            pl.BlockSpec((1, tile), lambda j: (0, j)),
        ],
        out_specs=pl.BlockSpec((n_rows, tile), lambda j: (0, j)),
        out_shape=jax.ShapeDtypeStruct((n_rows, width), F32),
        compiler_params=pltpu.CompilerParams(dimension_semantics=("parallel",)),
        name="modulation",
    )(c_rows, w_mod, b_mod)


G_HQ, G_HV, G_ZF, G_ZB, G_HG, G_RQ, G_RK, G_RV, G_RG = range(N_GROUPS)
GROUP_ORDER = (G_RQ, G_RK, G_HQ, G_HV, G_ZF, G_ZB, G_HG, G_RV, G_RG)


def _rope(t, cos, sin_signed, first_quarter):
    swapped = jnp.where(first_quarter, pltpu.roll(t, DH - DH // 4, axis=1),
                        pltpu.roll(t, DH // 4, axis=1))
    return t * cos + swapped * sin_signed


def _modulated_norm(x, gain, scale, shift):
    return (_rms(x, gain) * (1.0 + scale) + shift).astype(BF16)


def _inproj_kernel(x_ref, sh_ref, sc_ref, g_ref, w_ref, cos_ref, sin_ref, o_ref):
    hb = _modulated_norm(x_ref[...], g_ref[...], sc_ref[0], sh_ref[0])
    cos = cos_ref[...]
    sin = sin_ref[...]
    lane = lax.broadcasted_iota(jnp.int32, cos.shape, 1)
    first_quarter = (lane % (DH // 2)) < (DH // 4)
    for g in GROUP_ORDER:
        cols = slice(g * GROUP_W, (g + 1) * GROUP_W)
        r = _dot(hb, w_ref[:, cols])
        if g == G_RK:
            r = r * (DH ** -0.5)
        for h in range(HEADS):
            t = r[:, h * DH:(h + 1) * DH]
            if g in (G_RQ, G_RK):
                t = _rope(t, cos, sin, first_quarter)
            o_ref[0, g * HEADS + h] = t.astype(BF16)


def _in_projection(x2d, mod3, seq, norm_g, w_in_bf16, cos_t, sin_t):
    n_rows = x2d.shape[0]
    row_tile = min(IN_ROW_TILE, seq)
    tiles_per_seq = seq // row_tile

    def mod_spec(chunk):
        return pl.BlockSpec((1, 1, D_MODEL), lambda i: (i // tiles_per_seq, 0, chunk))

    table = pl.BlockSpec((row_tile, DH), lambda i: (i % tiles_per_seq, 0))
    return pl.pallas_call(
        _inproj_kernel,
        grid=(n_rows // row_tile,),
        in_specs=[
            pl.BlockSpec((row_tile, D_MODEL), lambda i: (i, 0)),
            mod_spec(0),
            mod_spec(1),
            pl.BlockSpec((1, D_MODEL), lambda i: (0, 0)),
            pl.BlockSpec((D_MODEL, IN_WIDTH), lambda i: (0, 0), pipeline_mode=pl.Buffered(1)),
            table, table,
        ],
        out_specs=pl.BlockSpec((1, N_GROUPS * HEADS, row_tile, DH),
                               lambda i: (i // tiles_per_seq, 0, i % tiles_per_seq, 0)),
        out_shape=jax.ShapeDtypeStruct((n_rows // seq, N_GROUPS * HEADS, seq, DH), BF16),
        compiler_params=pltpu.CompilerParams(
            dimension_semantics=("parallel",), vmem_limit_bytes=VMEM_LIMIT),
        name="in_projection",
    )(x2d, mod3, mod3, norm_g, w_in_bf16, cos_t, sin_t)


def _forget(z, lb):
    f = lb + (1.0 - lb) * jax.nn.sigmoid(z)
    return f, jnp.log(f)


def _ctx_kernel(lg_ref, c_ref, sh_ref, sc_ref, g_ref, w_ref, lb_ref,
                sf_ref, sb_ref, rf_ref, rb_ref):
    n = c_ref.shape[1]
    hb = _modulated_norm(c_ref[0], g_ref[...], sc_ref[0], sh_ref[0])

    def group(g):
        return _dot(hb, w_ref[:, g * GROUP_W:(g + 1) * GROUP_W])

    v = group(G_HV).astype(BF16)
    ff, lff = _forget(group(G_ZF), lb_ref[0:1, :])
    fb, lfb = _forget(group(G_ZB), lb_ref[1:2, :])
    bf = _cumsum_rows(lff)
    bb = _cumsum_rows(lfb)
    kf = ((1.0 - ff) * jnp.exp(bf[n - 1:n, :] - bf)).astype(BF16)
    kb = ((1.0 - fb) * jnp.exp(bb - lfb)).astype(BF16)
    rk = group(G_RK) * (DH ** -0.5)
    rv = group(G_RV).astype(BF16)
    pos = lax.broadcasted_iota(jnp.int32, (n, DH), 0).astype(F32)
    for h in range(HEADS):
        cols = slice(h * DH, (h + 1) * DH)
        sf_ref[0, h] = _dot_tn(v[:, cols], kf[:, cols])
        sb_ref[0, h] = _dot_tn(v[:, cols], kb[:, cols])
        wf = jnp.exp((n - 1.0 - pos) * lg_ref[0, h])
        wb = jnp.exp(pos * lg_ref[1, h])
        rf_ref[0, h] = _dot_tn(rv[:, cols], (rk[:, cols] * wf).astype(BF16))
        rb_ref[0, h] = _dot_tn(rv[:, cols], (rk[:, cols] * wb).astype(BF16))


def _context_states(ctx, mod_ctx, norm_g, w_in_bf16, lb, log_gamma):
    batch, n_ctx, _ = ctx.shape

    def mod_spec(chunk):
        return pl.BlockSpec((1, 1, D_MODEL), lambda b: (0, 0, chunk))

    state = pl.BlockSpec((1, HEADS, DH, DH), lambda b: (b, 0, 0, 0))
    state_shape = jax.ShapeDtypeStruct((batch, HEADS, DH, DH), F32)
    return pl.pallas_call(
        _ctx_kernel,
        grid=(batch,),
        in_specs=[
            pl.BlockSpec(memory_space=pltpu.SMEM),
            pl.BlockSpec((1, n_ctx, D_MODEL), lambda b: (b, 0, 0)),
            mod_spec(0), mod_spec(1),
            pl.BlockSpec((1, D_MODEL), lambda b: (0, 0)),
            pl.BlockSpec((D_MODEL, IN_WIDTH), lambda b: (0, 0), pipeline_mode=pl.Buffered(1)),
            pl.BlockSpec((2, GROUP_W), lambda b: (0, 0)),
        ],
        out_specs=[state, state, state, state],
        out_shape=[state_shape] * 4,
        compiler_params=pltpu.CompilerParams(
            dimension_semantics=("parallel",), vmem_limit_bytes=VMEM_LIMIT),
        name="context_states",
    )(log_gamma, ctx, mod_ctx, mod_ctx, norm_g, w_in_bf16, lb)


def _anchor(phi, half, forward):
    rows = phi.shape[0]
    pick = half - 1 if forward else half
    block = 2 * half
    p3 = phi.reshape(rows // 8, 8, DH)
    sub = lax.broadcasted_iota(jnp.int32, p3.shape, 1)
    out = None
    for start in range(0, 8, block):
        a = jnp.broadcast_to(p3[:, start + pick:start + pick + 1, :], p3.shape)
        out = a if out is None else jnp.where(sub >= start, a, out)
    return out.reshape(rows, DH)


def _level_operands(q, kf, kb, ff, fb, phif, phib, half, row):
    n = q.shape[0]
    if half == 1:
        odd = (row & 1) != 0
        zz = q * jnp.where(odd, ff, fb)
        ww = jnp.where(odd, kb, kf)
    elif half >= 8:
        zs, ws = [], []
        for b0 in range(0, n, 2 * half):
            first = slice(b0, b0 + half)
            second = slice(b0 + half, b0 + 2 * half)
            af = phif[b0 + half - 1:b0 + half, :]
            ab = phib[b0 + half:b0 + half + 1, :]
            zs += [q[first] * jnp.exp2(phib[first] - ab), q[second] * jnp.exp2(phif[second] - af)]
            ws += [kf[first] * jnp.exp2(af - phif[first]), kb[second] * jnp.exp2(ab - phib[second])]
        zz = jnp.concatenate(zs, axis=0)
        ww = jnp.concatenate(ws, axis=0)
    else:
        af = _anchor(phif, half, True)
        ab = _anchor(phib, half, False)
        second = (row & half) != 0
        zz = q * jnp.exp2(jnp.where(second, phif - af, phib - ab))
        ww = jnp.where(second, kb, kf) * jnp.exp2(jnp.where(second, ab - phib, af - phif))
    return zz.astype(BF16), ww.astype(BF16)


def _mixer_kernel(lg_ref, q_ref, v_ref, zf_ref, zb_ref, og_ref, rq_ref, rk_ref, rv_ref, rg_ref,
                  lb_ref, hgn_ref, rn_ref,
                  sf0_ref, sb0_ref, rf0_ref, rb0_ref,
                  hgo_ref, reto_ref,
                  oh_s, or_s, qcat_s, rqcat_s, sh_s, sr_s, shb_s, srb_s, dec_s):
    head = pl.program_id(1)
    seq = q_ref.shape[2]
    n_chunks = seq // CHUNK
    lgf = lg_ref[0, head]
    lgb = lg_ref[1, head]
    lbf = lb_ref[0:1, :]
    lbb = lb_ref[1:2, :]
    fwd = slice(0, DH)
    bwd = slice(DH, 2 * DH)

    row = lax.broadcasted_iota(jnp.int32, (CHUNK, DH), 0)
    t_idx = lax.broadcasted_iota(jnp.int32, (CHUNK, CHUNK), 0)
    s_idx = lax.broadcasted_iota(jnp.int32, (CHUNK, CHUNK), 1)
    split = t_idx ^ s_idx
    rel = (t_idx - s_idx).astype(F32)
    ret_decay = jnp.where(rel >= 0, jnp.exp(rel * lgf), 0.0) + jnp.where(rel <= 0, jnp.exp(-rel * lgb), 0.0)
    pos = row.astype(F32)
    qdec_f = jnp.exp((pos + 1.0) * lgf)
    kdec_f = jnp.exp((CHUNK - 1.0 - pos) * lgf)
    qdec_b = jnp.exp((CHUNK - pos) * lgb)
    kdec_b = jnp.exp(pos * lgb)

    def pass_a(i, carry):
        rows = pl.ds(pl.multiple_of(i * CHUNK, CHUNK), CHUNK)
        q = q_ref[0, 0, rows, :].astype(F32)
        v = v_ref[0, 0, rows, :]
        ff, lff = _forget(zf_ref[0, 0, rows, :].astype(F32), lbf)
        fb, lfb = _forget(zb_ref[0, 0, rows, :].astype(F32), lbb)
        lff = lff * LOG2E
        lfb = lfb * LOG2E
        kf = 1.0 - ff
        kb = 1.0 - fb
        cf = _cumsum_rows(lff)
        cb = _cumsum_rows(lfb)
        phif = cf
        phib = lfb - cb

        attn = None
        half = CHUNK // 2
        while half >= 1:
            zz, ww = _level_operands(q, kf, kb, ff, fb, phif, phib, half, row)
            a = _dot_nt(zz, ww)
            attn = a if attn is None else jnp.where(split < 2 * half, a, attn)
            half //= 2
        attn = jnp.where(split == 0, 0.0, attn)
        diag = jnp.sum(q * (kf + kb), axis=-1, keepdims=True)
        oh_s[rows, :] = _dot(attn.astype(BF16), v) + diag * v.astype(F32)

        last_f = cf[CHUNK - 1:CHUNK, :]
        last_b = cb[CHUNK - 1:CHUNK, :]
        qcat_s[rows, fwd] = (q * jnp.exp2(cf)).astype(BF16)
        qcat_s[rows, bwd] = (q * jnp.exp2(last_b + phib)).astype(BF16)
        kcat = jnp.concatenate([(kf * jnp.exp2(last_f - cf)).astype(BF16),
                                (kb * jnp.exp2(cb - lfb)).astype(BF16)], axis=1)
        sh_s[i] = _dot_tn(v, kcat)
        dec_s[i, :, fwd] = jnp.exp2(last_f)
        dec_s[i, :, bwd] = jnp.exp2(last_b)

        rqb = rq_ref[0, 0, rows, :]
        rkb = rk_ref[0, 0, rows, :]
        rq = rqb.astype(F32)
        rk = rkb.astype(F32)
        rv = rv_ref[0, 0, rows, :]
        scores = _dot_nt(rqb, rkb) * ret_decay
        or_s[rows, :] = _dot(scores.astype(BF16), rv)
        rqcat_s[rows, fwd] = (rq * qdec_f).astype(BF16)
        rqcat_s[rows, bwd] = (rq * qdec_b).astype(BF16)
        rkcat = jnp.concatenate([(rk * kdec_f).astype(BF16), (rk * kdec_b).astype(BF16)], axis=1)
        sr_s[i] = _dot_tn(rv, rkcat)
        return carry

    lax.fori_loop(0, n_chunks, pass_a, 0, unroll=min(PASS_A_UNROLL, n_chunks))

    cdec_f = jnp.exp(CHUNK * lgf)
    cdec_b = jnp.exp(CHUNK * lgb)
    sf, rf = sf0_ref[0, 0], rf0_ref[0, 0]
    for i in range(n_chunks):
        shb_s[i, :, fwd] = sf.astype(BF16)
        srb_s[i, :, fwd] = rf.astype(BF16)
        sf = sf * dec_s[i, :, fwd] + sh_s[i, :, fwd]
        rf = rf * cdec_f + sr_s[i, :, fwd]
    sb, rb = sb0_ref[0, 0], rb0_ref[0, 0]
    for i in reversed(range(n_chunks)):
        shb_s[i, :, bwd] = sb.astype(BF16)
        srb_s[i, :, bwd] = rb.astype(BF16)
        sb = sb * dec_s[i, :, bwd] + sh_s[i, :, bwd]
        rb = rb * cdec_b + sr_s[i, :, bwd]

    def pass_c(i, carry):
        rows = pl.ds(pl.multiple_of(i * CHUNK, CHUNK), CHUNK)
        o = oh_s[rows, :] + _dot_nt(qcat_s[rows, :], shb_s[i])
        hg = o * jax.nn.sigmoid(og_ref[0, 0, rows, :].astype(F32))
        hgo_ref[0, 0, rows, :] = _rms(hg, hgn_ref[...]).astype(BF16)
        y = or_s[rows, :] + _dot_nt(rqcat_s[rows, :], srb_s[i])
        rg = rg_ref[0, 0, rows, :].astype(F32)
        ret = _rms(y, rn_ref[...]) * (rg * jax.nn.sigmoid(rg))
        reto_ref[0, 0, rows, :] = ret.astype(BF16)
        return carry

    lax.fori_loop(0, n_chunks, pass_c, 0, unroll=min(PASS_C_UNROLL, n_chunks))


def _mixer(p, lb, log_gamma, hg_norm, ret_norm, states):
    batch, _, seq, _ = p.shape
    n_chunks = seq // CHUNK

    def group(g):
        return pl.BlockSpec((1, 1, seq, DH), lambda b, h: (b, g * HEADS + h, 0, 0))

    per_head_row = pl.BlockSpec((1, DH), lambda b, h: (0, h))
    state = pl.BlockSpec((1, 1, DH, DH), lambda b, h: (b, h, 0, 0))
    out_spec = pl.BlockSpec((1, 1, seq, DH), lambda b, h: (b, h, 0, 0))
    out_shape = jax.ShapeDtypeStruct((batch, HEADS, seq, DH), BF16)
    return pl.pallas_call(
        _mixer_kernel,
        grid=(batch, HEADS),
        in_specs=[pl.BlockSpec(memory_space=pltpu.SMEM)]
        + [group(g) for g in range(N_GROUPS)]
        + [pl.BlockSpec((2, DH), lambda b, h: (0, h)), per_head_row, per_head_row,
           state, state, state, state],
        out_specs=[out_spec, out_spec],
        out_shape=[out_shape, out_shape],
        scratch_shapes=[
            pltpu.VMEM((seq, DH), F32), pltpu.VMEM((seq, DH), F32),
            pltpu.VMEM((seq, 2 * DH), BF16), pltpu.VMEM((seq, 2 * DH), BF16),
            pltpu.VMEM((n_chunks, DH, 2 * DH), F32),
            pltpu.VMEM((n_chunks, DH, 2 * DH), F32),
            pltpu.VMEM((n_chunks, DH, 2 * DH), BF16),
            pltpu.VMEM((n_chunks, DH, 2 * DH), BF16),
            pltpu.VMEM((n_chunks, 1, 2 * DH), F32),
        ],
        compiler_params=pltpu.CompilerParams(
            dimension_semantics=("parallel", "parallel"), vmem_limit_bytes=VMEM_LIMIT),
        name="token_mixer",
    )(log_gamma, p, p, p, p, p, p, p, p, p, lb, hg_norm, ret_norm, *states)


FFN_ROWS = 1024
HALO = 16


def _ffn_kernel(x_ref, xp_ref, xn_ref, hg_ref, hgp_ref, hgn_ref, rt_ref, rtp_ref, rtn_ref,
                g1_ref, sh2_ref, sc2_ref, g2_ref, woa_ref, wob_ref, n2_ref, fin_ref,
                wup_ref, cw_ref, cb_ref, wd_ref,
                o_ref, h2_s, gate2_s, gate3_s, up2_s, up3_s, acc_s):
    m = pl.program_id(1)
    rows = x_ref.shape[1]
    n_sub = rows // ROW_TILE
    def ext(k):
        lo = 0 if k == 0 else HALO + k * ROW_TILE
        hi = HALO + (k + 1) * ROW_TILE + (HALO if k == n_sub - 1 else 0)
        return lo, hi

    def piece(main_ref, prev_ref, next_ref, k):
        def rows_of(ref, sl):
            if len(ref.shape) == 3:
                return ref[0, sl, :]
            return jnp.concatenate([ref[0, h, sl, :] for h in range(HEADS)], axis=1)

        parts = ([rows_of(prev_ref, slice(None))] if k == 0 else []) \
            + [rows_of(main_ref, slice(k * ROW_TILE, (k + 1) * ROW_TILE))] \
            + ([rows_of(next_ref, slice(None))] if k == n_sub - 1 else [])
        return jnp.concatenate(parts, axis=0) if len(parts) > 1 else parts[0]

    for k in range(n_sub):
        lo, hi = ext(k)
        skip = HALO if k == 0 else 0
        proj = (_dot(piece(hg_ref, hgp_ref, hgn_ref, k), woa_ref[...])
                + _dot(piece(rt_ref, rtp_ref, rtn_ref, k), wob_ref[...]))
        x1 = piece(x_ref, xp_ref, xn_ref, k) + g1_ref[0] * proj
        o_ref[0, k * ROW_TILE:(k + 1) * ROW_TILE, :] = x1[skip:skip + ROW_TILE]
        h2 = _rms(x1, n2_ref[...]) * (1.0 + sc2_ref[0]) + sh2_ref[0]
        h2_s[lo:hi, :] = h2.astype(BF16)

    prev_valid = (m > 0).astype(F32)
    next_valid = (m < pl.num_programs(1) - 1).astype(F32)
    edge_lo = slice(HALO - 8, HALO)
    edge_hi = slice(HALO + rows, HALO + rows + 8)

    gate_bufs = (gate2_s, gate3_s)
    up_bufs = (up2_s, up3_s)
    tiles = [(c0, min(FF_TILE, D_FF - c0)) for c0 in range(0, D_FF, FF_TILE)]

    def gate_up(j):
        c0, width = tiles[j]
        gate_s, up_s = gate_bufs[j % 2], up_bufs[j % 2]
        w_gu = jnp.concatenate([wup_ref[:, c0:c0 + width], wup_ref[:, D_FF + c0:D_FF + c0 + width]], axis=1)
        for k in range(n_sub):
            lo, hi = ext(k)
            skip = HALO if k == 0 else 0
            gu = _dot(h2_s[lo:hi, :], w_gu)
            gate_s[lo:hi, :width] = gu[:, :width]
            up_s[k * ROW_TILE:(k + 1) * ROW_TILE, :width] = gu[skip:skip + ROW_TILE, width:]
        gate_s[edge_lo, :width] = gate_s[edge_lo, :width] * prev_valid
        gate_s[edge_hi, :width] = gate_s[edge_hi, :width] * next_valid

    gate_up(0)
    for j, (c0, width) in enumerate(tiles):
        if j + 1 < len(tiles):
            gate_up(j + 1)
        cols = slice(c0, c0 + width)
        gate_s, up_s = gate_bufs[j % 2], up_bufs[j % 2]
        for k in range(n_sub):
            sub = slice(k * ROW_TILE, (k + 1) * ROW_TILE)
            base = HALO + k * ROW_TILE
            g = gate_s[base - 8:base + ROW_TILE + 8, :width]
            g_prev = pltpu.roll(g, 1, axis=0)[8:8 + ROW_TILE]
            g_next = pltpu.roll(g, ROW_TILE + 15, axis=0)[8:8 + ROW_TILE]
            conv = (g_prev * cw_ref[0:1, cols] + g[8:8 + ROW_TILE] * cw_ref[1:2, cols]
                    + g_next * cw_ref[2:3, cols] + cb_ref[:, cols])
            act = (conv * jax.nn.sigmoid(conv) * up_s[sub, :width]).astype(BF16)
            ffn = _dot(act, wd_ref[cols, :])
            if j > 0:
                ffn += acc_s[sub, :]
            if j < len(tiles) - 1:
                acc_s[sub, :] = ffn
            else:
                x2 = o_ref[0, sub, :] + g2_ref[0] * ffn
                o_ref[0, sub, :] = _rms(x2, fin_ref[...])


def _out_ffn(x, hgo, reto, mod3, w_out_bf16, norm2_g, final_g, w_up_bf16, conv_w, conv_b, w_down_bf16):
    batch, seq, _ = x.shape
    rows = min(FFN_ROWS, seq)
    n_row_tiles = seq // rows
    halo_per_tile = rows // HALO
    n_halo_blocks = seq // HALO
    once = pl.Buffered(1)

    def prev_block(m):
        return jnp.maximum(m * halo_per_tile - 1, 0)

    def next_block(m):
        return jnp.minimum((m + 1) * halo_per_tile, n_halo_blocks - 1)

    x_main = pl.BlockSpec((1, rows, D_MODEL), lambda b, m: (b, m, 0))
    x_prev = pl.BlockSpec((1, HALO, D_MODEL), lambda b, m: (b, prev_block(m), 0))
    x_next = pl.BlockSpec((1, HALO, D_MODEL), lambda b, m: (b, next_block(m), 0))
    h_main = pl.BlockSpec((1, HEADS, rows, DH), lambda b, m: (b, 0, m, 0))
    h_prev = pl.BlockSpec((1, HEADS, HALO, DH), lambda b, m: (b, 0, prev_block(m), 0))
    h_next = pl.BlockSpec((1, HEADS, HALO, DH), lambda b, m: (b, 0, next_block(m), 0))

    def mod_spec(chunk):
        return pl.BlockSpec((1, 1, D_MODEL), lambda b, m: (b, 0, chunk))

    def const(shape, index=(0, 0)):
        return pl.BlockSpec(shape, lambda b, m: index, pipeline_mode=once)

    return pl.pallas_call(
        _ffn_kernel,
        grid=(batch, n_row_tiles),
        in_specs=[
            x_main, x_prev, x_next,
            h_main, h_prev, h_next,
            h_main, h_prev, h_next,
            mod_spec(2), mod_spec(3), mod_spec(4), mod_spec(5),
            const((GROUP_W, D_MODEL), (0, 0)),
            const((GROUP_W, D_MODEL), (1, 0)),
            const((1, D_MODEL)), const((1, D_MODEL)),
            const((D_MODEL, 2 * D_FF)),
            const((3, D_FF)), const((1, D_FF)),
            const((D_FF, D_MODEL)),
        ],
        out_specs=pl.BlockSpec((1, rows, D_MODEL), lambda b, m: (b, m, 0)),
        out_shape=jax.ShapeDtypeStruct((batch, seq, D_MODEL), F32),
        scratch_shapes=[
            pltpu.VMEM((rows + 2 * HALO, D_MODEL), BF16),
            pltpu.VMEM((rows + 2 * HALO, FF_TILE), F32),
            pltpu.VMEM((rows + 2 * HALO, FF_TILE), F32),
            pltpu.VMEM((rows, FF_TILE), F32),
            pltpu.VMEM((rows, FF_TILE), F32),
            pltpu.VMEM((rows, D_MODEL), F32),
        ],
        compiler_params=pltpu.CompilerParams(
            dimension_semantics=("parallel", "parallel"), vmem_limit_bytes=VMEM_LIMIT),
        name="out_proj_ffn",
    )(x, x, x, hgo, hgo, hgo, reto, reto, reto, mod3, mod3, mod3, mod3,
      w_out_bf16, w_out_bf16, norm2_g, final_g, w_up_bf16, conv_w, conv_b, w_down_bf16)


def _rope_tables(seq):
    quarter = DH // 4
    freqs = np.float32(ROPE_THETA) ** (-np.arange(quarter, dtype=np.float32) / np.float32(quarter))
    t = np.arange(seq)
    ang_r = (t // GRID_W).astype(np.float32)[:, None] * freqs[None, :]
    ang_c = (t % GRID_W).astype(np.float32)[:, None] * freqs[None, :]
    cos = np.concatenate([np.cos(ang_r)] * 2 + [np.cos(ang_c)] * 2, axis=-1)
    sin = np.concatenate([-np.sin(ang_r), np.sin(ang_r), -np.sin(ang_c), np.sin(ang_c)], axis=-1)
    return jnp.asarray(cos, F32), jnp.asarray(sin, F32)


def kernel(x, c, ctx, c_ctx, w_mod, b_mod, norm1_g, w_in, hgrn_lb, hgrn_norm_g, ret_decay,
           ret_norm_g, w_out, norm2_g, w_up, conv_w, conv_b, w_down, final_g):
    batch, seq, _ = x.shape
    n_ctx = ctx.shape[1]
    assert w_mod.shape[0] == 1, "single-layer block"
    assert seq % CHUNK == 0 and seq % ROW_TILE == 0 and (n_ctx & (n_ctx - 1)) == 0

    lb = jnp.cumsum(jax.nn.softmax(hgrn_lb.astype(F32), axis=1), axis=1)[:, 0]
    log_gamma = jax.nn.log_sigmoid(ret_decay[0].astype(F32))
    cos_t, sin_t = _rope_tables(seq)

    n_mod_rows = -(-(batch + 1) // 8) * 8
    c_rows = jnp.zeros((n_mod_rows, D_MODEL), F32).at[:batch].set(c).at[batch].set(c_ctx)
    mod = _modulation(c_rows, w_mod[0], b_mod[0][None, :])
    mod3 = mod.reshape(n_mod_rows, 1, 6 * D_MODEL)

    w_in_b = w_in[0].astype(BF16)
    norm1 = norm1_g[0][None, :]
    states = _context_states(ctx, mod3[batch:batch + 1], norm1, w_in_b, lb, log_gamma)
    p = _in_projection(x.reshape(batch * seq, D_MODEL), mod3[:batch], seq, norm1, w_in_b, cos_t, sin_t)
    hgo, reto = _mixer(p, lb, log_gamma,
                       hgrn_norm_g[0][None, :], ret_norm_g[0][None, :], states)

    return _out_ffn(x, hgo, reto, mod3[:batch], w_out[0].astype(BF16), norm2_g[0][None, :],
                    final_g[None, :], w_up[0].astype(BF16), conv_w[0], conv_b[0][None, :],
                    w_down[0].astype(BF16))
```

```python
import jax
import jax.numpy as jnp
import numpy as np
from jax import lax
from jax.experimental import pallas as pl
from jax.experimental.pallas import tpu as pltpu

F32 = jnp.float32
BF16 = jnp.bfloat16

D_MODEL = 1024
HEADS = 4
DH = 128
GROUP_W = HEADS * DH
N_GROUPS = 9
IN_WIDTH = N_GROUPS * GROUP_W
D_FF = 2816
GRID_W = 64
ROPE_THETA = 10000.0
EPS = 1e-6
LOG2E = 1.4426950408889634

CHUNK = 128
FF_TILE = 256
ROW_TILE = 512
IN_ROW_TILE = 1024
PASS_A_UNROLL = 16
PASS_C_UNROLL = 16
VMEM_LIMIT = 56 * 1024 * 1024


def _dot(a, b):
    return jnp.dot(a, b, preferred_element_type=F32)


def _dot_nt(a, b):
    return lax.dot_general(a, b, (((1,), (1,)), ((), ())), preferred_element_type=F32)


def _dot_tn(a, b):
    return lax.dot_general(a, b, (((0,), (0,)), ((), ())), preferred_element_type=F32)


def _rms(x, gain):
    return x * lax.rsqrt(jnp.mean(x * x, axis=-1, keepdims=True) + EPS) * gain


def _cumsum_rows(x):
    rows = x.shape[0]
    row = lax.broadcasted_iota(jnp.int32, x.shape, 0)
    shift = 1
    while shift < rows:
        x = x + jnp.where(row >= shift, pltpu.roll(x, shift, axis=0), 0.0)
        shift *= 2
    return x


def _mod_kernel(c_ref, w_ref, b_ref, o_ref):
    c = c_ref[...]
    a = (c * jax.nn.sigmoid(c)).astype(BF16)
    o_ref[...] = _dot(a, w_ref[...].astype(BF16)) + b_ref[...]


def _modulation(c_rows, w_mod, b_mod):
    n_rows = c_rows.shape[0]
    width = w_mod.shape[1]
    tile = D_MODEL
    return pl.pallas_call(
        _mod_kernel,
        grid=(width // tile,),
        in_specs=[
            pl.BlockSpec((n_rows, D_MODEL), lambda j: (0, 0)),
            pl.BlockSpec((D_MODEL, tile), lambda j: (0, j)),
            pl.BlockSpec((1, tile), lambda j: (0, j)),
        ],
        out_specs=pl.BlockSpec((n_rows, tile), lambda j: (0, j)),
        out_shape=jax.ShapeDtypeStruct((n_rows, width), F32),
        compiler_params=pltpu.CompilerParams(dimension_semantics=("parallel",)),
        name="modulation",
    )(c_rows, w_mod, b_mod)


G_HQ, G_HV, G_ZF, G_ZB, G_HG, G_RQ, G_RK, G_RV, G_RG = range(N_GROUPS)
GROUP_ORDER = (G_ZF, G_ZB, G_RQ, G_RK, G_HQ, G_HV, G_HG, G_RV, G_RG)


def _rope(t, cos, sin_signed, first_quarter):
    swapped = jnp.where(first_quarter, pltpu.roll(t, DH - DH // 4, axis=1),
                        pltpu.roll(t, DH // 4, axis=1))
    return t * cos + swapped * sin_signed


def _modulated_norm(x, gain, scale, shift):
    return (_rms(x, gain) * (1.0 + scale) + shift).astype(BF16)


def _forget(z, lb):
    f = lb + (1.0 - lb) * jax.nn.sigmoid(z)
    return f, jnp.log(f)


def _inproj_kernel(x_ref, sh_ref, sc_ref, g_ref, w_ref, cos_ref, sin_ref, lb_ref, o_ref, phi_ref, zf_s, zb_s):
    hb = _modulated_norm(x_ref[...], g_ref[...], sc_ref[0], sh_ref[0])
    n_rows = hb.shape[0]
    cos = cos_ref[...]
    sin = sin_ref[...]
    lane = lax.broadcasted_iota(jnp.int32, cos.shape, 1)
    first_quarter = (lane % (DH // 2)) < (DH // 4)

    def gate_tile(backward, c0, h, zero):
        hcols = slice(h * DH, (h + 1) * DH)
        z_s, lb, g = (zb_s, lb_ref[1:2, hcols], G_ZB) if backward else (zf_s, lb_ref[0:1, hcols], G_ZF)
        f, lf = _forget(z_s[c0:c0 + CHUNK, hcols], lb + zero)
        lf = lf * LOG2E
        o_ref[0, g * HEADS + h, c0:c0 + CHUNK, :] = (1.0 - f).astype(BF16)
        csum = _cumsum_rows(lf)
        if backward:
            csum = csum[CHUNK - 1:CHUNK, :] - csum + lf
        phi_ref[0, (HEADS if backward else 0) + h, c0:c0 + CHUNK, :] = csum

    zf_s[...] = _dot(hb, w_ref[:, G_ZF * GROUP_W:(G_ZF + 1) * GROUP_W])
    gate_work = [(d, c0, h) for d in (False, True) for c0 in range(0, n_rows, CHUNK) for h in range(HEADS)]
    hosts = [g for g in GROUP_ORDER if g != G_ZF]
    per_host = -(-len(gate_work) // len(hosts))
    for idx, g in enumerate(hosts):
        cols = slice(g * GROUP_W, (g + 1) * GROUP_W)
        r = _dot(hb, w_ref[:, cols])
        if g == G_ZB:
            zb_s[...] = r
        share = gate_work[idx * per_host:(idx + 1) * per_host]
        for t, item in enumerate(share):
            r0 = (t * n_rows // len(share)) // 8 * 8
            bits = pltpu.bitcast(r[r0:r0 + 8, 0:DH], jnp.uint32)
            gate_tile(*item, ((bits >> 16) >> 16).astype(F32)[0:1, :])
        if g == G_ZB:
            continue
        if g == G_RK:
            r = r * (DH ** -0.5)
        for h in range(HEADS):
            t = r[:, h * DH:(h + 1) * DH]
            if g in (G_RQ, G_RK):
                t = _rope(t, cos, sin, first_quarter)
            o_ref[0, g * HEADS + h] = t.astype(BF16)


def _in_projection(x2d, mod3, seq, norm_g, w_in_bf16, cos_t, sin_t, lb):
    n_rows = x2d.shape[0]
    row_tile = min(IN_ROW_TILE, seq)
    tiles_per_seq = seq // row_tile

    def mod_spec(chunk):
        return pl.BlockSpec((1, 1, D_MODEL), lambda i: (i // tiles_per_seq, 0, chunk))

    table = pl.BlockSpec((row_tile, DH), lambda i: (i % tiles_per_seq, 0))
    return pl.pallas_call(
        _inproj_kernel,
        grid=(n_rows // row_tile,),
        in_specs=[
            pl.BlockSpec((row_tile, D_MODEL), lambda i: (i, 0)),
            mod_spec(0),
            mod_spec(1),
            pl.BlockSpec((1, D_MODEL), lambda i: (0, 0)),
            pl.BlockSpec((D_MODEL, IN_WIDTH), lambda i: (0, 0), pipeline_mode=pl.Buffered(1)),
            table, table,
            pl.BlockSpec((2, GROUP_W), lambda i: (0, 0)),
        ],
        out_specs=[pl.BlockSpec((1, N_GROUPS * HEADS, row_tile, DH),
                                lambda i: (i // tiles_per_seq, 0, i % tiles_per_seq, 0)),
                   pl.BlockSpec((1, 2 * HEADS, row_tile, DH),
                                lambda i: (i // tiles_per_seq, 0, i % tiles_per_seq, 0))],
        out_shape=[jax.ShapeDtypeStruct((n_rows // seq, N_GROUPS * HEADS, seq, DH), BF16),
                   jax.ShapeDtypeStruct((n_rows // seq, 2 * HEADS, seq, DH), F32)],
        scratch_shapes=[pltpu.VMEM((row_tile, GROUP_W), F32), pltpu.VMEM((row_tile, GROUP_W), F32)],
        compiler_params=pltpu.CompilerParams(
            dimension_semantics=("parallel",), vmem_limit_bytes=VMEM_LIMIT),
        name="in_projection",
    )(x2d, mod3, mod3, norm_g, w_in_bf16, cos_t, sin_t, lb)


def _ctx_kernel(lg_ref, c_ref, sh_ref, sc_ref, g_ref, w_ref, lb_ref,
                sf_ref, sb_ref, rf_ref, rb_ref):
    n = c_ref.shape[1]
    hb = _modulated_norm(c_ref[0], g_ref[...], sc_ref[0], sh_ref[0])

    def group(g):
        return _dot(hb, w_ref[:, g * GROUP_W:(g + 1) * GROUP_W])

    v = group(G_HV).astype(BF16)
    ff, lff = _forget(group(G_ZF), lb_ref[0:1, :])
    fb, lfb = _forget(group(G_ZB), lb_ref[1:2, :])
    bf = _cumsum_rows(lff)
    bb = _cumsum_rows(lfb)
    kf = ((1.0 - ff) * jnp.exp(bf[n - 1:n, :] - bf)).astype(BF16)
    kb = ((1.0 - fb) * jnp.exp(bb - lfb)).astype(BF16)
    rk = group(G_RK) * (DH ** -0.5)
    rv = group(G_RV).astype(BF16)
    pos = lax.broadcasted_iota(jnp.int32, (n, DH), 0).astype(F32)
    for h in range(HEADS):
        cols = slice(h * DH, (h + 1) * DH)
        sf_ref[0, h] = _dot_tn(v[:, cols], kf[:, cols])
        sb_ref[0, h] = _dot_tn(v[:, cols], kb[:, cols])
        wf = jnp.exp((n - 1.0 - pos) * lg_ref[0, h])
        wb = jnp.exp(pos * lg_ref[1, h])
        rf_ref[0, h] = _dot_tn(rv[:, cols], (rk[:, cols] * wf).astype(BF16))
        rb_ref[0, h] = _dot_tn(rv[:, cols], (rk[:, cols] * wb).astype(BF16))


def _context_states(ctx, mod_ctx, norm_g, w_in_bf16, lb, log_gamma):
    batch, n_ctx, _ = ctx.shape

    def mod_spec(chunk):
        return pl.BlockSpec((1, 1, D_MODEL), lambda b: (0, 0, chunk))

    state = pl.BlockSpec((1, HEADS, DH, DH), lambda b: (b, 0, 0, 0))
    state_shape = jax.ShapeDtypeStruct((batch, HEADS, DH, DH), F32)
    return pl.pallas_call(
        _ctx_kernel,
        grid=(batch,),
        in_specs=[
            pl.BlockSpec(memory_space=pltpu.SMEM),
            pl.BlockSpec((1, n_ctx, D_MODEL), lambda b: (b, 0, 0)),
            mod_spec(0), mod_spec(1),
            pl.BlockSpec((1, D_MODEL), lambda b: (0, 0)),
            pl.BlockSpec((D_MODEL, IN_WIDTH), lambda b: (0, 0), pipeline_mode=pl.Buffered(1)),
            pl.BlockSpec((2, GROUP_W), lambda b: (0, 0)),
        ],
        out_specs=[state, state, state, state],
        out_shape=[state_shape] * 4,
        compiler_params=pltpu.CompilerParams(
            dimension_semantics=("parallel",), vmem_limit_bytes=VMEM_LIMIT),
        name="context_states",
    )(log_gamma, ctx, mod_ctx, mod_ctx, norm_g, w_in_bf16, lb)


def _anchor(phi, half, forward):
    rows = phi.shape[0]
    pick = half - 1 if forward else half
    block = 2 * half
    p3 = phi.reshape(rows // 8, 8, DH)
    sub = lax.broadcasted_iota(jnp.int32, p3.shape, 1)
    out = None
    for start in range(0, 8, block):
        a = jnp.broadcast_to(p3[:, start + pick:start + pick + 1, :], p3.shape)
        out = a if out is None else jnp.where(sub >= start, a, out)
    return out.reshape(rows, DH)


def _level_operands(q, kf, kb, ff, fb, phif, phib, half, row):
    n = q.shape[0]
    if half == 1:
        odd = (row & 1) != 0
        zz = q * jnp.where(odd, ff, fb)
        ww = jnp.where(odd, kb, kf)
    elif half >= 8:
        zs, ws = [], []
        for b0 in range(0, n, 2 * half):
            first = slice(b0, b0 + half)
            second = slice(b0 + half, b0 + 2 * half)
            af = phif[b0 + half - 1:b0 + half, :]
            ab = phib[b0 + half:b0 + half + 1, :]
            zs += [q[first] * jnp.exp2(phib[first] - ab), q[second] * jnp.exp2(phif[second] - af)]
            ws += [kf[first] * jnp.exp2(af - phif[first]), kb[second] * jnp.exp2(ab - phib[second])]
        zz = jnp.concatenate(zs, axis=0)
        ww = jnp.concatenate(ws, axis=0)
    else:
        af = _anchor(phif, half, True)
        ab = _anchor(phib, half, False)
        second = (row & half) != 0
        zz = q * jnp.exp2(jnp.where(second, phif - af, phib - ab))
        ww = jnp.where(second, kb, kf) * jnp.exp2(jnp.where(second, ab - phib, af - phif))
    return zz.astype(BF16), ww.astype(BF16)


def _mixer_kernel(lg_ref, q_ref, v_ref, kf_ref, kb_ref, og_ref, rq_ref, rk_ref, rv_ref, rg_ref,
                  phif_ref, phib_ref, hgn_ref, rn_ref,
                  sf0_ref, sb0_ref, rf0_ref, rb0_ref,
                  hgo_ref, reto_ref,
                  oh_s, or_s, qcat_s, rqcat_s, sh_s, sr_s, shb_s, srb_s, dec_s):
    head = pl.program_id(1)
    seq = q_ref.shape[2]
    n_chunks = seq // CHUNK
    lgf = lg_ref[0, head]
    lgb = lg_ref[1, head]
    fwd = slice(0, DH)
    bwd = slice(DH, 2 * DH)

    row = lax.broadcasted_iota(jnp.int32, (CHUNK, DH), 0)
    t_idx = lax.broadcasted_iota(jnp.int32, (CHUNK, CHUNK), 0)
    s_idx = lax.broadcasted_iota(jnp.int32, (CHUNK, CHUNK), 1)
    split = t_idx ^ s_idx
    rel = (t_idx - s_idx).astype(F32)
    ret_decay = jnp.where(rel >= 0, jnp.exp(rel * lgf), 0.0) + jnp.where(rel <= 0, jnp.exp(-rel * lgb), 0.0)
    pos = row.astype(F32)
    qdec_f = jnp.exp((pos + 1.0) * lgf)
    kdec_f = jnp.exp((CHUNK - 1.0 - pos) * lgf)
    qdec_b = jnp.exp((CHUNK - pos) * lgb)
    kdec_b = jnp.exp(pos * lgb)

    def pass_a(i, carry):
        rows = pl.ds(pl.multiple_of(i * CHUNK, CHUNK), CHUNK)
        q = q_ref[0, 0, rows, :].astype(F32)
        v = v_ref[0, 0, rows, :]
        kf = kf_ref[0, 0, rows, :].astype(F32)
        kb = kb_ref[0, 0, rows, :].astype(F32)
        ff = 1.0 - kf
        fb = 1.0 - kb
        phif = phif_ref[0, 0, rows, :]
        phib = phib_ref[0, 0, rows, :]

        attn = None
        half = CHUNK // 2
        while half >= 1:
            zz, ww = _level_operands(q, kf, kb, ff, fb, phif, phib, half, row)
            a = _dot_nt(zz, ww)
            attn = a if attn is None else jnp.where(split < 2 * half, a, attn)
            half //= 2
        attn = jnp.where(split == 0, 0.0, attn)
        diag = jnp.sum(q * (kf + kb), axis=-1, keepdims=True)
        oh_s[rows, :] = _dot(attn.astype(BF16), v) + diag * v.astype(F32)

        last_f = phif[CHUNK - 1:CHUNK, :]
        last_b = phib[0:1, :]
        qcat_s[rows, fwd] = (q * jnp.exp2(phif)).astype(BF16)
        qcat_s[rows, bwd] = (q * jnp.exp2(phib)).astype(BF16)
        kcat = jnp.concatenate([(kf * jnp.exp2(last_f - phif)).astype(BF16),
                                (kb * jnp.exp2(last_b - phib)).astype(BF16)], axis=1)
        sh_s[i] = _dot_tn(v, kcat)
        dec_s[i, :, fwd] = jnp.exp2(last_f)
        dec_s[i, :, bwd] = jnp.exp2(last_b)

        rqb = rq_ref[0, 0, rows, :]
        rkb = rk_ref[0, 0, rows, :]
        rq = rqb.astype(F32)
        rk = rkb.astype(F32)
        rv = rv_ref[0, 0, rows, :]
        scores = _dot_nt(rqb, rkb) * ret_decay
        or_s[rows, :] = _dot(scores.astype(BF16), rv)
        rqcat_s[rows, fwd] = (rq * qdec_f).astype(BF16)
        rqcat_s[rows, bwd] = (rq * qdec_b).astype(BF16)
        rkcat = jnp.concatenate([(rk * kdec_f).astype(BF16), (rk * kdec_b).astype(BF16)], axis=1)
        sr_s[i] = _dot_tn(rv, rkcat)
        return carry

    lax.fori_loop(0, n_chunks, pass_a, 0, unroll=min(PASS_A_UNROLL, n_chunks))

    cdec_f = jnp.exp(CHUNK * lgf)
    cdec_b = jnp.exp(CHUNK * lgb)
    sf, rf = sf0_ref[0, 0], rf0_ref[0, 0]
    for i in range(n_chunks):
        shb_s[i, :, fwd] = sf.astype(BF16)
        srb_s[i, :, fwd] = rf.astype(BF16)
        sf = sf * dec_s[i, :, fwd] + sh_s[i, :, fwd]
        rf = rf * cdec_f + sr_s[i, :, fwd]
    sb, rb = sb0_ref[0, 0], rb0_ref[0, 0]
    for i in reversed(range(n_chunks)):
        shb_s[i, :, bwd] = sb.astype(BF16)
        srb_s[i, :, bwd] = rb.astype(BF16)
        sb = sb * dec_s[i, :, bwd] + sh_s[i, :, bwd]
        rb = rb * cdec_b + sr_s[i, :, bwd]

    def pass_c(i, carry):
        rows = pl.ds(pl.multiple_of(i * CHUNK, CHUNK), CHUNK)
        o = oh_s[rows, :] + _dot_nt(qcat_s[rows, :], shb_s[i])
        hg = o * jax.nn.sigmoid(og_ref[0, 0, rows, :].astype(F32))
        hgo_ref[0, 0, rows, :] = _rms(hg, hgn_ref[...]).astype(BF16)
        y = or_s[rows, :] + _dot_nt(rqcat_s[rows, :], srb_s[i])
        rg = rg_ref[0, 0, rows, :].astype(F32)
        ret = _rms(y, rn_ref[...]) * (rg * jax.nn.sigmoid(rg))
        reto_ref[0, 0, rows, :] = ret.astype(BF16)
        return carry

    lax.fori_loop(0, n_chunks, pass_c, 0, unroll=min(PASS_C_UNROLL, n_chunks))


def _mixer(p, phi, log_gamma, hg_norm, ret_norm, states):
    batch, _, seq, _ = p.shape
    n_chunks = seq // CHUNK

    def group(g):
        return pl.BlockSpec((1, 1, seq, DH), lambda b, h: (b, g * HEADS + h, 0, 0))

    per_head_row = pl.BlockSpec((1, DH), lambda b, h: (0, h))
    state = pl.BlockSpec((1, 1, DH, DH), lambda b, h: (b, h, 0, 0))
    out_spec = pl.BlockSpec((1, 1, seq, DH), lambda b, h: (b, h, 0, 0))
    out_shape = jax.ShapeDtypeStruct((batch, HEADS, seq, DH), BF16)
    return pl.pallas_call(
        _mixer_kernel,
        grid=(batch, HEADS),
        in_specs=[pl.BlockSpec(memory_space=pltpu.SMEM)]
        + [group(g) for g in range(N_GROUPS)]
        + [pl.BlockSpec((1, 1, seq, DH), lambda b, h: (b, h, 0, 0)),
           pl.BlockSpec((1, 1, seq, DH), lambda b, h: (b, HEADS + h, 0, 0)),
           per_head_row, per_head_row, state, state, state, state],
        out_specs=[out_spec, out_spec],
        out_shape=[out_shape, out_shape],
        scratch_shapes=[
            pltpu.VMEM((seq, DH), F32), pltpu.VMEM((seq, DH), F32),
            pltpu.VMEM((seq, 2 * DH), BF16), pltpu.VMEM((seq, 2 * DH), BF16),
            pltpu.VMEM((n_chunks, DH, 2 * DH), F32),
            pltpu.VMEM((n_chunks, DH, 2 * DH), F32),
            pltpu.VMEM((n_chunks, DH, 2 * DH), BF16),
            pltpu.VMEM((n_chunks, DH, 2 * DH), BF16),
            pltpu.VMEM((n_chunks, 1, 2 * DH), F32),
        ],
        compiler_params=pltpu.CompilerParams(
            dimension_semantics=("parallel", "parallel"), vmem_limit_bytes=VMEM_LIMIT),
        name="token_mixer",
    )(log_gamma, p, p, p, p, p, p, p, p, p, phi, phi, hg_norm, ret_norm, *states)


FFN_ROWS = 1024
HALO = 16


def _ffn_kernel(x_ref, xp_ref, xn_ref, hg_ref, hgp_ref, hgn_ref, rt_ref, rtp_ref, rtn_ref,
                g1_ref, sh2_ref, sc2_ref, g2_ref, woa_ref, wob_ref, n2_ref, fin_ref,
                wup_ref, cw_ref, cb_ref, wd_ref,
                o_ref, h2_s, gate2_s, gate3_s, up2_s, up3_s, acc_s):
    m = pl.program_id(1)
    rows = x_ref.shape[1]
    n_sub = rows // ROW_TILE
    def ext(k):
        lo = 0 if k == 0 else HALO + k * ROW_TILE
        hi = HALO + (k + 1) * ROW_TILE + (HALO if k == n_sub - 1 else 0)
        return lo, hi

    def piece(main_ref, prev_ref, next_ref, k):
        def rows_of(ref, sl):
            if len(ref.shape) == 3:
                return ref[0, sl, :]
            return jnp.concatenate([ref[0, h, sl, :] for h in range(HEADS)], axis=1)

        parts = ([rows_of(prev_ref, slice(None))] if k == 0 else []) \
            + [rows_of(main_ref, slice(k * ROW_TILE, (k + 1) * ROW_TILE))] \
            + ([rows_of(next_ref, slice(None))] if k == n_sub - 1 else [])
        return jnp.concatenate(parts, axis=0) if len(parts) > 1 else parts[0]

    for k in range(n_sub):
        lo, hi = ext(k)
        skip = HALO if k == 0 else 0
        proj = (_dot(piece(hg_ref, hgp_ref, hgn_ref, k), woa_ref[...])
                + _dot(piece(rt_ref, rtp_ref, rtn_ref, k), wob_ref[...]))
        x1 = piece(x_ref, xp_ref, xn_ref, k) + g1_ref[0] * proj
        o_ref[0, k * ROW_TILE:(k + 1) * ROW_TILE, :] = x1[skip:skip + ROW_TILE]
        h2 = _rms(x1, n2_ref[...]) * (1.0 + sc2_ref[0]) + sh2_ref[0]
        h2_s[lo:hi, :] = h2.astype(BF16)

    prev_valid = (m > 0).astype(F32)
    next_valid = (m < pl.num_programs(1) - 1).astype(F32)
    edge_lo = slice(HALO - 8, HALO)
    edge_hi = slice(HALO + rows, HALO + rows + 8)

    gate_bufs = (gate2_s, gate3_s)
    up_bufs = (up2_s, up3_s)
    tiles = [(c0, min(FF_TILE, D_FF - c0)) for c0 in range(0, D_FF, FF_TILE)]

    def gate_up(j):
        c0, width = tiles[j]
        gate_s, up_s = gate_bufs[j % 2], up_bufs[j % 2]
        w_gu = jnp.concatenate([wup_ref[:, c0:c0 + width], wup_ref[:, D_FF + c0:D_FF + c0 + width]], axis=1)
        for k in range(n_sub):
            lo, hi = ext(k)
            skip = HALO if k == 0 else 0
            gu = _dot(h2_s[lo:hi, :], w_gu)
            gate_s[lo:hi, :width] = gu[:, :width]
            up_s[k * ROW_TILE:(k + 1) * ROW_TILE, :width] = gu[skip:skip + ROW_TILE, width:]
        gate_s[edge_lo, :width] = gate_s[edge_lo, :width] * prev_valid
        gate_s[edge_hi, :width] = gate_s[edge_hi, :width] * next_valid

    gate_up(0)
    for j, (c0, width) in enumerate(tiles):
        if j + 1 < len(tiles):
            gate_up(j + 1)
        cols = slice(c0, c0 + width)
        gate_s, up_s = gate_bufs[j % 2], up_bufs[j % 2]
        for k in range(n_sub):
            sub = slice(k * ROW_TILE, (k + 1) * ROW_TILE)
            base = HALO + k * ROW_TILE
            g = gate_s[base - 8:base + ROW_TILE + 8, :width]
            g_prev = pltpu.roll(g, 1, axis=0)[8:8 + ROW_TILE]
            g_next = pltpu.roll(g, ROW_TILE + 15, axis=0)[8:8 + ROW_TILE]
            conv = (g_prev * cw_ref[0:1, cols] + g[8:8 + ROW_TILE] * cw_ref[1:2, cols]
                    + g_next * cw_ref[2:3, cols] + cb_ref[:, cols])
            act = (conv * jax.nn.sigmoid(conv) * up_s[sub, :width]).astype(BF16)
            ffn = _dot(act, wd_ref[cols, :])
            if j > 0:
                ffn += acc_s[sub, :]
            if j < len(tiles) - 1:
                acc_s[sub, :] = ffn
            else:
                x2 = o_ref[0, sub, :] + g2_ref[0] * ffn
                o_ref[0, sub, :] = _rms(x2, fin_ref[...])


def _out_ffn(x, hgo, reto, mod3, w_out_bf16, norm2_g, final_g, w_up_bf16, conv_w, conv_b, w_down_bf16):
    batch, seq, _ = x.shape
    rows = min(FFN_ROWS, seq)
    n_row_tiles = seq // rows
    halo_per_tile = rows // HALO
    n_halo_blocks = seq // HALO
    once = pl.Buffered(1)

    def prev_block(m):
        return jnp.maximum(m * halo_per_tile - 1, 0)

    def next_block(m):
        return jnp.minimum((m + 1) * halo_per_tile, n_halo_blocks - 1)

    x_main = pl.BlockSpec((1, rows, D_MODEL), lambda b, m: (b, m, 0))
    x_prev = pl.BlockSpec((1, HALO, D_MODEL), lambda b, m: (b, prev_block(m), 0))
    x_next = pl.BlockSpec((1, HALO, D_MODEL), lambda b, m: (b, next_block(m), 0))
    h_main = pl.BlockSpec((1, HEADS, rows, DH), lambda b, m: (b, 0, m, 0))
    h_prev = pl.BlockSpec((1, HEADS, HALO, DH), lambda b, m: (b, 0, prev_block(m), 0))
    h_next = pl.BlockSpec((1, HEADS, HALO, DH), lambda b, m: (b, 0, next_block(m), 0))

    def mod_spec(chunk):
        return pl.BlockSpec((1, 1, D_MODEL), lambda b, m: (b, 0, chunk))

    def const(shape, index=(0, 0)):
        return pl.BlockSpec(shape, lambda b, m: index, pipeline_mode=once)

    return pl.pallas_call(
        _ffn_kernel,
        grid=(batch, n_row_tiles),
        in_specs=[
            x_main, x_prev, x_next,
            h_main, h_prev, h_next,
            h_main, h_prev, h_next,
            mod_spec(2), mod_spec(3), mod_spec(4), mod_spec(5),
            const((GROUP_W, D_MODEL), (0, 0)),
            const((GROUP_W, D_MODEL), (1, 0)),
            const((1, D_MODEL)), const((1, D_MODEL)),
            const((D_MODEL, 2 * D_FF)),
            const((3, D_FF)), const((1, D_FF)),
            const((D_FF, D_MODEL)),
        ],
        out_specs=pl.BlockSpec((1, rows, D_MODEL), lambda b, m: (b, m, 0)),
        out_shape=jax.ShapeDtypeStruct((batch, seq, D_MODEL), F32),
        scratch_shapes=[
            pltpu.VMEM((rows + 2 * HALO, D_MODEL), BF16),
            pltpu.VMEM((rows + 2 * HALO, FF_TILE), F32),
            pltpu.VMEM((rows + 2 * HALO, FF_TILE), F32),
            pltpu.VMEM((rows, FF_TILE), F32),
            pltpu.VMEM((rows, FF_TILE), F32),
            pltpu.VMEM((rows, D_MODEL), F32),
        ],
        compiler_params=pltpu.CompilerParams(
            dimension_semantics=("parallel", "parallel"), vmem_limit_bytes=VMEM_LIMIT),
        name="out_proj_ffn",
    )(x, x, x, hgo, hgo, hgo, reto, reto, reto, mod3, mod3, mod3, mod3,
      w_out_bf16, w_out_bf16, norm2_g, final_g, w_up_bf16, conv_w, conv_b, w_down_bf16)


def _rope_tables(seq):
    quarter = DH // 4
    freqs = np.float32(ROPE_THETA) ** (-np.arange(quarter, dtype=np.float32) / np.float32(quarter))
    t = np.arange(seq)
    ang_r = (t // GRID_W).astype(np.float32)[:, None] * freqs[None, :]
    ang_c = (t % GRID_W).astype(np.float32)[:, None] * freqs[None, :]
    cos = np.concatenate([np.cos(ang_r)] * 2 + [np.cos(ang_c)] * 2, axis=-1)
    sin = np.concatenate([-np.sin(ang_r), np.sin(ang_r), -np.sin(ang_c), np.sin(ang_c)], axis=-1)
    return jnp.asarray(cos, F32), jnp.asarray(sin, F32)


def kernel(x, c, ctx, c_ctx, w_mod, b_mod, norm1_g, w_in, hgrn_lb, hgrn_norm_g, ret_decay,
           ret_norm_g, w_out, norm2_g, w_up, conv_w, conv_b, w_down, final_g):
    batch, seq, _ = x.shape
    n_ctx = ctx.shape[1]
    assert w_mod.shape[0] == 1, "single-layer block"
    assert seq % CHUNK == 0 and seq % ROW_TILE == 0 and (n_ctx & (n_ctx - 1)) == 0

    lb = jnp.cumsum(jax.nn.softmax(hgrn_lb.astype(F32), axis=1), axis=1)[:, 0]
    log_gamma = jax.nn.log_sigmoid(ret_decay[0].astype(F32))
    cos_t, sin_t = _rope_tables(seq)

    n_mod_rows = -(-(batch + 1) // 8) * 8
    c_rows = jnp.zeros((n_mod_rows, D_MODEL), F32).at[:batch].set(c).at[batch].set(c_ctx)
    mod = _modulation(c_rows, w_mod[0], b_mod[0][None, :])
    mod3 = mod.reshape(n_mod_rows, 1, 6 * D_MODEL)

    w_in_b = w_in[0].astype(BF16)
    norm1 = norm1_g[0][None, :]
    states = _context_states(ctx, mod3[batch:batch + 1], norm1, w_in_b, lb, log_gamma)
    p, phi = _in_projection(x.reshape(batch * seq, D_MODEL), mod3[:batch], seq, norm1, w_in_b,
                            cos_t, sin_t, lb)
    hgo, reto = _mixer(p, phi, log_gamma,
                       hgrn_norm_g[0][None, :], ret_norm_g[0][None, :], states)

    return _out_ffn(x, hgo, reto, mod3[:batch], w_out[0].astype(BF16), norm2_g[0][None, :],
                    final_g[None, :], w_up[0].astype(BF16), conv_w[0], conv_b[0][None, :],
                    w_down[0].astype(BF16))
```

```python
import jax
import jax.numpy as jnp
import numpy as np
from jax import lax
from jax.experimental import pallas as pl
from jax.experimental.pallas import tpu as pltpu

F32 = jnp.float32
BF16 = jnp.bfloat16

D_MODEL = 1024
HEADS = 4
DH = 128
GROUP_W = HEADS * DH
N_GROUPS = 9
IN_WIDTH = N_GROUPS * GROUP_W
D_FF = 2816
GRID_W = 64
ROPE_THETA = 10000.0
EPS = 1e-6
LOG2E = 1.4426950408889634

CHUNK = 128
FF_TILE = 256
ROW_TILE = 512
IN_ROW_TILE = 1024
CTX_PER_STEP = 4
PASS_A_UNROLL = 16
PASS_C_UNROLL = 16
VMEM_LIMIT = 56 * 1024 * 1024


def _dot(a, b):
    return jnp.dot(a, b, preferred_element_type=F32)


def _dot_nt(a, b):
    return lax.dot_general(a, b, (((1,), (1,)), ((), ())), preferred_element_type=F32)


def _dot_tn(a, b):
    return lax.dot_general(a, b, (((0,), (0,)), ((), ())), preferred_element_type=F32)


def _rms(x, gain):
    return x * lax.rsqrt(jnp.mean(x * x, axis=-1, keepdims=True) + EPS) * gain


def _cumsum_rows(x):
    rows = x.shape[0]
    row = lax.broadcasted_iota(jnp.int32, x.shape, 0)
    shift = 1
    while shift < rows:
        x = x + jnp.where(row >= shift, pltpu.roll(x, shift, axis=0), 0.0)
        shift *= 2
    return x


def _mod_kernel(c_ref, w_ref, b_ref, o_ref):
    c = c_ref[...]
    a = (c * jax.nn.sigmoid(c)).astype(BF16)
    o_ref[...] = _dot(a, w_ref[...].astype(BF16)) + b_ref[...]


def _modulation(c_rows, w_mod, b_mod):
    n_rows = c_rows.shape[0]
    width = w_mod.shape[1]
    tile = D_MODEL
    return pl.pallas_call(
        _mod_kernel,
        grid=(width // tile,),
        in_specs=[
            pl.BlockSpec((n_rows, D_MODEL), lambda j: (0, 0)),
            pl.BlockSpec((D_MODEL, tile), lambda j: (0, j)),
            pl.BlockSpec((1, tile), lambda j: (0, j)),
        ],
        out_specs=pl.BlockSpec((n_rows, tile), lambda j: (0, j)),
        out_shape=jax.ShapeDtypeStruct((n_rows, width), F32),
        compiler_params=pltpu.CompilerParams(dimension_semantics=("parallel",)),
        name="modulation",
    )(c_rows, w_mod, b_mod)


G_HQ, G_HV, G_ZF, G_ZB, G_HG, G_RQ, G_RK, G_RV, G_RG = range(N_GROUPS)
GROUP_ORDER = (G_ZF, G_ZB, G_RQ, G_RK, G_HQ, G_HV, G_HG, G_RV, G_RG)


def _rope(t, cos, sin_signed, first_quarter):
    swapped = jnp.where(first_quarter, pltpu.roll(t, DH - DH // 4, axis=1),
                        pltpu.roll(t, DH // 4, axis=1))
    return t * cos + swapped * sin_signed


def _modulated_norm(x, gain, scale, shift):
    return (_rms(x, gain) * (1.0 + scale) + shift).astype(BF16)


def _forget(z, lb):
    f = lb + (1.0 - lb) * jax.nn.sigmoid(z)
    return f, jnp.log(f)


def _inproj_kernel(x_ref, sh_ref, sc_ref, g_ref, w_ref, cos_ref, sin_ref, lb_ref, o_ref, phi_ref, zf_s, zb_s):
    hb = _modulated_norm(x_ref[...], g_ref[...], sc_ref[0], sh_ref[0])
    n_rows = hb.shape[0]
    cos = cos_ref[...]
    sin = sin_ref[...]
    lane = lax.broadcasted_iota(jnp.int32, cos.shape, 1)
    first_quarter = (lane % (DH // 2)) < (DH // 4)

    def gate_tile(backward, c0, h, zero):
        hcols = slice(h * DH, (h + 1) * DH)
        z_s, lb, g = (zb_s, lb_ref[1:2, hcols], G_ZB) if backward else (zf_s, lb_ref[0:1, hcols], G_ZF)
        f, lf = _forget(z_s[c0:c0 + CHUNK, hcols], lb + zero)
        lf = lf * LOG2E
        o_ref[0, g * HEADS + h, c0:c0 + CHUNK, :] = (1.0 - f).astype(BF16)
        csum = _cumsum_rows(lf)
        if backward:
            csum = csum[CHUNK - 1:CHUNK, :] - csum + lf
        phi_ref[0, (HEADS if backward else 0) + h, c0:c0 + CHUNK, :] = csum

    zf_s[...] = _dot(hb, w_ref[:, G_ZF * GROUP_W:(G_ZF + 1) * GROUP_W])
    gate_work = [(d, c0, h) for d in (False, True) for c0 in range(0, n_rows, CHUNK) for h in range(HEADS)]
    hosts = [g for g in GROUP_ORDER if g != G_ZF]
    per_host = -(-len(gate_work) // len(hosts))
    for idx, g in enumerate(hosts):
        cols = slice(g * GROUP_W, (g + 1) * GROUP_W)
        r = _dot(hb, w_ref[:, cols])
        if g == G_ZB:
            zb_s[...] = r
        share = gate_work[idx * per_host:(idx + 1) * per_host]
        for t, item in enumerate(share):
            r0 = (t * n_rows // len(share)) // 8 * 8
            bits = pltpu.bitcast(r[r0:r0 + 8, 0:DH], jnp.uint32)
            gate_tile(*item, ((bits >> 16) >> 16).astype(F32)[0:1, :])
        if g == G_ZB:
            continue
        if g == G_RK:
            r = r * (DH ** -0.5)
        for h in range(HEADS):
            t = r[:, h * DH:(h + 1) * DH]
            if g in (G_RQ, G_RK):
                t = _rope(t, cos, sin, first_quarter)
            o_ref[0, g * HEADS + h] = t.astype(BF16)


def _in_projection(x2d, mod3, seq, norm_g, w_in_bf16, cos_t, sin_t, lb):
    n_rows = x2d.shape[0]
    row_tile = min(IN_ROW_TILE, seq)
    tiles_per_seq = seq // row_tile

    def mod_spec(chunk):
        return pl.BlockSpec((1, 1, D_MODEL), lambda i: (i // tiles_per_seq, 0, chunk))

    table = pl.BlockSpec((row_tile, DH), lambda i: (i % tiles_per_seq, 0))
    return pl.pallas_call(
        _inproj_kernel,
        grid=(n_rows // row_tile,),
        in_specs=[
            pl.BlockSpec((row_tile, D_MODEL), lambda i: (i, 0)),
            mod_spec(0),
            mod_spec(1),
            pl.BlockSpec((1, D_MODEL), lambda i: (0, 0)),
            pl.BlockSpec((D_MODEL, IN_WIDTH), lambda i: (0, 0), pipeline_mode=pl.Buffered(1)),
            table, table,
            pl.BlockSpec((2, GROUP_W), lambda i: (0, 0)),
        ],
        out_specs=[pl.BlockSpec((1, N_GROUPS * HEADS, row_tile, DH),
                                lambda i: (i // tiles_per_seq, 0, i % tiles_per_seq, 0)),
                   pl.BlockSpec((1, 2 * HEADS, row_tile, DH),
                                lambda i: (i // tiles_per_seq, 0, i % tiles_per_seq, 0))],
        out_shape=[jax.ShapeDtypeStruct((n_rows // seq, N_GROUPS * HEADS, seq, DH), BF16),
                   jax.ShapeDtypeStruct((n_rows // seq, 2 * HEADS, seq, DH), F32)],
        scratch_shapes=[pltpu.VMEM((row_tile, GROUP_W), F32), pltpu.VMEM((row_tile, GROUP_W), F32)],
        compiler_params=pltpu.CompilerParams(
            dimension_semantics=("parallel",), vmem_limit_bytes=VMEM_LIMIT),
        name="in_projection",
    )(x2d, mod3, mod3, norm_g, w_in_bf16, cos_t, sin_t, lb)


def _ctx_kernel(lg_ref, c_ref, sh_ref, sc_ref, g_ref, w_ref, lb_ref,
                sf_ref, sb_ref, rf_ref, rb_ref):
    per_step, n = c_ref.shape[0], c_ref.shape[1]
    hb = _modulated_norm(c_ref[...].reshape(per_step * n, D_MODEL), g_ref[...], sc_ref[0], sh_ref[0])

    def group(g):
        return _dot(hb, w_ref[:, g * GROUP_W:(g + 1) * GROUP_W])

    v_all = group(G_HV).astype(BF16)
    zf_all = group(G_ZF)
    zb_all = group(G_ZB)
    rk_all = group(G_RK) * (DH ** -0.5)
    rv_all = group(G_RV).astype(BF16)
    pos = lax.broadcasted_iota(jnp.int32, (n, DH), 0).astype(F32)
    for e in range(per_step):
        rows = slice(e * n, (e + 1) * n)
        v, rk, rv = v_all[rows], rk_all[rows], rv_all[rows]
        ff, lff = _forget(zf_all[rows], lb_ref[0:1, :])
        fb, lfb = _forget(zb_all[rows], lb_ref[1:2, :])
        bf = _cumsum_rows(lff)
        bb = _cumsum_rows(lfb)
        kf = ((1.0 - ff) * jnp.exp(bf[n - 1:n, :] - bf)).astype(BF16)
        kb = ((1.0 - fb) * jnp.exp(bb - lfb)).astype(BF16)
        for h in range(HEADS):
            cols = slice(h * DH, (h + 1) * DH)
            sf_ref[e, h] = _dot_tn(v[:, cols], kf[:, cols])
            sb_ref[e, h] = _dot_tn(v[:, cols], kb[:, cols])
            wf = jnp.exp((n - 1.0 - pos) * lg_ref[0, h])
            wb = jnp.exp(pos * lg_ref[1, h])
            rf_ref[e, h] = _dot_tn(rv[:, cols], (rk[:, cols] * wf).astype(BF16))
            rb_ref[e, h] = _dot_tn(rv[:, cols], (rk[:, cols] * wb).astype(BF16))


def _context_states(ctx, mod_ctx, norm_g, w_in_bf16, lb, log_gamma):
    batch, n_ctx, _ = ctx.shape
    per_step = CTX_PER_STEP if batch % CTX_PER_STEP == 0 else 1

    def mod_spec(chunk):
        return pl.BlockSpec((1, 1, D_MODEL), lambda b: (0, 0, chunk))

    state = pl.BlockSpec((per_step, HEADS, DH, DH), lambda b: (b, 0, 0, 0))
    state_shape = jax.ShapeDtypeStruct((batch, HEADS, DH, DH), F32)
    return pl.pallas_call(
        _ctx_kernel,
        grid=(batch // per_step,),
        in_specs=[
            pl.BlockSpec(memory_space=pltpu.SMEM),
            pl.BlockSpec((per_step, n_ctx, D_MODEL), lambda b: (b, 0, 0)),
            mod_spec(0), mod_spec(1),
            pl.BlockSpec((1, D_MODEL), lambda b: (0, 0)),
            pl.BlockSpec((D_MODEL, IN_WIDTH), lambda b: (0, 0), pipeline_mode=pl.Buffered(1)),
            pl.BlockSpec((2, GROUP_W), lambda b: (0, 0)),
        ],
        out_specs=[state, state, state, state],
        out_shape=[state_shape] * 4,
        compiler_params=pltpu.CompilerParams(
            dimension_semantics=("parallel",), vmem_limit_bytes=VMEM_LIMIT),
        name="context_states",
    )(log_gamma, ctx, mod_ctx, mod_ctx, norm_g, w_in_bf16, lb)


def _anchor(phi, half, forward):
    rows = phi.shape[0]
    pick = half - 1 if forward else half
    block = 2 * half
    p3 = phi.reshape(rows // 8, 8, DH)
    sub = lax.broadcasted_iota(jnp.int32, p3.shape, 1)
    out = None
    for start in range(0, 8, block):
        a = jnp.broadcast_to(p3[:, start + pick:start + pick + 1, :], p3.shape)
        out = a if out is None else jnp.where(sub >= start, a, out)
    return out.reshape(rows, DH)


def _level_operands(q, kf, kb, ff, fb, phif, phib, half, row):
    n = q.shape[0]
    if half == 1:
        odd = (row & 1) != 0
        zz = q * jnp.where(odd, ff, fb)
        ww = jnp.where(odd, kb, kf)
    elif half >= 8:
        zs, ws = [], []
        for b0 in range(0, n, 2 * half):
            first = slice(b0, b0 + half)
            second = slice(b0 + half, b0 + 2 * half)
            af = phif[b0 + half - 1:b0 + half, :]
            ab = phib[b0 + half:b0 + half + 1, :]
            zs += [q[first] * jnp.exp2(phib[first] - ab), q[second] * jnp.exp2(phif[second] - af)]
            ws += [kf[first] * jnp.exp2(af - phif[first]), kb[second] * jnp.exp2(ab - phib[second])]
        zz = jnp.concatenate(zs, axis=0)
        ww = jnp.concatenate(ws, axis=0)
    else:
        af = _anchor(phif, half, True)
        ab = _anchor(phib, half, False)
        second = (row & half) != 0
        zz = q * jnp.exp2(jnp.where(second, phif - af, phib - ab))
        ww = jnp.where(second, kb, kf) * jnp.exp2(jnp.where(second, ab - phib, af - phif))
    return zz.astype(BF16), ww.astype(BF16)


def _mixer_kernel(lg_ref, q_ref, v_ref, kf_ref, kb_ref, og_ref, rq_ref, rk_ref, rv_ref, rg_ref,
                  phif_ref, phib_ref, hgn_ref, rn_ref,
                  sf0_ref, sb0_ref, rf0_ref, rb0_ref,
                  hgo_ref, reto_ref,
                  oh_s, or_s, qcat_s, rqcat_s, sh_s, sr_s, shb_s, srb_s, dec_s):
    head = pl.program_id(1)
    seq = q_ref.shape[2]
    n_chunks = seq // CHUNK
    lgf = lg_ref[0, head]
    lgb = lg_ref[1, head]
    fwd = slice(0, DH)
    bwd = slice(DH, 2 * DH)

    row = lax.broadcasted_iota(jnp.int32, (CHUNK, DH), 0)
    t_idx = lax.broadcasted_iota(jnp.int32, (CHUNK, CHUNK), 0)
    s_idx = lax.broadcasted_iota(jnp.int32, (CHUNK, CHUNK), 1)
    split = t_idx ^ s_idx
    rel = (t_idx - s_idx).astype(F32)
    ret_decay = jnp.where(rel >= 0, jnp.exp(rel * lgf), 0.0) + jnp.where(rel <= 0, jnp.exp(-rel * lgb), 0.0)
    pos = row.astype(F32)
    qdec_f = jnp.exp((pos + 1.0) * lgf)
    kdec_f = jnp.exp((CHUNK - 1.0 - pos) * lgf)
    qdec_b = jnp.exp((CHUNK - pos) * lgb)
    kdec_b = jnp.exp(pos * lgb)

    def pass_a(i, carry):
        rows = pl.ds(pl.multiple_of(i * CHUNK, CHUNK), CHUNK)
        q = q_ref[0, 0, rows, :].astype(F32)
        v = v_ref[0, 0, rows, :]
        kf = kf_ref[0, 0, rows, :].astype(F32)
        kb = kb_ref[0, 0, rows, :].astype(F32)
        ff = 1.0 - kf
        fb = 1.0 - kb
        phif = phif_ref[0, 0, rows, :]
        phib = phib_ref[0, 0, rows, :]

        attn = None
        half = CHUNK // 2
        while half >= 1:
            zz, ww = _level_operands(q, kf, kb, ff, fb, phif, phib, half, row)
            a = _dot_nt(zz, ww)
            attn = a if attn is None else jnp.where(split < 2 * half, a, attn)
            half //= 2
        attn = jnp.where(split == 0, 0.0, attn)
        diag = jnp.sum(q * (kf + kb), axis=-1, keepdims=True)
        oh_s[rows, :] = _dot(attn.astype(BF16), v) + diag * v.astype(F32)

        last_f = phif[CHUNK - 1:CHUNK, :]
        last_b = phib[0:1, :]
        qcat_s[rows, fwd] = (q * jnp.exp2(phif)).astype(BF16)
        qcat_s[rows, bwd] = (q * jnp.exp2(phib)).astype(BF16)
        kcat = jnp.concatenate([(kf * jnp.exp2(last_f - phif)).astype(BF16),
                                (kb * jnp.exp2(last_b - phib)).astype(BF16)], axis=1)
        sh_s[i] = _dot_tn(v, kcat)
        dec_s[i, :, fwd] = jnp.exp2(last_f)
        dec_s[i, :, bwd] = jnp.exp2(last_b)

        rqb = rq_ref[0, 0, rows, :]
        rkb = rk_ref[0, 0, rows, :]
        rq = rqb.astype(F32)
        rk = rkb.astype(F32)
        rv = rv_ref[0, 0, rows, :]
        scores = _dot_nt(rqb, rkb) * ret_decay
        or_s[rows, :] = _dot(scores.astype(BF16), rv)
        rqcat_s[rows, fwd] = (rq * qdec_f).astype(BF16)
        rqcat_s[rows, bwd] = (rq * qdec_b).astype(BF16)
        rkcat = jnp.concatenate([(rk * kdec_f).astype(BF16), (rk * kdec_b).astype(BF16)], axis=1)
        sr_s[i] = _dot_tn(rv, rkcat)
        return carry

    lax.fori_loop(0, n_chunks, pass_a, 0, unroll=min(PASS_A_UNROLL, n_chunks))

    cdec_f = jnp.exp(CHUNK * lgf)
    cdec_b = jnp.exp(CHUNK * lgb)
    sf, rf = sf0_ref[0, 0], rf0_ref[0, 0]
    for i in range(n_chunks):
        shb_s[i, :, fwd] = sf.astype(BF16)
        srb_s[i, :, fwd] = rf.astype(BF16)
        sf = sf * dec_s[i, :, fwd] + sh_s[i, :, fwd]
        rf = rf * cdec_f + sr_s[i, :, fwd]
    sb, rb = sb0_ref[0, 0], rb0_ref[0, 0]
    for i in reversed(range(n_chunks)):
        shb_s[i, :, bwd] = sb.astype(BF16)
        srb_s[i, :, bwd] = rb.astype(BF16)
        sb = sb * dec_s[i, :, bwd] + sh_s[i, :, bwd]
        rb = rb * cdec_b + sr_s[i, :, bwd]

    def pass_c(i, carry):
        rows = pl.ds(pl.multiple_of(i * CHUNK, CHUNK), CHUNK)
        o = oh_s[rows, :] + _dot_nt(qcat_s[rows, :], shb_s[i])
        hg = o * jax.nn.sigmoid(og_ref[0, 0, rows, :].astype(F32))
        hgo_ref[0, 0, rows, :] = _rms(hg, hgn_ref[...]).astype(BF16)
        y = or_s[rows, :] + _dot_nt(rqcat_s[rows, :], srb_s[i])
        rg = rg_ref[0, 0, rows, :].astype(F32)
        ret = _rms(y, rn_ref[...]) * (rg * jax.nn.sigmoid(rg))
        reto_ref[0, 0, rows, :] = ret.astype(BF16)
        return carry

    lax.fori_loop(0, n_chunks, pass_c, 0, unroll=min(PASS_C_UNROLL, n_chunks))


def _mixer(p, phi, log_gamma, hg_norm, ret_norm, states):
    batch, _, seq, _ = p.shape
    n_chunks = seq // CHUNK

    def group(g):
        return pl.BlockSpec((1, 1, seq, DH), lambda b, h: (b, g * HEADS + h, 0, 0))

    per_head_row = pl.BlockSpec((1, DH), lambda b, h: (0, h))
    state = pl.BlockSpec((1, 1, DH, DH), lambda b, h: (b, h, 0, 0))
    out_spec = pl.BlockSpec((1, 1, seq, DH), lambda b, h: (b, h, 0, 0))
    out_shape = jax.ShapeDtypeStruct((batch, HEADS, seq, DH), BF16)
    return pl.pallas_call(
        _mixer_kernel,
        grid=(batch, HEADS),
        in_specs=[pl.BlockSpec(memory_space=pltpu.SMEM)]
        + [group(g) for g in range(N_GROUPS)]
        + [pl.BlockSpec((1, 1, seq, DH), lambda b, h: (b, h, 0, 0)),
           pl.BlockSpec((1, 1, seq, DH), lambda b, h: (b, HEADS + h, 0, 0)),
           per_head_row, per_head_row, state, state, state, state],
        out_specs=[out_spec, out_spec],
        out_shape=[out_shape, out_shape],
        scratch_shapes=[
            pltpu.VMEM((seq, DH), F32), pltpu.VMEM((seq, DH), F32),
            pltpu.VMEM((seq, 2 * DH), BF16), pltpu.VMEM((seq, 2 * DH), BF16),
            pltpu.VMEM((n_chunks, DH, 2 * DH), F32),
            pltpu.VMEM((n_chunks, DH, 2 * DH), F32),
            pltpu.VMEM((n_chunks, DH, 2 * DH), BF16),
            pltpu.VMEM((n_chunks, DH, 2 * DH), BF16),
            pltpu.VMEM((n_chunks, 1, 2 * DH), F32),
        ],
        compiler_params=pltpu.CompilerParams(
            dimension_semantics=("parallel", "parallel"), vmem_limit_bytes=VMEM_LIMIT),
        name="token_mixer",
    )(log_gamma, p, p, p, p, p, p, p, p, p, phi, phi, hg_norm, ret_norm, *states)


FFN_ROWS = 1024
HALO = 16


def _ffn_kernel(x_ref, xp_ref, xn_ref, hg_ref, hgp_ref, hgn_ref, rt_ref, rtp_ref, rtn_ref,
                g1_ref, sh2_ref, sc2_ref, g2_ref, woa_ref, wob_ref, n2_ref, fin_ref,
                wup_ref, cw_ref, cb_ref, wd_ref,
                o_ref, h2_s, gate2_s, gate3_s, up2_s, up3_s, acc_s):
    m = pl.program_id(1)
    rows = x_ref.shape[1]
    n_sub = rows // ROW_TILE
    def ext(k):
        lo = 0 if k == 0 else HALO + k * ROW_TILE
        hi = HALO + (k + 1) * ROW_TILE + (HALO if k == n_sub - 1 else 0)
        return lo, hi

    def piece(main_ref, prev_ref, next_ref, k):
        def rows_of(ref, sl):
            if len(ref.shape) == 3:
                return ref[0, sl, :]
            return jnp.concatenate([ref[0, h, sl, :] for h in range(HEADS)], axis=1)

        parts = ([rows_of(prev_ref, slice(None))] if k == 0 else []) \
            + [rows_of(main_ref, slice(k * ROW_TILE, (k + 1) * ROW_TILE))] \
            + ([rows_of(next_ref, slice(None))] if k == n_sub - 1 else [])
        return jnp.concatenate(parts, axis=0) if len(parts) > 1 else parts[0]

    for k in range(n_sub):
        lo, hi = ext(k)
        skip = HALO if k == 0 else 0
        proj = (_dot(piece(hg_ref, hgp_ref, hgn_ref, k), woa_ref[...])
                + _dot(piece(rt_ref, rtp_ref, rtn_ref, k), wob_ref[...]))
        x1 = piece(x_ref, xp_ref, xn_ref, k) + g1_ref[0] * proj
        o_ref[0, k * ROW_TILE:(k + 1) * ROW_TILE, :] = x1[skip:skip + ROW_TILE]
        h2 = _rms(x1, n2_ref[...]) * (1.0 + sc2_ref[0]) + sh2_ref[0]
        h2_s[lo:hi, :] = h2.astype(BF16)

    prev_valid = (m > 0).astype(F32)
    next_valid = (m < pl.num_programs(1) - 1).astype(F32)
    edge_lo = slice(HALO - 8, HALO)
    edge_hi = slice(HALO + rows, HALO + rows + 8)

    gate_bufs = (gate2_s, gate3_s)
    up_bufs = (up2_s, up3_s)
    tiles = [(c0, min(FF_TILE, D_FF - c0)) for c0 in range(0, D_FF, FF_TILE)]

    def gate_up(j):
        c0, width = tiles[j]
        gate_s, up_s = gate_bufs[j % 2], up_bufs[j % 2]
        w_gu = jnp.concatenate([wup_ref[:, c0:c0 + width], wup_ref[:, D_FF + c0:D_FF + c0 + width]], axis=1)
        for k in range(n_sub):
            lo, hi = ext(k)
            skip = HALO if k == 0 else 0
            gu = _dot(h2_s[lo:hi, :], w_gu)
            gate_s[lo:hi, :width] = gu[:, :width]
            up_s[k * ROW_TILE:(k + 1) * ROW_TILE, :width] = gu[skip:skip + ROW_TILE, width:]
        gate_s[edge_lo, :width] = gate_s[edge_lo, :width] * prev_valid
        gate_s[edge_hi, :width] = gate_s[edge_hi, :width] * next_valid

    gate_up(0)
    for j, (c0, width) in enumerate(tiles):
        if j + 1 < len(tiles):
            gate_up(j + 1)
        cols = slice(c0, c0 + width)
        gate_s, up_s = gate_bufs[j % 2], up_bufs[j % 2]
        for k in range(n_sub):
            sub = slice(k * ROW_TILE, (k + 1) * ROW_TILE)
            base = HALO + k * ROW_TILE
            g = gate_s[base - 8:base + ROW_TILE + 8, :width]
            g_prev = pltpu.roll(g, 1, axis=0)[8:8 + ROW_TILE]
            g_next = pltpu.roll(g, ROW_TILE + 15, axis=0)[8:8 + ROW_TILE]
            conv = (g_prev * cw_ref[0:1, cols] + g[8:8 + ROW_TILE] * cw_ref[1:2, cols]
                    + g_next * cw_ref[2:3, cols] + cb_ref[:, cols])
            act = (conv * jax.nn.sigmoid(conv) * up_s[sub, :width]).astype(BF16)
            ffn = _dot(act, wd_ref[cols, :])
            if j > 0:
                ffn += acc_s[sub, :]
            if j < len(tiles) - 1:
                acc_s[sub, :] = ffn
            else:
                x2 = o_ref[0, sub, :] + g2_ref[0] * ffn
                o_ref[0, sub, :] = _rms(x2, fin_ref[...])


def _out_ffn(x, hgo, reto, mod3, w_out_bf16, norm2_g, final_g, w_up_bf16, conv_w, conv_b, w_down_bf16):
    batch, seq, _ = x.shape
    rows = min(FFN_ROWS, seq)
    n_row_tiles = seq // rows
    halo_per_tile = rows // HALO
    n_halo_blocks = seq // HALO
    once = pl.Buffered(1)

    def prev_block(m):
        return jnp.maximum(m * halo_per_tile - 1, 0)

    def next_block(m):
        return jnp.minimum((m + 1) * halo_per_tile, n_halo_blocks - 1)

    x_main = pl.BlockSpec((1, rows, D_MODEL), lambda b, m: (b, m, 0))
    x_prev = pl.BlockSpec((1, HALO, D_MODEL), lambda b, m: (b, prev_block(m), 0))
    x_next = pl.BlockSpec((1, HALO, D_MODEL), lambda b, m: (b, next_block(m), 0))
    h_main = pl.BlockSpec((1, HEADS, rows, DH), lambda b, m: (b, 0, m, 0))
    h_prev = pl.BlockSpec((1, HEADS, HALO, DH), lambda b, m: (b, 0, prev_block(m), 0))
    h_next = pl.BlockSpec((1, HEADS, HALO, DH), lambda b, m: (b, 0, next_block(m), 0))

    def mod_spec(chunk):
        return pl.BlockSpec((1, 1, D_MODEL), lambda b, m: (b, 0, chunk))

    def const(shape, index=(0, 0)):
        return pl.BlockSpec(shape, lambda b, m: index, pipeline_mode=once)

    return pl.pallas_call(
        _ffn_kernel,
        grid=(batch, n_row_tiles),
        in_specs=[
            x_main, x_prev, x_next,
            h_main, h_prev, h_next,
            h_main, h_prev, h_next,
            mod_spec(2), mod_spec(3), mod_spec(4), mod_spec(5),
            const((GROUP_W, D_MODEL), (0, 0)),
            const((GROUP_W, D_MODEL), (1, 0)),
            const((1, D_MODEL)), const((1, D_MODEL)),
            const((D_MODEL, 2 * D_FF)),
            const((3, D_FF)), const((1, D_FF)),
            const((D_FF, D_MODEL)),
        ],
        out_specs=pl.BlockSpec((1, rows, D_MODEL), lambda b, m: (b, m, 0)),
        out_shape=jax.ShapeDtypeStruct((batch, seq, D_MODEL), F32),
        scratch_shapes=[
            pltpu.VMEM((rows + 2 * HALO, D_MODEL), BF16),
            pltpu.VMEM((rows + 2 * HALO, FF_TILE), F32),
            pltpu.VMEM((rows + 2 * HALO, FF_TILE), F32),
            pltpu.VMEM((rows, FF_TILE), F32),
            pltpu.VMEM((rows, FF_TILE), F32),
            pltpu.VMEM((rows, D_MODEL), F32),
        ],
        compiler_params=pltpu.CompilerParams(
            dimension_semantics=("parallel", "parallel"), vmem_limit_bytes=VMEM_LIMIT),
        name="out_proj_ffn",
    )(x, x, x, hgo, hgo, hgo, reto, reto, reto, mod3, mod3, mod3, mod3,
      w_out_bf16, w_out_bf16, norm2_g, final_g, w_up_bf16, conv_w, conv_b, w_down_bf16)


def _rope_tables(seq):
    quarter = DH // 4
    freqs = np.float32(ROPE_THETA) ** (-np.arange(quarter, dtype=np.float32) / np.float32(quarter))
    t = np.arange(seq)
    ang_r = (t // GRID_W).astype(np.float32)[:, None] * freqs[None, :]
    ang_c = (t % GRID_W).astype(np.float32)[:, None] * freqs[None, :]
    cos = np.concatenate([np.cos(ang_r)] * 2 + [np.cos(ang_c)] * 2, axis=-1)
    sin = np.concatenate([-np.sin(ang_r), np.sin(ang_r), -np.sin(ang_c), np.sin(ang_c)], axis=-1)
    return jnp.asarray(cos, F32), jnp.asarray(sin, F32)


def kernel(x, c, ctx, c_ctx, w_mod, b_mod, norm1_g, w_in, hgrn_lb, hgrn_norm_g, ret_decay,
           ret_norm_g, w_out, norm2_g, w_up, conv_w, conv_b, w_down, final_g):
    batch, seq, _ = x.shape
    n_ctx = ctx.shape[1]
    assert w_mod.shape[0] == 1, "single-layer block"
    assert seq % CHUNK == 0 and seq % ROW_TILE == 0 and (n_ctx & (n_ctx - 1)) == 0

    lb = jnp.cumsum(jax.nn.softmax(hgrn_lb.astype(F32), axis=1), axis=1)[:, 0]
    log_gamma = jax.nn.log_sigmoid(ret_decay[0].astype(F32))
    cos_t, sin_t = _rope_tables(seq)

    n_mod_rows = -(-(batch + 1) // 8) * 8
    c_rows = jnp.zeros((n_mod_rows, D_MODEL), F32).at[:batch].set(c).at[batch].set(c_ctx)
    mod = _modulation(c_rows, w_mod[0], b_mod[0][None, :])
    mod3 = mod.reshape(n_mod_rows, 1, 6 * D_MODEL)

    w_in_b = w_in[0].astype(BF16)
    norm1 = norm1_g[0][None, :]
    states = _context_states(ctx, mod3[batch:batch + 1], norm1, w_in_b, lb, log_gamma)
    p, phi = _in_projection(x.reshape(batch * seq, D_MODEL), mod3[:batch], seq, norm1, w_in_b,
                            cos_t, sin_t, lb)
    hgo, reto = _mixer(p, phi, log_gamma,
                       hgrn_norm_g[0][None, :], ret_norm_g[0][None, :], states)

    return _out_ffn(x, hgo, reto, mod3[:batch], w_out[0].astype(BF16), norm2_g[0][None, :],
                    final_g[None, :], w_up[0].astype(BF16), conv_w[0], conv_b[0][None, :],
                    w_down[0].astype(BF16))
```

```python
import jax
import jax.numpy as jnp
import numpy as np
from jax import lax
from jax.experimental import pallas as pl
from jax.experimental.pallas import tpu as pltpu

F32 = jnp.float32
BF16 = jnp.bfloat16

D_MODEL = 1024
HEADS = 4
DH = 128
GROUP_W = HEADS * DH
N_GROUPS = 9
IN_WIDTH = N_GROUPS * GROUP_W
D_FF = 2816
GRID_W = 64
ROPE_THETA = 10000.0
EPS = 1e-6
LOG2E = 1.4426950408889634

CHUNK = 128
FF_TILE = 256
ROW_TILE = 512
IN_ROW_TILE = 1024
PASS_A_UNROLL = 16
PASS_C_UNROLL = 16
VMEM_LIMIT = 56 * 1024 * 1024


def _dot(a, b):
    return jnp.dot(a, b, preferred_element_type=F32)


def _dot_nt(a, b):
    return lax.dot_general(a, b, (((1,), (1,)), ((), ())), preferred_element_type=F32)


def _dot_tn(a, b):
    return lax.dot_general(a, b, (((0,), (0,)), ((), ())), preferred_element_type=F32)


def _rms(x, gain):
    return x * lax.rsqrt(jnp.mean(x * x, axis=-1, keepdims=True) + EPS) * gain


def _cumsum_rows(x):
    rows = x.shape[0]
    row = lax.broadcasted_iota(jnp.int32, x.shape, 0)
    shift = 1
    while shift < rows:
        x = x + jnp.where(row >= shift, pltpu.roll(x, shift, axis=0), 0.0)
        shift *= 2
    return x


def _mod_kernel(c_ref, w_ref, b_ref, o_ref):
    c = c_ref[...]
    a = (c * jax.nn.sigmoid(c)).astype(BF16)
    o_ref[...] = _dot(a, w_ref[...].astype(BF16)) + b_ref[...]


def _modulation(c_rows, w_mod, b_mod):
    n_rows = c_rows.shape[0]
    width = w_mod.shape[1]
    tile = D_MODEL
    return pl.pallas_call(
        _mod_kernel,
        grid=(width // tile,),
        in_specs=[
            pl.BlockSpec((n_rows, D_MODEL), lambda j: (0, 0)),
            pl.BlockSpec((D_MODEL, tile), lambda j: (0, j)),
            pl.BlockSpec((1, tile), lambda j: (0, j)),
        ],
        out_specs=pl.BlockSpec((n_rows, tile), lambda j: (0, j)),
        out_shape=jax.ShapeDtypeStruct((n_rows, width), F32),
        compiler_params=pltpu.CompilerParams(dimension_semantics=("parallel",)),
        name="modulation",
    )(c_rows, w_mod, b_mod)


G_HQ, G_HV, G_ZF, G_ZB, G_HG, G_RQ, G_RK, G_RV, G_RG = range(N_GROUPS)
GROUP_ORDER = (G_ZF, G_ZB, G_RQ, G_RK, G_HQ, G_HV, G_HG, G_RV, G_RG)


def _rope(t, cos, sin_signed, first_quarter):
    swapped = jnp.where(first_quarter, pltpu.roll(t, DH - DH // 4, axis=1),
                        pltpu.roll(t, DH // 4, axis=1))
    return t * cos + swapped * sin_signed


def _modulated_norm(x, gain, scale, shift):
    return (_rms(x, gain) * (1.0 + scale) + shift).astype(BF16)


def _forget(z, lb):
    f = lb + (1.0 - lb) * jax.nn.sigmoid(z)
    return f, jnp.log(f)


def _inproj_kernel(x_ref, sh_ref, sc_ref, g_ref, w_ref, cos_ref, sin_ref, lb_ref, o_ref, phi_ref, zf_s, zb_s):
    hb = _modulated_norm(x_ref[...], g_ref[...], sc_ref[0], sh_ref[0])
    n_rows = hb.shape[0]
    cos = cos_ref[...]
    sin = sin_ref[...]
    lane = lax.broadcasted_iota(jnp.int32, cos.shape, 1)
    first_quarter = (lane % (DH // 2)) < (DH // 4)

    def gate_tile(backward, c0, h, zero):
        hcols = slice(h * DH, (h + 1) * DH)
        z_s, lb, g = (zb_s, lb_ref[1:2, hcols], G_ZB) if backward else (zf_s, lb_ref[0:1, hcols], G_ZF)
        f, lf = _forget(z_s[c0:c0 + CHUNK, hcols], lb + zero)
        lf = lf * LOG2E
        o_ref[0, h, g, c0:c0 + CHUNK, :] = (1.0 - f).astype(BF16)
        csum = _cumsum_rows(lf)
        if backward:
            csum = csum[CHUNK - 1:CHUNK, :] - csum + lf
        phi_ref[0, h, 1 if backward else 0, c0:c0 + CHUNK, :] = csum

    zf_s[...] = _dot(hb, w_ref[:, G_ZF * GROUP_W:(G_ZF + 1) * GROUP_W])
    gate_work = [(d, c0, h) for d in (False, True) for c0 in range(0, n_rows, CHUNK) for h in range(HEADS)]
    hosts = [g for g in GROUP_ORDER if g != G_ZF]
    per_host = -(-len(gate_work) // len(hosts))
    for idx, g in enumerate(hosts):
        cols = slice(g * GROUP_W, (g + 1) * GROUP_W)
        r = _dot(hb, w_ref[:, cols])
        if g == G_ZB:
            zb_s[...] = r
        share = gate_work[idx * per_host:(idx + 1) * per_host]
        for t, item in enumerate(share):
            r0 = (t * n_rows // len(share)) // 8 * 8
            bits = pltpu.bitcast(r[r0:r0 + 8, 0:DH], jnp.uint32)
            gate_tile(*item, ((bits >> 16) >> 16).astype(F32)[0:1, :])
        if g == G_ZB:
            continue
        if g == G_RK:
            r = r * (DH ** -0.5)
        for h in range(HEADS):
            t = r[:, h * DH:(h + 1) * DH]
            if g in (G_RQ, G_RK):
                t = _rope(t, cos, sin, first_quarter)
            o_ref[0, h, g] = t.astype(BF16)


def _in_projection(x2d, mod3, seq, norm_g, w_in_bf16, cos_t, sin_t, lb):
    n_rows = x2d.shape[0]
    row_tile = min(IN_ROW_TILE, seq)
    tiles_per_seq = seq // row_tile

    def mod_spec(chunk):
        return pl.BlockSpec((1, 1, D_MODEL), lambda i: (i // tiles_per_seq, 0, chunk))

    table = pl.BlockSpec((row_tile, DH), lambda i: (i % tiles_per_seq, 0))
    return pl.pallas_call(
        _inproj_kernel,
        grid=(n_rows // row_tile,),
        in_specs=[
            pl.BlockSpec((row_tile, D_MODEL), lambda i: (i, 0)),
            mod_spec(0),
            mod_spec(1),
            pl.BlockSpec((1, D_MODEL), lambda i: (0, 0)),
            pl.BlockSpec((D_MODEL, IN_WIDTH), lambda i: (0, 0), pipeline_mode=pl.Buffered(1)),
            table, table,
            pl.BlockSpec((2, GROUP_W), lambda i: (0, 0)),
        ],
        out_specs=[pl.BlockSpec((1, HEADS, N_GROUPS, row_tile, DH),
                                lambda i: (i // tiles_per_seq, 0, 0, i % tiles_per_seq, 0)),
                   pl.BlockSpec((1, HEADS, 2, row_tile, DH),
                                lambda i: (i // tiles_per_seq, 0, 0, i % tiles_per_seq, 0))],
        out_shape=[jax.ShapeDtypeStruct((n_rows // seq, HEADS, N_GROUPS, seq, DH), BF16),
                   jax.ShapeDtypeStruct((n_rows // seq, HEADS, 2, seq, DH), F32)],
        scratch_shapes=[pltpu.VMEM((row_tile, GROUP_W), F32), pltpu.VMEM((row_tile, GROUP_W), F32)],
        compiler_params=pltpu.CompilerParams(
            dimension_semantics=("parallel",), vmem_limit_bytes=VMEM_LIMIT),
        name="in_projection",
    )(x2d, mod3, mod3, norm_g, w_in_bf16, cos_t, sin_t, lb)


def _ctx_kernel(lg_ref, c_ref, sh_ref, sc_ref, g_ref, w_ref, lb_ref, st_ref):
    n = c_ref.shape[1]
    hb = _modulated_norm(c_ref[0], g_ref[...], sc_ref[0], sh_ref[0])

    def group(g):
        return _dot(hb, w_ref[:, g * GROUP_W:(g + 1) * GROUP_W])

    v = group(G_HV).astype(BF16)
    ff, lff = _forget(group(G_ZF), lb_ref[0:1, :])
    fb, lfb = _forget(group(G_ZB), lb_ref[1:2, :])
    bf = _cumsum_rows(lff)
    bb = _cumsum_rows(lfb)
    kf = ((1.0 - ff) * jnp.exp(bf[n - 1:n, :] - bf)).astype(BF16)
    kb = ((1.0 - fb) * jnp.exp(bb - lfb)).astype(BF16)
    rk = group(G_RK) * (DH ** -0.5)
    rv = group(G_RV).astype(BF16)
    pos = lax.broadcasted_iota(jnp.int32, (n, DH), 0).astype(F32)
    for h in range(HEADS):
        cols = slice(h * DH, (h + 1) * DH)
        st_ref[0, h, 0] = _dot_tn(v[:, cols], kf[:, cols])
        st_ref[0, h, 1] = _dot_tn(v[:, cols], kb[:, cols])
        wf = jnp.exp((n - 1.0 - pos) * lg_ref[0, h])
        wb = jnp.exp(pos * lg_ref[1, h])
        st_ref[0, h, 2] = _dot_tn(rv[:, cols], (rk[:, cols] * wf).astype(BF16))
        st_ref[0, h, 3] = _dot_tn(rv[:, cols], (rk[:, cols] * wb).astype(BF16))


def _context_states(ctx, mod_ctx, norm_g, w_in_bf16, lb, log_gamma):
    batch, n_ctx, _ = ctx.shape

    def mod_spec(chunk):
        return pl.BlockSpec((1, 1, D_MODEL), lambda b: (0, 0, chunk))

    state = pl.BlockSpec((1, HEADS, 4, DH, DH), lambda b: (b, 0, 0, 0, 0))
    state_shape = jax.ShapeDtypeStruct((batch, HEADS, 4, DH, DH), F32)
    return pl.pallas_call(
        _ctx_kernel,
        grid=(batch,),
        in_specs=[
            pl.BlockSpec(memory_space=pltpu.SMEM),
            pl.BlockSpec((1, n_ctx, D_MODEL), lambda b: (b, 0, 0)),
            mod_spec(0), mod_spec(1),
            pl.BlockSpec((1, D_MODEL), lambda b: (0, 0)),
            pl.BlockSpec((D_MODEL, IN_WIDTH), lambda b: (0, 0), pipeline_mode=pl.Buffered(1)),
            pl.BlockSpec((2, GROUP_W), lambda b: (0, 0)),
        ],
        out_specs=state,
        out_shape=state_shape,
        compiler_params=pltpu.CompilerParams(
            dimension_semantics=("parallel",), vmem_limit_bytes=VMEM_LIMIT),
        name="context_states",
    )(log_gamma, ctx, mod_ctx, mod_ctx, norm_g, w_in_bf16, lb)


def _anchor(phi, half, forward):
    rows = phi.shape[0]
    pick = half - 1 if forward else half
    block = 2 * half
    p3 = phi.reshape(rows // 8, 8, DH)
    sub = lax.broadcasted_iota(jnp.int32, p3.shape, 1)
    out = None
    for start in range(0, 8, block):
        a = jnp.broadcast_to(p3[:, start + pick:start + pick + 1, :], p3.shape)
        out = a if out is None else jnp.where(sub >= start, a, out)
    return out.reshape(rows, DH)


def _level_operands(q, kf, kb, ff, fb, phif, phib, half, row):
    n = q.shape[0]
    if half == 1:
        odd = (row & 1) != 0
        zz = q * jnp.where(odd, ff, fb)
        ww = jnp.where(odd, kb, kf)
    elif half >= 8:
        zs, ws = [], []
        for b0 in range(0, n, 2 * half):
            first = slice(b0, b0 + half)
            second = slice(b0 + half, b0 + 2 * half)
            af = phif[b0 + half - 1:b0 + half, :]
            ab = phib[b0 + half:b0 + half + 1, :]
            zs += [q[first] * jnp.exp2(phib[first] - ab), q[second] * jnp.exp2(phif[second] - af)]
            ws += [kf[first] * jnp.exp2(af - phif[first]), kb[second] * jnp.exp2(ab - phib[second])]
        zz = jnp.concatenate(zs, axis=0)
        ww = jnp.concatenate(ws, axis=0)
    else:
        af = _anchor(phif, half, True)
        ab = _anchor(phib, half, False)
        second = (row & half) != 0
        zz = q * jnp.exp2(jnp.where(second, phif - af, phib - ab))
        ww = jnp.where(second, kb, kf) * jnp.exp2(jnp.where(second, ab - phib, af - phif))
    return zz.astype(BF16), ww.astype(BF16)


def _mixer_kernel(lg_ref, p_ref, phi_ref, norm_ref, st0_ref, o_ref,
                  oh_s, or_s, qcat_s, rqcat_s, sh_s, sr_s, shb_s, srb_s, dec_s):
    head = pl.program_id(1)
    seq = p_ref.shape[3]
    n_chunks = seq // CHUNK
    lgf = lg_ref[0, head]
    lgb = lg_ref[1, head]
    fwd = slice(0, DH)
    bwd = slice(DH, 2 * DH)

    row = lax.broadcasted_iota(jnp.int32, (CHUNK, DH), 0)
    t_idx = lax.broadcasted_iota(jnp.int32, (CHUNK, CHUNK), 0)
    s_idx = lax.broadcasted_iota(jnp.int32, (CHUNK, CHUNK), 1)
    split = t_idx ^ s_idx
    rel = (t_idx - s_idx).astype(F32)
    ret_decay = jnp.where(rel >= 0, jnp.exp(rel * lgf), 0.0) + jnp.where(rel <= 0, jnp.exp(-rel * lgb), 0.0)
    pos = row.astype(F32)
    qdec_f = jnp.exp((pos + 1.0) * lgf)
    kdec_f = jnp.exp((CHUNK - 1.0 - pos) * lgf)
    qdec_b = jnp.exp((CHUNK - pos) * lgb)
    kdec_b = jnp.exp(pos * lgb)

    def pass_a(i, carry):
        rows = pl.ds(pl.multiple_of(i * CHUNK, CHUNK), CHUNK)
        q = p_ref[0, 0, G_HQ, rows, :].astype(F32)
        v = p_ref[0, 0, G_HV, rows, :]
        kf = p_ref[0, 0, G_ZF, rows, :].astype(F32)
        kb = p_ref[0, 0, G_ZB, rows, :].astype(F32)
        ff = 1.0 - kf
        fb = 1.0 - kb
        phif = phi_ref[0, 0, 0, rows, :]
        phib = phi_ref[0, 0, 1, rows, :]

        attn = None
        half = CHUNK // 2
        while half >= 1:
            zz, ww = _level_operands(q, kf, kb, ff, fb, phif, phib, half, row)
            a = _dot_nt(zz, ww)
            attn = a if attn is None else jnp.where(split < 2 * half, a, attn)
            half //= 2
        attn = jnp.where(split == 0, 0.0, attn)
        diag = jnp.sum(q * (kf + kb), axis=-1, keepdims=True)
        oh_s[rows, :] = _dot(attn.astype(BF16), v) + diag * v.astype(F32)

        last_f = phif[CHUNK - 1:CHUNK, :]
        last_b = phib[0:1, :]
        qcat_s[rows, fwd] = (q * jnp.exp2(phif)).astype(BF16)
        qcat_s[rows, bwd] = (q * jnp.exp2(phib)).astype(BF16)
        kcat = jnp.concatenate([(kf * jnp.exp2(last_f - phif)).astype(BF16),
                                (kb * jnp.exp2(last_b - phib)).astype(BF16)], axis=1)
        sh_s[i] = _dot_tn(v, kcat)
        dec_s[i, :, fwd] = jnp.exp2(last_f)
        dec_s[i, :, bwd] = jnp.exp2(last_b)

        rqb = p_ref[0, 0, G_RQ, rows, :]
        rkb = p_ref[0, 0, G_RK, rows, :]
        rq = rqb.astype(F32)
        rk = rkb.astype(F32)
        rv = p_ref[0, 0, G_RV, rows, :]
        scores = _dot_nt(rqb, rkb) * ret_decay
        or_s[rows, :] = _dot(scores.astype(BF16), rv)
        rqcat_s[rows, fwd] = (rq * qdec_f).astype(BF16)
        rqcat_s[rows, bwd] = (rq * qdec_b).astype(BF16)
        rkcat = jnp.concatenate([(rk * kdec_f).astype(BF16), (rk * kdec_b).astype(BF16)], axis=1)
        sr_s[i] = _dot_tn(rv, rkcat)
        return carry

    lax.fori_loop(0, n_chunks, pass_a, 0, unroll=min(PASS_A_UNROLL, n_chunks))

    cdec_f = jnp.exp(CHUNK * lgf)
    cdec_b = jnp.exp(CHUNK * lgb)
    sf, rf = st0_ref[0, 0, 0], st0_ref[0, 0, 2]
    for i in range(n_chunks):
        shb_s[i, :, fwd] = sf.astype(BF16)
        srb_s[i, :, fwd] = rf.astype(BF16)
        sf = sf * dec_s[i, :, fwd] + sh_s[i, :, fwd]
        rf = rf * cdec_f + sr_s[i, :, fwd]
    sb, rb = st0_ref[0, 0, 1], st0_ref[0, 0, 3]
    for i in reversed(range(n_chunks)):
        shb_s[i, :, bwd] = sb.astype(BF16)
        srb_s[i, :, bwd] = rb.astype(BF16)
        sb = sb * dec_s[i, :, bwd] + sh_s[i, :, bwd]
        rb = rb * cdec_b + sr_s[i, :, bwd]

    def pass_c(i, carry):
        rows = pl.ds(pl.multiple_of(i * CHUNK, CHUNK), CHUNK)
        o = oh_s[rows, :] + _dot_nt(qcat_s[rows, :], shb_s[i])
        hg = o * jax.nn.sigmoid(p_ref[0, 0, G_HG, rows, :].astype(F32))
        o_ref[0, 0, 0, rows, :] = _rms(hg, norm_ref[0:1, :]).astype(BF16)
        y = or_s[rows, :] + _dot_nt(rqcat_s[rows, :], srb_s[i])
        rg = p_ref[0, 0, G_RG, rows, :].astype(F32)
        ret = _rms(y, norm_ref[1:2, :]) * (rg * jax.nn.sigmoid(rg))
        o_ref[0, 0, 1, rows, :] = ret.astype(BF16)
        return carry

    lax.fori_loop(0, n_chunks, pass_c, 0, unroll=min(PASS_C_UNROLL, n_chunks))


def _mixer(p, phi, log_gamma, norms, states):
    batch, _, _, seq, _ = p.shape
    n_chunks = seq // CHUNK
    return pl.pallas_call(
        _mixer_kernel,
        grid=(batch, HEADS),
        in_specs=[
            pl.BlockSpec(memory_space=pltpu.SMEM),
            pl.BlockSpec((1, 1, N_GROUPS, seq, DH), lambda b, h: (b, h, 0, 0, 0)),
            pl.BlockSpec((1, 1, 2, seq, DH), lambda b, h: (b, h, 0, 0, 0)),
            pl.BlockSpec((2, DH), lambda b, h: (0, h)),
            pl.BlockSpec((1, 1, 4, DH, DH), lambda b, h: (b, h, 0, 0, 0)),
        ],
        out_specs=pl.BlockSpec((1, 1, 2, seq, DH), lambda b, h: (b, h, 0, 0, 0)),
        out_shape=jax.ShapeDtypeStruct((batch, HEADS, 2, seq, DH), BF16),
        scratch_shapes=[
            pltpu.VMEM((seq, DH), F32), pltpu.VMEM((seq, DH), F32),
            pltpu.VMEM((seq, 2 * DH), BF16), pltpu.VMEM((seq, 2 * DH), BF16),
            pltpu.VMEM((n_chunks, DH, 2 * DH), F32),
            pltpu.VMEM((n_chunks, DH, 2 * DH), F32),
            pltpu.VMEM((n_chunks, DH, 2 * DH), BF16),
            pltpu.VMEM((n_chunks, DH, 2 * DH), BF16),
            pltpu.VMEM((n_chunks, 1, 2 * DH), F32),
        ],
        compiler_params=pltpu.CompilerParams(
            dimension_semantics=("parallel", "parallel"), vmem_limit_bytes=VMEM_LIMIT),
        name="token_mixer",
    )(log_gamma, p, phi, norms, states)


FFN_ROWS = 1024
HALO = 16


def _ffn_kernel(x_ref, xp_ref, xn_ref, mx_ref, mxp_ref, mxn_ref,
                g1_ref, sh2_ref, sc2_ref, g2_ref, woa_ref, wob_ref, n2_ref, fin_ref,
                wup_ref, cw_ref, cb_ref, wd_ref,
                o_ref, h2_s, gate2_s, gate3_s, up2_s, up3_s, acc_s):
    m = pl.program_id(1)
    rows = x_ref.shape[1]
    n_sub = rows // ROW_TILE
    def ext(k):
        lo = 0 if k == 0 else HALO + k * ROW_TILE
        hi = HALO + (k + 1) * ROW_TILE + (HALO if k == n_sub - 1 else 0)
        return lo, hi

    def piece(main_ref, prev_ref, next_ref, k, mixer=None):
        def rows_of(ref, sl):
            if mixer is None:
                return ref[0, sl, :]
            return jnp.concatenate([ref[0, h, mixer, sl, :] for h in range(HEADS)], axis=1)

        parts = ([rows_of(prev_ref, slice(None))] if k == 0 else []) \
            + [rows_of(main_ref, slice(k * ROW_TILE, (k + 1) * ROW_TILE))] \
            + ([rows_of(next_ref, slice(None))] if k == n_sub - 1 else [])
        return jnp.concatenate(parts, axis=0) if len(parts) > 1 else parts[0]

    for k in range(n_sub):
        lo, hi = ext(k)
        skip = HALO if k == 0 else 0
        proj = (_dot(piece(mx_ref, mxp_ref, mxn_ref, k, mixer=0), woa_ref[...])
                + _dot(piece(mx_ref, mxp_ref, mxn_ref, k, mixer=1), wob_ref[...]))
        x1 = piece(x_ref, xp_ref, xn_ref, k) + g1_ref[0] * proj
        o_ref[0, k * ROW_TILE:(k + 1) * ROW_TILE, :] = x1[skip:skip + ROW_TILE]
        h2 = _rms(x1, n2_ref[...]) * (1.0 + sc2_ref[0]) + sh2_ref[0]
        h2_s[lo:hi, :] = h2.astype(BF16)

    prev_valid = (m > 0).astype(F32)
    next_valid = (m < pl.num_programs(1) - 1).astype(F32)
    edge_lo = slice(HALO - 8, HALO)
    edge_hi = slice(HALO + rows, HALO + rows + 8)

    gate_bufs = (gate2_s, gate3_s)
    up_bufs = (up2_s, up3_s)
    tiles = [(c0, min(FF_TILE, D_FF - c0)) for c0 in range(0, D_FF, FF_TILE)]

    def gate_up(j):
        c0, width = tiles[j]
        gate_s, up_s = gate_bufs[j % 2], up_bufs[j % 2]
        w_gu = jnp.concatenate([wup_ref[:, c0:c0 + width], wup_ref[:, D_FF + c0:D_FF + c0 + width]], axis=1)
        for k in range(n_sub):
            lo, hi = ext(k)
            skip = HALO if k == 0 else 0
            gu = _dot(h2_s[lo:hi, :], w_gu)
            gate_s[lo:hi, :width] = gu[:, :width]
            up_s[k * ROW_TILE:(k + 1) * ROW_TILE, :width] = gu[skip:skip + ROW_TILE, width:]
        gate_s[edge_lo, :width] = gate_s[edge_lo, :width] * prev_valid
        gate_s[edge_hi, :width] = gate_s[edge_hi, :width] * next_valid

    gate_up(0)
    for j, (c0, width) in enumerate(tiles):
        if j + 1 < len(tiles):
            gate_up(j + 1)
        cols = slice(c0, c0 + width)
        gate_s, up_s = gate_bufs[j % 2], up_bufs[j % 2]
        for k in range(n_sub):
            sub = slice(k * ROW_TILE, (k + 1) * ROW_TILE)
            base = HALO + k * ROW_TILE
            g = gate_s[base - 8:base + ROW_TILE + 8, :width]
            g_prev = pltpu.roll(g, 1, axis=0)[8:8 + ROW_TILE]
            g_next = pltpu.roll(g, ROW_TILE + 15, axis=0)[8:8 + ROW_TILE]
            conv = (g_prev * cw_ref[0:1, cols] + g[8:8 + ROW_TILE] * cw_ref[1:2, cols]
                    + g_next * cw_ref[2:3, cols] + cb_ref[:, cols])
            act = (conv * jax.nn.sigmoid(conv) * up_s[sub, :width]).astype(BF16)
            ffn = _dot(act, wd_ref[cols, :])
            if j > 0:
                ffn += acc_s[sub, :]
            if j < len(tiles) - 1:
                acc_s[sub, :] = ffn
            else:
                x2 = o_ref[0, sub, :] + g2_ref[0] * ffn
                o_ref[0, sub, :] = _rms(x2, fin_ref[...])


def _out_ffn(x, mixed, mod3, w_out_bf16, norm2_g, final_g, w_up_bf16, conv_w, conv_b, w_down_bf16):
    batch, seq, _ = x.shape
    rows = min(FFN_ROWS, seq)
    n_row_tiles = seq // rows
    halo_per_tile = rows // HALO
    n_halo_blocks = seq // HALO
    once = pl.Buffered(1)

    def prev_block(m):
        return jnp.maximum(m * halo_per_tile - 1, 0)

    def next_block(m):
        return jnp.minimum((m + 1) * halo_per_tile, n_halo_blocks - 1)

    x_main = pl.BlockSpec((1, rows, D_MODEL), lambda b, m: (b, m, 0))
    x_prev = pl.BlockSpec((1, HALO, D_MODEL), lambda b, m: (b, prev_block(m), 0))
    x_next = pl.BlockSpec((1, HALO, D_MODEL), lambda b, m: (b, next_block(m), 0))
    h_main = pl.BlockSpec((1, HEADS, 2, rows, DH), lambda b, m: (b, 0, 0, m, 0))
    h_prev = pl.BlockSpec((1, HEADS, 2, HALO, DH), lambda b, m: (b, 0, 0, prev_block(m), 0))
    h_next = pl.BlockSpec((1, HEADS, 2, HALO, DH), lambda b, m: (b, 0, 0, next_block(m), 0))

    def mod_spec(chunk):
        return pl.BlockSpec((1, 1, D_MODEL), lambda b, m: (b, 0, chunk))

    def const(shape, index=(0, 0)):
        return pl.BlockSpec(shape, lambda b, m: index, pipeline_mode=once)

    return pl.pallas_call(
        _ffn_kernel,
        grid=(batch, n_row_tiles),
        in_specs=[
            x_main, x_prev, x_next,
            h_main, h_prev, h_next,
            mod_spec(2), mod_spec(3), mod_spec(4), mod_spec(5),
            const((GROUP_W, D_MODEL), (0, 0)),
            const((GROUP_W, D_MODEL), (1, 0)),
            const((1, D_MODEL)), const((1, D_MODEL)),
            const((D_MODEL, 2 * D_FF)),
            const((3, D_FF)), const((1, D_FF)),
            const((D_FF, D_MODEL)),
        ],
        out_specs=pl.BlockSpec((1, rows, D_MODEL), lambda b, m: (b, m, 0)),
        out_shape=jax.ShapeDtypeStruct((batch, seq, D_MODEL), F32),
        scratch_shapes=[
            pltpu.VMEM((rows + 2 * HALO, D_MODEL), BF16),
            pltpu.VMEM((rows + 2 * HALO, FF_TILE), F32),
            pltpu.VMEM((rows + 2 * HALO, FF_TILE), F32),
            pltpu.VMEM((rows, FF_TILE), F32),
            pltpu.VMEM((rows, FF_TILE), F32),
            pltpu.VMEM((rows, D_MODEL), F32),
        ],
        compiler_params=pltpu.CompilerParams(
            dimension_semantics=("parallel", "parallel"), vmem_limit_bytes=VMEM_LIMIT),
        name="out_proj_ffn",
    )(x, x, x, mixed, mixed, mixed, mod3, mod3, mod3, mod3,
      w_out_bf16, w_out_bf16, norm2_g, final_g, w_up_bf16, conv_w, conv_b, w_down_bf16)


def _rope_tables(seq):
    quarter = DH // 4
    freqs = np.float32(ROPE_THETA) ** (-np.arange(quarter, dtype=np.float32) / np.float32(quarter))
    t = np.arange(seq)
    ang_r = (t // GRID_W).astype(np.float32)[:, None] * freqs[None, :]
    ang_c = (t % GRID_W).astype(np.float32)[:, None] * freqs[None, :]
    cos = np.concatenate([np.cos(ang_r)] * 2 + [np.cos(ang_c)] * 2, axis=-1)
    sin = np.concatenate([-np.sin(ang_r), np.sin(ang_r), -np.sin(ang_c), np.sin(ang_c)], axis=-1)
    return jnp.asarray(cos, F32), jnp.asarray(sin, F32)


def kernel(x, c, ctx, c_ctx, w_mod, b_mod, norm1_g, w_in, hgrn_lb, hgrn_norm_g, ret_decay,
           ret_norm_g, w_out, norm2_g, w_up, conv_w, conv_b, w_down, final_g):
    batch, seq, _ = x.shape
    n_ctx = ctx.shape[1]
    assert w_mod.shape[0] == 1, "single-layer block"
    assert seq % CHUNK == 0 and seq % ROW_TILE == 0 and (n_ctx & (n_ctx - 1)) == 0

    lb = jnp.cumsum(jax.nn.softmax(hgrn_lb.astype(F32), axis=1), axis=1)[:, 0]
    log_gamma = jax.nn.log_sigmoid(ret_decay[0].astype(F32))
    cos_t, sin_t = _rope_tables(seq)

    n_mod_rows = -(-(batch + 1) // 8) * 8
    c_rows = jnp.zeros((n_mod_rows, D_MODEL), F32).at[:batch].set(c).at[batch].set(c_ctx)
    mod = _modulation(c_rows, w_mod[0], b_mod[0][None, :])
    mod3 = mod.reshape(n_mod_rows, 1, 6 * D_MODEL)

    w_in_b = w_in[0].astype(BF16)
    norm1 = norm1_g[0][None, :]
    states = _context_states(ctx, mod3[batch:batch + 1], norm1, w_in_b, lb, log_gamma)
    p, phi = _in_projection(x.reshape(batch * seq, D_MODEL), mod3[:batch], seq, norm1, w_in_b,
                            cos_t, sin_t, lb)
    mixed = _mixer(p, phi, log_gamma, jnp.stack([hgrn_norm_g[0], ret_norm_g[0]]), states)

    return _out_ffn(x, mixed, mod3[:batch], w_out[0].astype(BF16), norm2_g[0][None, :],
                    final_g[None, :], w_up[0].astype(BF16), conv_w[0], conv_b[0][None, :],
                    w_down[0].astype(BF16))
```

```python
import jax
import jax.numpy as jnp
import numpy as np
from jax import lax
from jax.experimental import pallas as pl
from jax.experimental.pallas import tpu as pltpu

F32 = jnp.float32
BF16 = jnp.bfloat16

D_MODEL = 1024
HEADS = 4
DH = 128
GROUP_W = HEADS * DH
N_GROUPS = 9
IN_WIDTH = N_GROUPS * GROUP_W
D_FF = 2816
GRID_W = 64
ROPE_THETA = 10000.0
EPS = 1e-6
LOG2E = 1.4426950408889634

CHUNK = 128
FF_TILE = 256
ROW_TILE = 512
IN_ROW_TILE = 1024
PASS_A_UNROLL = 16
PASS_C_UNROLL = 16
VMEM_LIMIT = 56 * 1024 * 1024


def _dot(a, b):
    return jnp.dot(a, b, preferred_element_type=F32)


def _dot_nt(a, b):
    return lax.dot_general(a, b, (((1,), (1,)), ((), ())), preferred_element_type=F32)


def _dot_tn(a, b):
    return lax.dot_general(a, b, (((0,), (0,)), ((), ())), preferred_element_type=F32)


def _rms(x, gain):
    return x * lax.rsqrt(jnp.mean(x * x, axis=-1, keepdims=True) + EPS) * gain


def _cumsum_rows(x):
    rows = x.shape[0]
    row = lax.broadcasted_iota(jnp.int32, x.shape, 0)
    shift = 1
    while shift < rows:
        x = x + jnp.where(row >= shift, pltpu.roll(x, shift, axis=0), 0.0)
        shift *= 2
    return x


def _mod_kernel(c_ref, w_ref, b_ref, o_ref):
    c = c_ref[...]
    a = (c * jax.nn.sigmoid(c)).astype(BF16)
    o_ref[...] = _dot(a, w_ref[...].astype(BF16)) + b_ref[...]


def _modulation(c_rows, w_mod, b_mod):
    n_rows = c_rows.shape[0]
    width = w_mod.shape[1]
    tile = D_MODEL
    return pl.pallas_call(
        _mod_kernel,
        grid=(width // tile,),
        in_specs=[
            pl.BlockSpec((n_rows, D_MODEL), lambda j: (0, 0)),
            pl.BlockSpec((D_MODEL, tile), lambda j: (0, j)),
            pl.BlockSpec((1, tile), lambda j: (0, j)),
        ],
        out_specs=pl.BlockSpec((n_rows, tile), lambda j: (0, j)),
        out_shape=jax.ShapeDtypeStruct((n_rows, width), F32),
        compiler_params=pltpu.CompilerParams(dimension_semantics=("parallel",)),
        name="modulation",
    )(c_rows, w_mod, b_mod)


G_HQ, G_HV, G_ZF, G_ZB, G_HG, G_RQ, G_RK, G_RV, G_RG = range(N_GROUPS)
GROUP_ORDER = (G_ZF, G_ZB, G_RQ, G_RK, G_HQ, G_HV, G_HG, G_RV, G_RG)


def _rope(t, cos, sin_signed, first_quarter):
    swapped = jnp.where(first_quarter, pltpu.roll(t, DH - DH // 4, axis=1),
                        pltpu.roll(t, DH // 4, axis=1))
    return t * cos + swapped * sin_signed


def _modulated_norm(x, gain, scale, shift):
    return (_rms(x, gain) * (1.0 + scale) + shift).astype(BF16)


def _forget(z, lb):
    f = lb + (1.0 - lb) * jax.nn.sigmoid(z)
    return f, jnp.log(f)


def _inproj_kernel(x_ref, sh_ref, sc_ref, g_ref, w_ref, cos_ref, sin_ref, lb_ref, o_ref, phi_ref, zf_s, zb_s):
    hb = _modulated_norm(x_ref[...], g_ref[...], sc_ref[0], sh_ref[0])
    n_rows = hb.shape[0]
    cos = cos_ref[...]
    sin = sin_ref[...]
    lane = lax.broadcasted_iota(jnp.int32, cos.shape, 1)
    first_quarter = (lane % (DH // 2)) < (DH // 4)

    def gate_tile(backward, c0, h, zero):
        hcols = slice(h * DH, (h + 1) * DH)
        z_s, lb, g = (zb_s, lb_ref[1:2, hcols], G_ZB) if backward else (zf_s, lb_ref[0:1, hcols], G_ZF)
        f, lf = _forget(z_s[c0:c0 + CHUNK, hcols], lb + zero)
        lf = lf * LOG2E
        o_ref[0, g * HEADS + h, c0:c0 + CHUNK, :] = (1.0 - f).astype(BF16)
        csum = _cumsum_rows(lf)
        if backward:
            csum = csum[CHUNK - 1:CHUNK, :] - csum + lf
        phi_ref[0, (HEADS if backward else 0) + h, c0:c0 + CHUNK, :] = csum

    zf_s[...] = _dot(hb, w_ref[:, G_ZF * GROUP_W:(G_ZF + 1) * GROUP_W])
    gate_work = [(d, c0, h) for d in (False, True) for c0 in range(0, n_rows, CHUNK) for h in range(HEADS)]
    hosts = [g for g in GROUP_ORDER if g != G_ZF]
    per_host = -(-len(gate_work) // len(hosts))
    for idx, g in enumerate(hosts):
        cols = slice(g * GROUP_W, (g + 1) * GROUP_W)
        r = _dot(hb, w_ref[:, cols])
        if g == G_ZB:
            zb_s[...] = r
        share = gate_work[idx * per_host:(idx + 1) * per_host]
        for t, item in enumerate(share):
            r0 = (t * n_rows // len(share)) // 8 * 8
            bits = pltpu.bitcast(r[r0:r0 + 8, 0:DH], jnp.uint32)
            gate_tile(*item, ((bits >> 16) >> 16).astype(F32)[0:1, :])
        if g == G_ZB:
            continue
        if g == G_RK:
            r = r * (DH ** -0.5)
        for h in range(HEADS):
            t = r[:, h * DH:(h + 1) * DH]
            if g in (G_RQ, G_RK):
                t = _rope(t, cos, sin, first_quarter)
            o_ref[0, g * HEADS + h] = t.astype(BF16)


def _in_projection(x2d, mod3, seq, norm_g, w_in_bf16, cos_t, sin_t, lb):
    n_rows = x2d.shape[0]
    row_tile = min(IN_ROW_TILE, seq)
    tiles_per_seq = seq // row_tile

    def mod_spec(chunk):
        return pl.BlockSpec((1, 1, D_MODEL), lambda i: (i // tiles_per_seq, 0, chunk))

    table = pl.BlockSpec((row_tile, DH), lambda i: (i % tiles_per_seq, 0))
    return pl.pallas_call(
        _inproj_kernel,
        grid=(n_rows // row_tile,),
        in_specs=[
            pl.BlockSpec((row_tile, D_MODEL), lambda i: (i, 0)),
            mod_spec(0),
            mod_spec(1),
            pl.BlockSpec((1, D_MODEL), lambda i: (0, 0)),
            pl.BlockSpec((D_MODEL, IN_WIDTH), lambda i: (0, 0), pipeline_mode=pl.Buffered(1)),
            table, table,
            pl.BlockSpec((2, GROUP_W), lambda i: (0, 0)),
        ],
        out_specs=[pl.BlockSpec((1, N_GROUPS * HEADS, row_tile, DH),
                                lambda i: (i // tiles_per_seq, 0, i % tiles_per_seq, 0)),
                   pl.BlockSpec((1, 2 * HEADS, row_tile, DH),
                                lambda i: (i // tiles_per_seq, 0, i % tiles_per_seq, 0))],
        out_shape=[jax.ShapeDtypeStruct((n_rows // seq, N_GROUPS * HEADS, seq, DH), BF16),
                   jax.ShapeDtypeStruct((n_rows // seq, 2 * HEADS, seq, DH), F32)],
        scratch_shapes=[pltpu.VMEM((row_tile, GROUP_W), F32), pltpu.VMEM((row_tile, GROUP_W), F32)],
        compiler_params=pltpu.CompilerParams(
            dimension_semantics=("parallel",), vmem_limit_bytes=VMEM_LIMIT),
        name="in_projection",
    )(x2d, mod3, mod3, norm_g, w_in_bf16, cos_t, sin_t, lb)


def _ctx_kernel(lg_ref, c_ref, sh_ref, sc_ref, g_ref, w_ref, lb_ref,
                sf_ref, sb_ref, rf_ref, rb_ref):
    n = c_ref.shape[1]
    hb = _modulated_norm(c_ref[0], g_ref[...], sc_ref[0], sh_ref[0])

    def group(g):
        return _dot(hb, w_ref[:, g * GROUP_W:(g + 1) * GROUP_W])

    v = group(G_HV).astype(BF16)
    ff, lff = _forget(group(G_ZF), lb_ref[0:1, :])
    fb, lfb = _forget(group(G_ZB), lb_ref[1:2, :])
    bf = _cumsum_rows(lff)
    bb = _cumsum_rows(lfb)
    kf = ((1.0 - ff) * jnp.exp(bf[n - 1:n, :] - bf)).astype(BF16)
    kb = ((1.0 - fb) * jnp.exp(bb - lfb)).astype(BF16)
    rk = group(G_RK) * (DH ** -0.5)
    rv = group(G_RV).astype(BF16)
    pos = lax.broadcasted_iota(jnp.int32, (n, DH), 0).astype(F32)
    for h in range(HEADS):
        cols = slice(h * DH, (h + 1) * DH)
        sf_ref[0, h] = _dot_tn(v[:, cols], kf[:, cols])
        sb_ref[0, h] = _dot_tn(v[:, cols], kb[:, cols])
        wf = jnp.exp((n - 1.0 - pos) * lg_ref[0, h])
        wb = jnp.exp(pos * lg_ref[1, h])
        rf_ref[0, h] = _dot_tn(rv[:, cols], (rk[:, cols] * wf).astype(BF16))
        rb_ref[0, h] = _dot_tn(rv[:, cols], (rk[:, cols] * wb).astype(BF16))


def _context_states(ctx, mod_ctx, norm_g, w_in_bf16, lb, log_gamma):
    batch, n_ctx, _ = ctx.shape

    def mod_spec(chunk):
        return pl.BlockSpec((1, 1, D_MODEL), lambda b: (0, 0, chunk))

    state = pl.BlockSpec((1, HEADS, DH, DH), lambda b: (b, 0, 0, 0))
    state_shape = jax.ShapeDtypeStruct((batch, HEADS, DH, DH), F32)
    return pl.pallas_call(
        _ctx_kernel,
        grid=(batch,),
        in_specs=[
            pl.BlockSpec(memory_space=pltpu.SMEM),
            pl.BlockSpec((1, n_ctx, D_MODEL), lambda b: (b, 0, 0)),
            mod_spec(0), mod_spec(1),
            pl.BlockSpec((1, D_MODEL), lambda b: (0, 0)),
            pl.BlockSpec((D_MODEL, IN_WIDTH), lambda b: (0, 0), pipeline_mode=pl.Buffered(1)),
            pl.BlockSpec((2, GROUP_W), lambda b: (0, 0)),
        ],
        out_specs=[state, state, state, state],
        out_shape=[state_shape] * 4,
        compiler_params=pltpu.CompilerParams(
            dimension_semantics=("parallel",), vmem_limit_bytes=VMEM_LIMIT),
        name="context_states",
    )(log_gamma, ctx, mod_ctx, mod_ctx, norm_g, w_in_bf16, lb)


def _anchor(phi, half, forward):
    rows = phi.shape[0]
    pick = half - 1 if forward else half
    block = 2 * half
    p3 = phi.reshape(rows // 8, 8, DH)
    sub = lax.broadcasted_iota(jnp.int32, p3.shape, 1)
    out = None
    for start in range(0, 8, block):
        a = jnp.broadcast_to(p3[:, start + pick:start + pick + 1, :], p3.shape)
        out = a if out is None else jnp.where(sub >= start, a, out)
    return out.reshape(rows, DH)


def _level_operands(q, kf, kb, ff, fb, phif, phib, half, row):
    n = q.shape[0]
    if half == 1:
        odd = (row & 1) != 0
        zz = q * jnp.where(odd, ff, fb)
        ww = jnp.where(odd, kb, kf)
    elif half >= 8:
        zs, ws = [], []
        for b0 in range(0, n, 2 * half):
            first = slice(b0, b0 + half)
            second = slice(b0 + half, b0 + 2 * half)
            af = phif[b0 + half - 1:b0 + half, :]
            ab = phib[b0 + half:b0 + half + 1, :]
            zs += [q[first] * jnp.exp2(phib[first] - ab), q[second] * jnp.exp2(phif[second] - af)]
            ws += [kf[first] * jnp.exp2(af - phif[first]), kb[second] * jnp.exp2(ab - phib[second])]
        zz = jnp.concatenate(zs, axis=0)
        ww = jnp.concatenate(ws, axis=0)
    else:
        af = _anchor(phif, half, True)
        ab = _anchor(phib, half, False)
        second = (row & half) != 0
        zz = q * jnp.exp2(jnp.where(second, phif - af, phib - ab))
        ww = jnp.where(second, kb, kf) * jnp.exp2(jnp.where(second, ab - phib, af - phif))
    return zz.astype(BF16), ww.astype(BF16)


HEADS_PER_STEP = 2


def _mixer_kernel(lg_ref, *refs):
    n_seq, n_norm, n_state_out = 11, 2, 6
    for hh in range(HEADS_PER_STEP):
        one = slice(hh, hh + 1)
        views = ([r.at[:, one] for r in refs[:n_seq]]
                 + [r.at[:, hh * DH:(hh + 1) * DH] for r in refs[n_seq:n_seq + n_norm]]
                 + [r.at[:, one] for r in refs[n_seq + n_norm:n_seq + n_norm + n_state_out]])
        _mixer_head(HEADS_PER_STEP * pl.program_id(1) + hh, lg_ref, *views,
                    *refs[n_seq + n_norm + n_state_out:])


def _mixer_head(head, lg_ref, q_ref, v_ref, kf_ref, kb_ref, og_ref, rq_ref, rk_ref, rv_ref, rg_ref,
                phif_ref, phib_ref, hgn_ref, rn_ref,
                sf0_ref, sb0_ref, rf0_ref, rb0_ref,
                hgo_ref, reto_ref,
                oh_s, or_s, qcat_s, rqcat_s, sh_s, sr_s, shb_s, srb_s, dec_s):
    seq = q_ref.shape[2]
    n_chunks = seq // CHUNK
    lgf = lg_ref[0, head]
    lgb = lg_ref[1, head]
    fwd = slice(0, DH)
    bwd = slice(DH, 2 * DH)

    row = lax.broadcasted_iota(jnp.int32, (CHUNK, DH), 0)
    t_idx = lax.broadcasted_iota(jnp.int32, (CHUNK, CHUNK), 0)
    s_idx = lax.broadcasted_iota(jnp.int32, (CHUNK, CHUNK), 1)
    split = t_idx ^ s_idx
    rel = (t_idx - s_idx).astype(F32)
    ret_decay = jnp.where(rel >= 0, jnp.exp(rel * lgf), 0.0) + jnp.where(rel <= 0, jnp.exp(-rel * lgb), 0.0)
    pos = row.astype(F32)
    qdec_f = jnp.exp((pos + 1.0) * lgf)
    kdec_f = jnp.exp((CHUNK - 1.0 - pos) * lgf)
    qdec_b = jnp.exp((CHUNK - pos) * lgb)
    kdec_b = jnp.exp(pos * lgb)

    def pass_a(i, carry):
        rows = pl.ds(pl.multiple_of(i * CHUNK, CHUNK), CHUNK)
        q = q_ref[0, 0, rows, :].astype(F32)
        v = v_ref[0, 0, rows, :]
        kf = kf_ref[0, 0, rows, :].astype(F32)
        kb = kb_ref[0, 0, rows, :].astype(F32)
        ff = 1.0 - kf
        fb = 1.0 - kb
        phif = phif_ref[0, 0, rows, :]
        phib = phib_ref[0, 0, rows, :]

        attn = None
        half = CHUNK // 2
        while half >= 1:
            zz, ww = _level_operands(q, kf, kb, ff, fb, phif, phib, half, row)
            a = _dot_nt(zz, ww)
            attn = a if attn is None else jnp.where(split < 2 * half, a, attn)
            half //= 2
        attn = jnp.where(split == 0, 0.0, attn)
        diag = jnp.sum(q * (kf + kb), axis=-1, keepdims=True)
        oh_s[rows, :] = _dot(attn.astype(BF16), v) + diag * v.astype(F32)

        last_f = phif[CHUNK - 1:CHUNK, :]
        last_b = phib[0:1, :]
        qcat_s[rows, fwd] = (q * jnp.exp2(phif)).astype(BF16)
        qcat_s[rows, bwd] = (q * jnp.exp2(phib)).astype(BF16)
        kcat = jnp.concatenate([(kf * jnp.exp2(last_f - phif)).astype(BF16),
                                (kb * jnp.exp2(last_b - phib)).astype(BF16)], axis=1)
        sh_s[i] = _dot_tn(v, kcat)
        dec_s[i, :, fwd] = jnp.exp2(last_f)
        dec_s[i, :, bwd] = jnp.exp2(last_b)

        rqb = rq_ref[0, 0, rows, :]
        rkb = rk_ref[0, 0, rows, :]
        rq = rqb.astype(F32)
        rk = rkb.astype(F32)
        rv = rv_ref[0, 0, rows, :]
        scores = _dot_nt(rqb, rkb) * ret_decay
        or_s[rows, :] = _dot(scores.astype(BF16), rv)
        rqcat_s[rows, fwd] = (rq * qdec_f).astype(BF16)
        rqcat_s[rows, bwd] = (rq * qdec_b).astype(BF16)
        rkcat = jnp.concatenate([(rk * kdec_f).astype(BF16), (rk * kdec_b).astype(BF16)], axis=1)
        sr_s[i] = _dot_tn(rv, rkcat)
        return carry

    lax.fori_loop(0, n_chunks, pass_a, 0, unroll=min(PASS_A_UNROLL, n_chunks))

    cdec_f = jnp.exp(CHUNK * lgf)
    cdec_b = jnp.exp(CHUNK * lgb)
    sf, rf = sf0_ref[0, 0], rf0_ref[0, 0]
    for i in range(n_chunks):
        shb_s[i, :, fwd] = sf.astype(BF16)
        srb_s[i, :, fwd] = rf.astype(BF16)
        sf = sf * dec_s[i, :, fwd] + sh_s[i, :, fwd]
        rf = rf * cdec_f + sr_s[i, :, fwd]
    sb, rb = sb0_ref[0, 0], rb0_ref[0, 0]
    for i in reversed(range(n_chunks)):
        shb_s[i, :, bwd] = sb.astype(BF16)
        srb_s[i, :, bwd] = rb.astype(BF16)
        sb = sb * dec_s[i, :, bwd] + sh_s[i, :, bwd]
        rb = rb * cdec_b + sr_s[i, :, bwd]

    def pass_c(i, carry):
        rows = pl.ds(pl.multiple_of(i * CHUNK, CHUNK), CHUNK)
        o = oh_s[rows, :] + _dot_nt(qcat_s[rows, :], shb_s[i])
        hg = o * jax.nn.sigmoid(og_ref[0, 0, rows, :].astype(F32))
        hgo_ref[0, 0, rows, :] = _rms(hg, hgn_ref[...]).astype(BF16)
        y = or_s[rows, :] + _dot_nt(rqcat_s[rows, :], srb_s[i])
        rg = rg_ref[0, 0, rows, :].astype(F32)
        ret = _rms(y, rn_ref[...]) * (rg * jax.nn.sigmoid(rg))
        reto_ref[0, 0, rows, :] = ret.astype(BF16)
        return carry

    lax.fori_loop(0, n_chunks, pass_c, 0, unroll=min(PASS_C_UNROLL, n_chunks))


def _mixer(p, phi, log_gamma, hg_norm, ret_norm, states):
    batch, _, seq, _ = p.shape
    n_chunks = seq // CHUNK

    hps = HEADS_PER_STEP
    steps_per_group = HEADS // hps

    def group(g):
        return pl.BlockSpec((1, hps, seq, DH), lambda b, h: (b, g * steps_per_group + h, 0, 0))

    per_head_row = pl.BlockSpec((1, hps * DH), lambda b, h: (0, h))
    state = pl.BlockSpec((1, hps, DH, DH), lambda b, h: (b, h, 0, 0))
    out_spec = pl.BlockSpec((1, hps, seq, DH), lambda b, h: (b, h, 0, 0))
    out_shape = jax.ShapeDtypeStruct((batch, HEADS, seq, DH), BF16)
    return pl.pallas_call(
        _mixer_kernel,
        grid=(batch, steps_per_group),
        in_specs=[pl.BlockSpec(memory_space=pltpu.SMEM)]
        + [group(g) for g in range(N_GROUPS)]
        + [pl.BlockSpec((1, hps, seq, DH), lambda b, h: (b, h, 0, 0)),
           pl.BlockSpec((1, hps, seq, DH), lambda b, h: (b, steps_per_group + h, 0, 0)),
           per_head_row, per_head_row, state, state, state, state],
        out_specs=[out_spec, out_spec],
        out_shape=[out_shape, out_shape],
        scratch_shapes=[
            pltpu.VMEM((seq, DH), F32), pltpu.VMEM((seq, DH), F32),
            pltpu.VMEM((seq, 2 * DH), BF16), pltpu.VMEM((seq, 2 * DH), BF16),
            pltpu.VMEM((n_chunks, DH, 2 * DH), F32),
            pltpu.VMEM((n_chunks, DH, 2 * DH), F32),
            pltpu.VMEM((n_chunks, DH, 2 * DH), BF16),
            pltpu.VMEM((n_chunks, DH, 2 * DH), BF16),
            pltpu.VMEM((n_chunks, 1, 2 * DH), F32),
        ],
        compiler_params=pltpu.CompilerParams(
            dimension_semantics=("parallel", "parallel"), vmem_limit_bytes=VMEM_LIMIT),
        name="token_mixer",
    )(log_gamma, p, p, p, p, p, p, p, p, p, phi, phi, hg_norm, ret_norm, *states)


FFN_ROWS = 1024
HALO = 16


def _ffn_kernel(x_ref, xp_ref, xn_ref, hg_ref, hgp_ref, hgn_ref, rt_ref, rtp_ref, rtn_ref,
                g1_ref, sh2_ref, sc2_ref, g2_ref, woa_ref, wob_ref, n2_ref, fin_ref,
                wup_ref, cw_ref, cb_ref, wd_ref,
                o_ref, h2_s, gate2_s, gate3_s, up2_s, up3_s, acc_s):
    m = pl.program_id(1)
    rows = x_ref.shape[1]
    n_sub = rows // ROW_TILE
    def ext(k):
        lo = 0 if k == 0 else HALO + k * ROW_TILE
        hi = HALO + (k + 1) * ROW_TILE + (HALO if k == n_sub - 1 else 0)
        return lo, hi

    def piece(main_ref, prev_ref, next_ref, k):
        def rows_of(ref, sl):
            if len(ref.shape) == 3:
                return ref[0, sl, :]
            return jnp.concatenate([ref[0, h, sl, :] for h in range(HEADS)], axis=1)

        parts = ([rows_of(prev_ref, slice(None))] if k == 0 else []) \
            + [rows_of(main_ref, slice(k * ROW_TILE, (k + 1) * ROW_TILE))] \
            + ([rows_of(next_ref, slice(None))] if k == n_sub - 1 else [])
        return jnp.concatenate(parts, axis=0) if len(parts) > 1 else parts[0]

    for k in range(n_sub):
        lo, hi = ext(k)
        skip = HALO if k == 0 else 0
        proj = (_dot(piece(hg_ref, hgp_ref, hgn_ref, k), woa_ref[...])
                + _dot(piece(rt_ref, rtp_ref, rtn_ref, k), wob_ref[...]))
        x1 = piece(x_ref, xp_ref, xn_ref, k) + g1_ref[0] * proj
        o_ref[0, k * ROW_TILE:(k + 1) * ROW_TILE, :] = x1[skip:skip + ROW_TILE]
        h2 = _rms(x1, n2_ref[...]) * (1.0 + sc2_ref[0]) + sh2_ref[0]
        h2_s[lo:hi, :] = h2.astype(BF16)

    prev_valid = (m > 0).astype(F32)
    next_valid = (m < pl.num_programs(1) - 1).astype(F32)
    edge_lo = slice(HALO - 8, HALO)
    edge_hi = slice(HALO + rows, HALO + rows + 8)

    gate_bufs = (gate2_s, gate3_s)
    up_bufs = (up2_s, up3_s)
    tiles = [(c0, min(FF_TILE, D_FF - c0)) for c0 in range(0, D_FF, FF_TILE)]

    def gate_up(j):
        c0, width = tiles[j]
        gate_s, up_s = gate_bufs[j % 2], up_bufs[j % 2]
        w_gu = jnp.concatenate([wup_ref[:, c0:c0 + width], wup_ref[:, D_FF + c0:D_FF + c0 + width]], axis=1)
        for k in range(n_sub):
            lo, hi = ext(k)
            skip = HALO if k == 0 else 0
            gu = _dot(h2_s[lo:hi, :], w_gu)
            gate_s[lo:hi, :width] = gu[:, :width]
            up_s[k * ROW_TILE:(k + 1) * ROW_TILE, :width] = gu[skip:skip + ROW_TILE, width:]
        gate_s[edge_lo, :width] = gate_s[edge_lo, :width] * prev_valid
        gate_s[edge_hi, :width] = gate_s[edge_hi, :width] * next_valid

    gate_up(0)
    for j, (c0, width) in enumerate(tiles):
        if j + 1 < len(tiles):
            gate_up(j + 1)
        cols = slice(c0, c0 + width)
        gate_s, up_s = gate_bufs[j % 2], up_bufs[j % 2]
        for k in range(n_sub):
            sub = slice(k * ROW_TILE, (k + 1) * ROW_TILE)
            base = HALO + k * ROW_TILE
            g = gate_s[base - 8:base + ROW_TILE + 8, :width]
            g_prev = pltpu.roll(g, 1, axis=0)[8:8 + ROW_TILE]
            g_next = pltpu.roll(g, ROW_TILE + 15, axis=0)[8:8 + ROW_TILE]
            conv = (g_prev * cw_ref[0:1, cols] + g[8:8 + ROW_TILE] * cw_ref[1:2, cols]
                    + g_next * cw_ref[2:3, cols] + cb_ref[:, cols])
            act = (conv * jax.nn.sigmoid(conv) * up_s[sub, :width]).astype(BF16)
            ffn = _dot(act, wd_ref[cols, :])
            if j > 0:
                ffn += acc_s[sub, :]
            if j < len(tiles) - 1:
                acc_s[sub, :] = ffn
            else:
                x2 = o_ref[0, sub, :] + g2_ref[0] * ffn
                o_ref[0, sub, :] = _rms(x2, fin_ref[...])


def _out_ffn(x, hgo, reto, mod3, w_out_bf16, norm2_g, final_g, w_up_bf16, conv_w, conv_b, w_down_bf16):
    batch, seq, _ = x.shape
    rows = min(FFN_ROWS, seq)
    n_row_tiles = seq // rows
    halo_per_tile = rows // HALO
    n_halo_blocks = seq // HALO
    once = pl.Buffered(1)

    def prev_block(m):
        return jnp.maximum(m * halo_per_tile - 1, 0)

    def next_block(m):
        return jnp.minimum((m + 1) * halo_per_tile, n_halo_blocks - 1)

    x_main = pl.BlockSpec((1, rows, D_MODEL), lambda b, m: (b, m, 0))
    x_prev = pl.BlockSpec((1, HALO, D_MODEL), lambda b, m: (b, prev_block(m), 0))
    x_next = pl.BlockSpec((1, HALO, D_MODEL), lambda b, m: (b, next_block(m), 0))
    h_main = pl.BlockSpec((1, HEADS, rows, DH), lambda b, m: (b, 0, m, 0))
    h_prev = pl.BlockSpec((1, HEADS, HALO, DH), lambda b, m: (b, 0, prev_block(m), 0))
    h_next = pl.BlockSpec((1, HEADS, HALO, DH), lambda b, m: (b, 0, next_block(m), 0))

    def mod_spec(chunk):
        return pl.BlockSpec((1, 1, D_MODEL), lambda b, m: (b, 0, chunk))

    def const(shape, index=(0, 0)):
        return pl.BlockSpec(shape, lambda b, m: index, pipeline_mode=once)

    return pl.pallas_call(
        _ffn_kernel,
        grid=(batch, n_row_tiles),
        in_specs=[
            x_main, x_prev, x_next,
            h_main, h_prev, h_next,
            h_main, h_prev, h_next,
            mod_spec(2), mod_spec(3), mod_spec(4), mod_spec(5),
            const((GROUP_W, D_MODEL), (0, 0)),
            const((GROUP_W, D_MODEL), (1, 0)),
            const((1, D_MODEL)), const((1, D_MODEL)),
            const((D_MODEL, 2 * D_FF)),
            const((3, D_FF)), const((1, D_FF)),
            const((D_FF, D_MODEL)),
        ],
        out_specs=pl.BlockSpec((1, rows, D_MODEL), lambda b, m: (b, m, 0)),
        out_shape=jax.ShapeDtypeStruct((batch, seq, D_MODEL), F32),
        scratch_shapes=[
            pltpu.VMEM((rows + 2 * HALO, D_MODEL), BF16),
            pltpu.VMEM((rows + 2 * HALO, FF_TILE), F32),
            pltpu.VMEM((rows + 2 * HALO, FF_TILE), F32),
            pltpu.VMEM((rows, FF_TILE), F32),
            pltpu.VMEM((rows, FF_TILE), F32),
            pltpu.VMEM((rows, D_MODEL), F32),
        ],
        compiler_params=pltpu.CompilerParams(
            dimension_semantics=("parallel", "parallel"), vmem_limit_bytes=VMEM_LIMIT),
        name="out_proj_ffn",
    )(x, x, x, hgo, hgo, hgo, reto, reto, reto, mod3, mod3, mod3, mod3,
      w_out_bf16, w_out_bf16, norm2_g, final_g, w_up_bf16, conv_w, conv_b, w_down_bf16)


def _rope_tables(seq):
    quarter = DH // 4
    freqs = np.float32(ROPE_THETA) ** (-np.arange(quarter, dtype=np.float32) / np.float32(quarter))
    t = np.arange(seq)
    ang_r = (t // GRID_W).astype(np.float32)[:, None] * freqs[None, :]
    ang_c = (t % GRID_W).astype(np.float32)[:, None] * freqs[None, :]
    cos = np.concatenate([np.cos(ang_r)] * 2 + [np.cos(ang_c)] * 2, axis=-1)
    sin = np.concatenate([-np.sin(ang_r), np.sin(ang_r), -np.sin(ang_c), np.sin(ang_c)], axis=-1)
    return jnp.asarray(cos, F32), jnp.asarray(sin, F32)


def kernel(x, c, ctx, c_ctx, w_mod, b_mod, norm1_g, w_in, hgrn_lb, hgrn_norm_g, ret_decay,
           ret_norm_g, w_out, norm2_g, w_up, conv_w, conv_b, w_down, final_g):
    batch, seq, _ = x.shape
    n_ctx = ctx.shape[1]
    assert w_mod.shape[0] == 1, "single-layer block"
    assert seq % CHUNK == 0 and seq % ROW_TILE == 0 and (n_ctx & (n_ctx - 1)) == 0

    lb = jnp.cumsum(jax.nn.softmax(hgrn_lb.astype(F32), axis=1), axis=1)[:, 0]
    log_gamma = jax.nn.log_sigmoid(ret_decay[0].astype(F32))
    cos_t, sin_t = _rope_tables(seq)

    n_mod_rows = -(-(batch + 1) // 8) * 8
    c_rows = jnp.zeros((n_mod_rows, D_MODEL), F32).at[:batch].set(c).at[batch].set(c_ctx)
    mod = _modulation(c_rows, w_mod[0], b_mod[0][None, :])
    mod3 = mod.reshape(n_mod_rows, 1, 6 * D_MODEL)

    w_in_b = w_in[0].astype(BF16)
    norm1 = norm1_g[0][None, :]
    states = _context_states(ctx, mod3[batch:batch + 1], norm1, w_in_b, lb, log_gamma)
    p, phi = _in_projection(x.reshape(batch * seq, D_MODEL), mod3[:batch], seq, norm1, w_in_b,
                            cos_t, sin_t, lb)
    hgo, reto = _mixer(p, phi, log_gamma,
                       hgrn_norm_g[0][None, :], ret_norm_g[0][None, :], states)

    return _out_ffn(x, hgo, reto, mod3[:batch], w_out[0].astype(BF16), norm2_g[0][None, :],
                    final_g[None, :], w_up[0].astype(BF16), conv_w[0], conv_b[0][None, :],
                    w_down[0].astype(BF16))
```
